```python
import jax, jax.numpy as jnp
from jax import lax
import numpy as np

D_MODEL = 1024
BATCH = 4
SEQ = 4096
DEPTH = 2

GRID_W = 64
CTX_LEN = 256
ROPE_THETA = 10000.0
EPS = 1e-6
Q_BLOCK = 128
MLA_HEADS = 8
MLA_Q_RANK = 384
MLA_KV_RANK = 256
MLA_NOPE = 64
MLA_ROPE = 32
MLA_V = 64
GQA_HEADS = 8
GQA_KV_HEADS = 2
GQA_GROUP = GQA_HEADS // GQA_KV_HEADS
GQA_HEAD_DIM = 64
GLA_HEADS = 4
GLA_DK = D_MODEL // 2 // GLA_HEADS
GLA_DV = D_MODEL // GLA_HEADS
GLA_GATE_RANK = 16
GLA_GATE_NORM = 16.0
GLA_CHUNK = 32
FFN_DIM = 2816
N_EXPERTS = 8
TOP_K = 2
EXPERT_DIM = 3584

N_EVEN = (DEPTH + 1) // 2
N_ODD = DEPTH // 2

ATTN_WIDTHS = (MLA_Q_RANK, MLA_KV_RANK, MLA_ROPE, GQA_HEADS * GQA_HEAD_DIM,
               GQA_KV_HEADS * GQA_HEAD_DIM, GQA_KV_HEADS * GQA_HEAD_DIM)
ATTN_IN = MLA_Q_RANK + MLA_KV_RANK + MLA_ROPE + (GQA_HEADS + 2 * GQA_KV_HEADS) * GQA_HEAD_DIM
ATTN_MIX = MLA_HEADS * MLA_V + GQA_HEADS * GQA_HEAD_DIM
GLA_WIDTHS = (GLA_HEADS * GLA_DK, GLA_HEADS * GLA_DK, GLA_HEADS * GLA_DV, GLA_HEADS * GLA_DV,
              GLA_GATE_RANK, GLA_GATE_RANK)
GLA_IN = 2 * GLA_HEADS * GLA_DK + 2 * GLA_HEADS * GLA_DV + 2 * GLA_GATE_RANK
GLA_MIX = GLA_HEADS * GLA_DV

kernel_name = 'hybrid_mla_gqa_gla_moe_dit_block'


def rms_norm(x, g):
    x32 = x.astype(jnp.float32)
    y = x32 * lax.rsqrt(jnp.mean(x32 * x32, axis=-1, keepdims=True) + EPS)
    return (y * g).astype(x.dtype)


def modulate(h, shift, scale):
    return h * (1.0 + scale) + shift


def split_cols(z, widths):
    idx, acc = [], 0
    for w in widths[:-1]:
        acc += w
        idx.append(acc)
    return jnp.split(z, idx, axis=-1)


def to_heads(z, h):
    b, t, _ = z.shape
    return z.reshape(b, t, h, -1).transpose(0, 2, 1, 3)


def from_heads(o):
    b, h, t, d = o.shape
    return o.transpose(0, 2, 1, 3).reshape(b, t, h * d)


def rope_1d(x, pos):
    d = x.shape[-1]
    freqs = ROPE_THETA ** (-jnp.arange(0, d, 2, dtype=jnp.float32) / d)
    ang = pos.astype(jnp.float32)[:, None] * freqs[None, :]
    cos, sin = jnp.cos(ang), jnp.sin(ang)
    x32 = x.astype(jnp.float32)
    x1, x2 = x32[..., : d // 2], x32[..., d // 2:]
    return jnp.concatenate([x1 * cos - x2 * sin, x2 * cos + x1 * sin], axis=-1).astype(x.dtype)


def axial_rope(x, rows, cols):
    half = x.shape[-1] // 2
    return jnp.concatenate([rope_1d(x[..., :half], rows), rope_1d(x[..., half:], cols)], axis=-1)


def attend(q, k, v):
    s = jnp.einsum('bhgqd,bhkd->bhgqk', q, k, preferred_element_type=jnp.float32) * (q.shape[-1] ** -0.5)
    p = jax.nn.softmax(s, axis=-1)
    return jnp.einsum('bhgqk,bhkd->bhgqd', p.astype(v.dtype), v)


def blocked_attend(q, k, v):
    b, hk, g, t, d = q.shape
    nb = t // Q_BLOCK
    qb = jnp.moveaxis(q.reshape(b, hk, g, nb, Q_BLOCK, d), 3, 0)
    ob = lax.map(lambda qi: attend(qi, k, v), qb)
    return jnp.moveaxis(ob, 0, 3).reshape(b, hk, g, t, v.shape[-1])


def attn_project(u, w_in, q_a_norm, w_uq, kv_a_norm, w_ukv, qk_norm, rows, cols):
    b, t, _ = u.shape
    q_lat, kv_lat, k_rope, q_b, k_b, v_b = split_cols(u @ w_in, ATTN_WIDTHS)
    q_a = to_heads(rms_norm(q_lat, q_a_norm) @ w_uq, MLA_HEADS)
    kv_a = to_heads(rms_norm(kv_lat, kv_a_norm) @ w_ukv, MLA_HEADS)
    q_nope, q_rope = q_a[..., :MLA_NOPE], q_a[..., MLA_NOPE:]
    k_nope, v_a = kv_a[..., :MLA_NOPE], kv_a[..., MLA_NOPE:]
    k_rope = k_rope[:, None]
    q_b = rms_norm(to_heads(q_b, GQA_HEADS), qk_norm[0])
    k_b = rms_norm(to_heads(k_b, GQA_KV_HEADS), qk_norm[1])
    v_b = to_heads(v_b, GQA_KV_HEADS)
    if rows is not None:
        q_rope = axial_rope(q_rope, rows, cols)
        k_rope = axial_rope(k_rope, rows, cols)
        q_b = axial_rope(q_b, rows, cols)
        k_b = axial_rope(k_b, rows, cols)
    q_a = jnp.concatenate([q_nope, q_rope], axis=-1)[:, :, None]
    k_a = jnp.concatenate([k_nope, jnp.broadcast_to(k_rope, k_nope.shape[:-1] + (MLA_ROPE,))], axis=-1)
    q_b = q_b.reshape(b, GQA_KV_HEADS, GQA_GROUP, t, GQA_HEAD_DIM)
    return (q_a, k_a, v_a), (q_b, k_b, v_b)


def merge_heads(o_a, o_b):
    b, _, _, t, _ = o_a.shape
    o_a = o_a.transpose(0, 3, 1, 2, 4).reshape(b, t, -1)
    o_b = o_b.transpose(0, 3, 1, 2, 4).reshape(b, t, -1)
    return jnp.concatenate([o_a, o_b], axis=-1)


def attn_mixer(u_ctx, u_lat, w_in, q_a_norm, w_uq, kv_a_norm, w_ukv, qk_norm, w_out, rows, cols, ctx_out):
    (qa_c, ka_c, va_c), (qb_c, kb_c, vb_c) = attn_project(u_ctx, w_in, q_a_norm, w_uq, kv_a_norm, w_ukv, qk_norm, None, None)
    (qa_l, ka_l, va_l), (qb_l, kb_l, vb_l) = attn_project(u_lat, w_in, q_a_norm, w_uq, kv_a_norm, w_ukv, qk_norm, rows, cols)
    o_a = blocked_attend(qa_l, jnp.concatenate([ka_c, ka_l], axis=2), jnp.concatenate([va_c, va_l], axis=2))
    o_b = blocked_attend(qb_l, jnp.concatenate([kb_c, kb_l], axis=2), jnp.concatenate([vb_c, vb_l], axis=2))
    y_lat = merge_heads(o_a, o_b) @ w_out
    y_ctx = None
    if ctx_out:
        y_ctx = merge_heads(attend(qa_c, ka_c, va_c), attend(qb_c, kb_c, vb_c)) @ w_out
    return y_ctx, y_lat


def gla_chunked(q, k, v, log_a, s0):
    b, h, t, dk = q.shape
    dv = v.shape[-1]
    n = t // GLA_CHUNK

    def chunks(a):
        return jnp.moveaxis(a.astype(jnp.float32).reshape(b, h, n, GLA_CHUNK, a.shape[-1]), 2, 0)

    qc, kc, vc, ac = chunks(q), chunks(k), chunks(v), chunks(log_a)
    cum = jnp.cumsum(ac, axis=-2)
    cum_last = cum[..., -1:, :]
    q_dec = qc * jnp.exp(cum)
    k_intra = kc * jnp.exp(-cum)
    k_state = kc * jnp.exp(cum_last - cum)
    mask = jnp.tril(jnp.ones((GLA_CHUNK, GLA_CHUNK), jnp.float32))
    scores = jnp.einsum('nbhtk,nbhsk->nbhts', q_dec, k_intra) * mask
    o_intra = jnp.einsum('nbhts,nbhsv->nbhtv', scores, vc)

    def step(state, xs):
        q_n, k_n, v_n, cl_n = xs
        o_n = jnp.einsum('bhtk,bhkv->bhtv', q_n, state)
        state = jnp.exp(cl_n[..., 0, :])[..., None] * state + jnp.einsum('bhtk,bhtv->bhkv', k_n, v_n)
        return state, o_n

    s_final, o_inter = lax.scan(step, s0, (q_dec, k_state, vc, cum_last))
    o = jnp.moveaxis(o_intra + o_inter, 0, 2).reshape(b, h, t, dv)
    return o.astype(v.dtype), s_final


def gla_final_state(k, v, log_a):
    a = log_a.astype(jnp.float32)
    cum = jnp.cumsum(a, axis=-2)
    tail = cum[..., -1:, :] - cum
    return jnp.einsum('bhtk,bhtv->bhkv', k.astype(jnp.float32) * jnp.exp(tail), v.astype(jnp.float32))


def gla_project(u, w_in, w_gate2, b_gate):
    q, k, v, g, r_f, r_b = split_cols(u @ w_in, GLA_WIDTHS)
    q = to_heads(q, GLA_HEADS) * (GLA_DK ** -0.5)
    k = to_heads(k, GLA_HEADS)
    v = to_heads(v, GLA_HEADS)

    def log_decay(r, d):
        z = (r @ w_gate2[d] + b_gate[d]).astype(jnp.float32)
        return to_heads(jax.nn.log_sigmoid(z) / GLA_GATE_NORM, GLA_HEADS)

    return q, k, v, g, log_decay(r_f, 0), log_decay(r_b, 1)


def flip_t(a):
    return a[:, :, ::-1, :]


def gla_mixer(u_ctx, u_lat, w_in, w_gate2, b_gate, o_norm, w_out, ctx_out):
    qc, kc, vc, gc, afc, abc = gla_project(u_ctx, w_in, w_gate2, b_gate)
    ql, kl, vl, gl, afl, abl = gla_project(u_lat, w_in, w_gate2, b_gate)

    def finish(o, g):
        return (from_heads(rms_norm(o, o_norm)) * jax.nn.silu(g)) @ w_out

    y_ctx = None
    if ctx_out:
        zeros = jnp.zeros((qc.shape[0], GLA_HEADS, GLA_DK, GLA_DV), jnp.float32)
        oc_f, s_f = gla_chunked(qc, kc, vc, afc, zeros)
        oc_b, s_b = gla_chunked(flip_t(qc), flip_t(kc), flip_t(vc), flip_t(abc), zeros)
        y_ctx = finish(oc_f + flip_t(oc_b), gc)
    else:
        s_f = gla_final_state(kc, vc, afc)
        s_b = gla_final_state(flip_t(kc), flip_t(vc), flip_t(abc))
    ol_f, _ = gla_chunked(ql, kl, vl, afl, s_f)
    ol_b, _ = gla_chunked(flip_t(ql), flip_t(kl), flip_t(vl), flip_t(abl), s_b)
    y_lat = finish(ol_f + flip_t(ol_b), gl)
    return y_ctx, y_lat


def swiglu(u, w_gate, w_up, w_down):
    return (jax.nn.silu(u @ w_gate) * (u @ w_up)) @ w_down


def moe_swiglu(u, w_router, b_router, w_gate, w_up, w_down):
    logits = (u @ w_router).astype(jnp.float32) + b_router
    probs = jax.nn.softmax(logits, axis=-1)
    top_p, top_i = lax.top_k(probs, TOP_K)
    top_p = top_p / jnp.sum(top_p, axis=-1, keepdims=True)
    combine = jnp.einsum('btk,btke->bte', top_p, jax.nn.one_hot(top_i, N_EXPERTS, dtype=jnp.float32)).astype(u.dtype)
    y = jnp.zeros_like(u)
    for e in range(N_EXPERTS):
        y = y + combine[..., e:e + 1] * swiglu(u, w_gate[e], w_up[e], w_down[e])
    return y


def setup_inputs(seed: int = 0) -> dict:
    key = jax.random.key(seed)
    ks = iter(jax.random.split(key, 32))
    f32 = jnp.float32
    D = D_MODEL

    def normal(shape, scale=1.0):
        return jax.random.normal(next(ks), shape, f32) * scale

    def w(shape, fan_in, scale=1.0):
        return normal(shape, scale * fan_in ** -0.5)

    def gain(shape):
        return 1.0 + normal(shape, 0.05)

    return {
        'x': normal((BATCH, SEQ, D)),
        'c': normal((BATCH, D)),
        'ctx': normal((BATCH, CTX_LEN, D)),
        'c_ctx': normal((D,)),
        'mod_w': w((DEPTH, D, 6 * D), D, 0.5),
        'mod_b': normal((DEPTH, 6 * D), 0.02),
        'norm_g': gain((DEPTH, 4, D)),
        'attn_w_in': w((N_EVEN, D, ATTN_IN), D),
        'attn_q_norm': gain((N_EVEN, MLA_Q_RANK)),
        'attn_w_uq': w((N_EVEN, MLA_Q_RANK, MLA_HEADS * (MLA_NOPE + MLA_ROPE)), MLA_Q_RANK),
        'attn_kv_norm': gain((N_EVEN, MLA_KV_RANK)),
        'attn_w_ukv': w((N_EVEN, MLA_KV_RANK, MLA_HEADS * (MLA_NOPE + MLA_V)), MLA_KV_RANK),
        'attn_qk_norm': gain((N_EVEN, 2, GQA_HEAD_DIM)),
        'attn_w_out': w((N_EVEN, ATTN_MIX, D), ATTN_MIX),
        'gla_w_in': w((N_ODD, D, GLA_IN), D),
        'gla_w_gate2': w((N_ODD, 2, GLA_GATE_RANK, GLA_HEADS * GLA_DK), GLA_GATE_RANK),
        'gla_b_gate': normal((N_ODD, 2, GLA_HEADS * GLA_DK), 0.1),
        'gla_o_norm': gain((N_ODD, GLA_DV)),
        'gla_w_out': w((N_ODD, GLA_MIX, D), GLA_MIX),
        'ffn_w_gate': w((N_EVEN, D, FFN_DIM), D),
        'ffn_w_up': w((N_EVEN, D, FFN_DIM), D),
        'ffn_w_down': w((N_EVEN, FFN_DIM, D), FFN_DIM),
        'moe_w_router': w((N_ODD, D, N_EXPERTS), D),
        'moe_b_router': normal((N_ODD, N_EXPERTS), 0.01),
        'moe_w_gate': w((N_ODD, N_EXPERTS, D, EXPERT_DIM), D),
        'moe_w_up': w((N_ODD, N_EXPERTS, D, EXPERT_DIM), D),
        'moe_w_down': w((N_ODD, N_EXPERTS, EXPERT_DIM, D), EXPERT_DIM),
    }


def reference(x, c, ctx, c_ctx, mod_w, mod_b, norm_g, attn_w_in, attn_q_norm, attn_w_uq, attn_kv_norm,
              attn_w_ukv, attn_qk_norm, attn_w_out, gla_w_in, gla_w_gate2, gla_b_gate, gla_o_norm, gla_w_out,
              ffn_w_gate, ffn_w_up, ffn_w_down, moe_w_router, moe_b_router, moe_w_gate, moe_w_up, moe_w_down):
    t = x.shape[1]
    ROWS = t // GRID_W
    rows = jnp.repeat(jnp.arange(ROWS, dtype=jnp.int32), GRID_W)
    cols = jnp.tile(jnp.arange(GRID_W, dtype=jnp.int32), ROWS)

    def channel_mixer(u, layer):
        i = layer // 2
        if layer % 2 == 0:
            return swiglu(u, ffn_w_gate[i], ffn_w_up[i], ffn_w_down[i])
        return moe_swiglu(u, moe_w_router[i], moe_b_router[i], moe_w_gate[i], moe_w_up[i], moe_w_down[i])

    h_lat, h_ctx = x, ctx
    for layer in range(DEPTH):
        i = layer // 2
        last = layer == DEPTH - 1
        g = norm_g[layer]
        m_lat = [m[:, None, :] for m in jnp.split(jax.nn.silu(c) @ mod_w[layer] + mod_b[layer], 6, axis=-1)]
        m_ctx = jnp.split(jax.nn.silu(c_ctx) @ mod_w[layer] + mod_b[layer], 6, axis=-1)
        u_lat = modulate(rms_norm(h_lat, g[0]), m_lat[0], m_lat[1])
        u_ctx = modulate(rms_norm(h_ctx, g[0]), m_ctx[0], m_ctx[1])
        if layer % 2 == 0:
            y_ctx, y_lat = attn_mixer(u_ctx, u_lat, attn_w_in[i], attn_q_norm[i], attn_w_uq[i], attn_kv_norm[i],
                                      attn_w_ukv[i], attn_qk_norm[i], attn_w_out[i], rows, cols, not last)
        else:
            y_ctx, y_lat = gla_mixer(u_ctx, u_lat, gla_w_in[i], gla_w_gate2[i], gla_b_gate[i], gla_o_norm[i],
                                     gla_w_out[i], not last)
        h_lat = h_lat + m_lat[2] * rms_norm(y_lat, g[1])
        u_lat = modulate(rms_norm(h_lat, g[2]), m_lat[3], m_lat[4])
        h_lat = h_lat + m_lat[5] * rms_norm(channel_mixer(u_lat, layer), g[3])
        if not last:
            h_ctx = h_ctx + m_ctx[2] * rms_norm(y_ctx, g[1])
            u_ctx = modulate(rms_norm(h_ctx, g[2]), m_ctx[3], m_ctx[4])
            h_ctx = h_ctx + m_ctx[5] * rms_norm(channel_mixer(u_ctx, layer), g[3])
    return h_lat
```

```python
import functools

import jax
import jax.numpy as jnp
from jax import lax
from jax.experimental import pallas as pl
from jax.experimental.pallas import tpu as pltpu

F32 = jnp.float32
BF16 = jnp.bfloat16

D = 1024
BATCH = 4
SEQ = 4096
CTX = 256
GRID_W = 64
ROPE_THETA = 10000.0
EPS = 1e-6

N_LAT = BATCH * SEQ
N_CTX = BATCH * CTX
N_ROWS = N_LAT + N_CTX
MOD_ROWS = 8
CTX_MOD_ROW = BATCH

LANE = 128
ROW_TILE = 256

MLA_HEADS = 8
MLA_Q_RANK = 384
MLA_KV_RANK = 256
MLA_NOPE = 64
MLA_ROPE = 32
MLA_V = 64
GQA_HEADS = 8
GQA_KV_HEADS = 2
GQA_GROUP = GQA_HEADS // GQA_KV_HEADS
GQA_HEAD_DIM = 64
N_Q_HEADS = MLA_HEADS + GQA_HEADS
N_KV_HEADS = MLA_HEADS + GQA_KV_HEADS

GLA_HEADS = 4
GLA_DK = 128
GLA_DV = 256
GLA_GATE_RANK = 16
GLA_GATE_NORM = 16.0
GLA_BLOCK = 128
GLA_SUB = 32

FFN_DIM = 2816
FFN_TILE = 1408
N_EXPERTS = 8
EXPERT_DIM = 3584
EXPERT_TILE = 896
MLP_ROW_TILE = 512

VMEM_LIMIT = 56 * 1024 * 1024


def _params(sem):
    return pltpu.CompilerParams(dimension_semantics=sem, vmem_limit_bytes=VMEM_LIMIT)


def _rms(x, g):
    return x * lax.rsqrt(jnp.mean(x * x, axis=-1, keepdims=True) + EPS) * g


def _silu(x):
    return x / (1.0 + jnp.exp(-x))


def _split_bf16(x):
    hi = x.astype(BF16)
    lo = (x - hi.astype(F32)).astype(BF16)
    return hi, lo


def _dot(a, b):
    return jnp.dot(a, b, preferred_element_type=F32)


def _dot_nt(a, b):
    return lax.dot_general(a, b, (((1,), (1,)), ((), ())), preferred_element_type=F32)


def _dot3(a, b_hi, b_lo):
    a_hi, a_lo = _split_bf16(a)
    return _dot(a_hi, b_hi) + (_dot(a_hi, b_lo) + _dot(a_lo, b_hi))


def _modulated(h, g_row, shift, scale):
    return _rms(h, g_row) * (1.0 + scale) + shift


def _rope(x, cos, sa, sb, p):
    return x * cos + pltpu.roll(x, LANE - p, 1) * sa + pltpu.roll(x, p, 1) * sb


def _mod_row(i, tile):
    r0 = i * tile
    return jnp.where(r0 < N_LAT, r0 // SEQ, CTX_MOD_ROW)


def _mod_spec(layer, tile):
    return pl.BlockSpec((None, None, 6, D), lambda i, *_: (layer, _mod_row(i, tile), 0, 0))


def _rope_block(i):
    r0 = i * ROW_TILE
    return jnp.where(r0 < N_LAT, (r0 % SEQ) // ROW_TILE, SEQ // ROW_TILE)


def _full(shape):
    return pl.BlockSpec(shape, lambda *_: (0,) * len(shape))


def _mod_kernel(c_ref, w_ref, b_ref, o_ref):
    w_hi, w_lo = _split_bf16(w_ref[...])
    o_ref[...] = _dot3(_silu(c_ref[...]), w_hi, w_lo) + b_ref[...]


def _mod_vectors(cc, mod_w, mod_b):
    depth, _, n = mod_w.shape
    tn = 1536
    return pl.pallas_call(
        _mod_kernel,
        out_shape=jax.ShapeDtypeStruct((depth, MOD_ROWS, n), F32),
        grid=(depth, n // tn),
        in_specs=[
            pl.BlockSpec((MOD_ROWS, D), lambda l, j: (0, 0)),
            pl.BlockSpec((None, D, tn), lambda l, j: (l, 0, j)),
            pl.BlockSpec((None, 1, tn), lambda l, j: (l, 0, j)),
        ],
        out_specs=pl.BlockSpec((None, MOD_ROWS, tn), lambda l, j: (l, 0, j)),
        compiler_params=_params(("parallel", "parallel")),
        name="mod_vectors",
    )(cc, mod_w, mod_b.reshape(depth, 1, n))


Q_LAT0, KV_LAT0, K_ROPE0 = 0, 384, 640
Q_B0 = 768
K_B0 = Q_B0 + GQA_HEADS * LANE
V_B0 = K_B0 + GQA_KV_HEADS * LANE
ATTN_IN_P = V_B0 + GQA_KV_HEADS * LANE


def _attn_proj_kernel(h_ref, m_ref, g_ref, w_in_ref, qn_ref, w_uq_ref, kvn_ref, w_uk_ref, w_uv_ref,
                      qkn_ref, ca_ref, saa_ref, sba_ref, cb_ref, sab_ref, sbb_ref,
                      q_ref, k_ref, v_ref):
    u = _modulated(h_ref[...], g_ref[0:1, :], m_ref[0:1, :], m_ref[1:2, :]).astype(BF16)
    z = _dot(u, w_in_ref[...])
    q_lat = _rms(z[:, Q_LAT0:Q_LAT0 + MLA_Q_RANK], qn_ref[...]).astype(BF16)
    kv_lat = _rms(z[:, KV_LAT0:KV_LAT0 + MLA_KV_RANK], kvn_ref[...]).astype(BF16)
    q_a = _dot(q_lat, w_uq_ref[...])
    k_a = _dot(kv_lat, w_uk_ref[...])
    v_a = _dot(kv_lat, w_uv_ref[...])
    ca, saa, sba = ca_ref[...], saa_ref[...], sba_ref[...]
    cb, sab, sbb = cb_ref[...], sab_ref[...], sbb_ref[...]
    pa, pb = MLA_ROPE // 4, GQA_HEAD_DIM // 4
    scale_a = (MLA_NOPE + MLA_ROPE) ** -0.5
    scale_b = GQA_HEAD_DIM ** -0.5
    k_rope = _rope(z[:, K_ROPE0:K_ROPE0 + LANE], ca, saa, sba, pa)
    for hd in range(MLA_HEADS):
        sl = slice(hd * LANE, (hd + 1) * LANE)
        q_ref[:, sl] = (_rope(q_a[:, sl], ca, saa, sba, pa) * scale_a).astype(BF16)
        k_ref[:, sl] = (k_a[:, sl] + k_rope).astype(BF16)
    v_ref[:, 0:MLA_HEADS * LANE] = v_a.astype(BF16)

    def head_norm(x, gain):
        ms = jnp.sum(x * x, axis=-1, keepdims=True) * (1.0 / GQA_HEAD_DIM)
        return x * lax.rsqrt(ms + EPS) * gain

    for hd in range(GQA_HEADS):
        x = head_norm(z[:, Q_B0 + hd * LANE:Q_B0 + (hd + 1) * LANE], qkn_ref[0:1, :])
        sl = slice((MLA_HEADS + hd) * LANE, (MLA_HEADS + hd + 1) * LANE)
        q_ref[:, sl] = (_rope(x, cb, sab, sbb, pb) * scale_b).astype(BF16)
    for hd in range(GQA_KV_HEADS):
        x = head_norm(z[:, K_B0 + hd * LANE:K_B0 + (hd + 1) * LANE], qkn_ref[1:2, :])
        sl = slice((MLA_HEADS + hd) * LANE, (MLA_HEADS + hd + 1) * LANE)
        k_ref[:, sl] = _rope(x, cb, sab, sbb, pb).astype(BF16)
    v_ref[:, MLA_HEADS * LANE:] = z[:, V_B0:].astype(BF16)


def _attn_project(h, mods, g, w, tables):
    tm = ROW_TILE
    row = lambda n: pl.BlockSpec((tm, n), lambda i: (i, 0))
    tab = pl.BlockSpec((tm, LANE), lambda i: (_rope_block(i), 0))
    return pl.pallas_call(
        _attn_proj_kernel,
        out_shape=(jax.ShapeDtypeStruct((N_ROWS, N_Q_HEADS * LANE), BF16),
                   jax.ShapeDtypeStruct((N_ROWS, N_KV_HEADS * LANE), BF16),
                   jax.ShapeDtypeStruct((N_ROWS, N_KV_HEADS * LANE), BF16)),
        grid=(N_ROWS // tm,),
        in_specs=[row(D), _mod_spec(0, tm), _full((4, D)), _full((D, ATTN_IN_P)),
                  _full((1, MLA_Q_RANK)), _full((MLA_Q_RANK, MLA_HEADS * LANE)),
                  _full((1, MLA_KV_RANK)), _full((MLA_KV_RANK, MLA_HEADS * LANE)),
                  _full((MLA_KV_RANK, MLA_HEADS * LANE)), _full((2, LANE)),
                  tab, tab, tab, tab, tab, tab],
        out_specs=(row(N_Q_HEADS * LANE), row(N_KV_HEADS * LANE), row(N_KV_HEADS * LANE)),
        compiler_params=_params(("parallel",)),
        name="attn_project",
    )(h, mods, g, w["w_in"], w["q_norm"], w["w_uq"], w["kv_norm"], w["w_uk"], w["w_uv"], w["qk_norm"],
      *tables)


ATT_TQ = 256
ATT_TK = 256
ATT_LAT_TILES = SEQ // ATT_TQ


def _attn_kernel(q_ref, kc_ref, kl_ref, vc_ref, vl_ref, o_ref):
    qi = pl.program_id(2)
    q = q_ref[...]

    def chunk(k, v, carry):
        m, l, acc = carry
        s = _dot_nt(q, k)
        m_new = jnp.maximum(m, jnp.max(s, axis=-1, keepdims=True))
        alpha = jnp.exp(m - m_new)
        p = jnp.exp(s - m_new)
        l = alpha * l + jnp.sum(p, axis=-1, keepdims=True)
        acc = alpha * acc + _dot(p.astype(BF16), v)
        return m_new, l, acc

    init = (jnp.full((ATT_TQ, 1), -jnp.inf, F32), jnp.zeros((ATT_TQ, 1), F32), jnp.zeros((ATT_TQ, LANE), F32))
    carry = chunk(kc_ref[...], vc_ref[...], init)

    def body(j, c):
        off = pl.multiple_of(j * ATT_TK, ATT_TK)
        return chunk(kl_ref[pl.ds(off, ATT_TK), :], vl_ref[pl.ds(off, ATT_TK), :], c)

    n_lat_chunks = jnp.where(qi < ATT_LAT_TILES, SEQ // ATT_TK, 0)
    _, l, acc = lax.fori_loop(0, n_lat_chunks, body, carry)
    o_ref[...] = (acc / l).astype(BF16)


def _kv_head(hd):
    return jnp.where(hd < MLA_HEADS, hd, MLA_HEADS + (hd - MLA_HEADS) // GQA_GROUP)


def _attention(q, k, v):
    def q_map(b, hd, qi):
        return (jnp.where(qi < ATT_LAT_TILES, b * ATT_LAT_TILES + qi, N_LAT // ATT_TQ + b), hd)

    lat = pl.BlockSpec((SEQ, LANE), lambda b, hd, qi: (b, _kv_head(hd)))
    ctx = pl.BlockSpec((CTX, LANE), lambda b, hd, qi: (N_LAT // CTX + b, _kv_head(hd)))
    return pl.pallas_call(
        _attn_kernel,
        out_shape=jax.ShapeDtypeStruct((N_ROWS, N_Q_HEADS * LANE), BF16),
        grid=(BATCH, N_Q_HEADS, ATT_LAT_TILES + 1),
        in_specs=[pl.BlockSpec((ATT_TQ, LANE), q_map), ctx, lat, ctx, lat],
        out_specs=pl.BlockSpec((ATT_TQ, LANE), q_map),
        compiler_params=_params(("parallel", "parallel", "arbitrary")),
        name="attention",
    )(q, k, k, v, v)


def _mixer_epilogue(y, h, m_ref, g_ref):
    h_new = h + m_ref[2:3, :] * _rms(y, g_ref[1:2, :])
    u = _modulated(h_new, g_ref[2:3, :], m_ref[3:4, :], m_ref[4:5, :])
    return h_new, u


def _attn_out_kernel(o_ref, w_ref, h_ref, m_ref, g_ref, h_out_ref, u_ref):
    y = _dot(o_ref[...], w_ref[...])
    h_new, u = _mixer_epilogue(y, h_ref[...], m_ref, g_ref)
    h_out_ref[...] = h_new
    u_ref[...] = u.astype(BF16)


def _attn_output(o, w_out, h, mods, g):
    tm = ROW_TILE
    row = lambda n: pl.BlockSpec((tm, n), lambda i: (i, 0))
    return pl.pallas_call(
        _attn_out_kernel,
        out_shape=(jax.ShapeDtypeStruct((N_ROWS, D), F32), jax.ShapeDtypeStruct((N_ROWS, D), BF16)),
        grid=(N_ROWS // tm,),
        in_specs=[row(N_Q_HEADS * LANE), _full((N_Q_HEADS * LANE, D)), row(D), _mod_spec(0, tm), _full((4, D))],
        out_specs=(row(D), row(D)),
        compiler_params=_params(("parallel",)),
        name="attn_output",
    )(o, w_out, h, mods, g)


def _mlp_kernel(x_ref, wg_ref, wu_ref, wd_ref, c_ref, h_ref, m_ref, g_ref, o_ref, acc_ref, *, gated):
    e, f = pl.program_id(1), pl.program_id(2)

    @pl.when((e == 0) & (f == 0))
    def _():
        acc_ref[...] = jnp.zeros_like(acc_ref)

    x = x_ref[...]
    a = _silu(_dot(x, wg_ref[...])) * _dot(x, wu_ref[...])
    y = _dot(a.astype(BF16), wd_ref[...])
    if gated:
        lane = lax.broadcasted_iota(jnp.int32, c_ref.shape, 1)
        y = jnp.sum(jnp.where(lane == e, c_ref[...], 0.0), axis=-1, keepdims=True) * y
    acc_ref[...] += y

    @pl.when((e == pl.num_programs(1) - 1) & (f == pl.num_programs(2) - 1))
    def _():
        o_ref[...] = h_ref[...] + m_ref[5:6, :] * _rms(acc_ref[...], g_ref[3:4, :])


def _mlp(x, wg, wu, wd, combine, h, mods, g, layer, n_rows, f_tile):
    n_e, _, f_dim = wg.shape
    tm = MLP_ROW_TILE
    row = lambda n: pl.BlockSpec((tm, n), lambda i, e, f: (i, 0))
    gated = combine is not None
    if not gated:
        combine = jnp.zeros((tm, LANE), F32)
        c_spec = pl.BlockSpec((tm, LANE), lambda i, e, f: (0, 0))
    else:
        c_spec = row(LANE)
    return pl.pallas_call(
        functools.partial(_mlp_kernel, gated=gated),
        out_shape=jax.ShapeDtypeStruct((n_rows, D), F32),
        grid=(n_rows // tm, n_e, f_dim // f_tile),
        in_specs=[row(D),
                  pl.BlockSpec((None, D, f_tile), lambda i, e, f: (e, 0, f)),
                  pl.BlockSpec((None, D, f_tile), lambda i, e, f: (e, 0, f)),
                  pl.BlockSpec((None, f_tile, D), lambda i, e, f: (e, f, 0)),
                  c_spec, row(D), _mod_spec(layer, tm), _full((4, D))],
        out_specs=row(D),
        scratch_shapes=[pltpu.VMEM((tm, D), F32)],
        compiler_params=_params(("parallel", "arbitrary", "arbitrary")),
        name="moe_mlp" if gated else "ffn_mlp",
    )(x, wg, wu, wd, combine, h, mods, g)


GLA_Q0, GLA_K0, GLA_V0, GLA_G0, GLA_R0 = 0, 512, 1024, 2048, 3072
GLA_IN_P = GLA_R0 + LANE


def _gla_proj_kernel(h_ref, m_ref, g_ref, w_in_ref, w_vt_ref, w_gate_ref, b_gate_ref,
                     q_ref, k_ref, v_ref, vt_ref, sg_ref, laf_ref, lab_ref):
    u = _modulated(h_ref[...], g_ref[0:1, :], m_ref[0:1, :], m_ref[1:2, :]).astype(BF16)
    z = _dot(u, w_in_ref[...])
    q_ref[...] = (z[:, GLA_Q0:GLA_K0] * (GLA_DK ** -0.5)).astype(BF16)
    k_ref[...] = z[:, GLA_K0:GLA_V0].astype(BF16)
    v_ref[...] = z[:, GLA_V0:GLA_G0].astype(BF16)
    vt_ref[...] = _dot_nt(w_vt_ref[...], u).astype(BF16)
    sg_ref[...] = _silu(z[:, GLA_G0:GLA_R0]).astype(BF16)
    zg = _dot(z[:, GLA_R0:].astype(BF16), w_gate_ref[...]) + b_gate_ref[...]
    la = (jnp.minimum(zg, 0.0) - jnp.log(1.0 + jnp.exp(-jnp.abs(zg)))) * (1.0 / GLA_GATE_NORM)
    laf_ref[...] = la[:, :GLA_HEADS * GLA_DK]
    lab_ref[...] = la[:, GLA_HEADS * GLA_DK:]


def _gla_project(h, mods, g, w):
    tm = ROW_TILE
    row = lambda n, dt=None: pl.BlockSpec((tm, n), lambda i: (i, 0))
    hk, hv = GLA_HEADS * GLA_DK, GLA_HEADS * GLA_DV
    sds = jax.ShapeDtypeStruct
    return pl.pallas_call(
        _gla_proj_kernel,
        out_shape=(sds((N_ROWS, hk), BF16), sds((N_ROWS, hk), BF16), sds((N_ROWS, hv), BF16),
                   sds((hv, N_ROWS), BF16), sds((N_ROWS, hv), BF16),
                   sds((N_ROWS, hk), F32), sds((N_ROWS, hk), F32)),
        grid=(N_ROWS // tm,),
        in_specs=[row(D), _mod_spec(1, tm), _full((4, D)), _full((D, GLA_IN_P)), _full((hv, D)),
                  _full((LANE, 2 * hk)), _full((1, 2 * hk))],
        out_specs=(row(hk), row(hk), row(hv), pl.BlockSpec((hv, tm), lambda i: (0, i)), row(hv),
                   row(hk), row(hk)),
        compiler_params=_params(("parallel",)),
        name="gla_project",
    )(h, mods, g, w["w_in"], w["w_vt"], w["w_gate"], w["b_gate"])


GLA_LAT_BLOCKS = SEQ // GLA_BLOCK
GLA_CTX_BLOCKS = CTX // GLA_BLOCK
GLA_STRIPS = GLA_BLOCK // GLA_SUB


def _gla_block(q, k, v, vt, la, state, reverse):
    n = GLA_BLOCK
    r = lax.broadcasted_iota(jnp.int32, (n, n), 0)
    c = lax.broadcasted_iota(jnp.int32, (n, n), 1)
    keep = (c >= r) if reverse else (c <= r)
    tri = jnp.where(keep, 1.0, 0.0).astype(BF16)
    la_hi, la_lo = _split_bf16(la)
    cum = _dot(tri, la_hi) + _dot(tri, la_lo)
    total = cum[0:1, :] if reverse else cum[n - 1:n, :]
    k_state = (k * jnp.exp(total - cum)).astype(BF16)
    new_state = state * jnp.exp(total) + _dot(vt, k_state)
    if q is None:
        return None, new_state
    o = _dot_nt((q * jnp.exp(cum)).astype(BF16), state.astype(BF16))
    row = lax.broadcasted_iota(jnp.int32, (n, GLA_DK), 0)
    strips = []
    for i in range(GLA_STRIPS):
        lo, hi = i * GLA_SUB, (i + 1) * GLA_SUB
        if reverse:
            ref = cum[hi:hi + 1, :] if i < GLA_STRIPS - 1 else jnp.zeros((1, GLA_DK), F32)
            live = row >= lo
        else:
            ref = cum[lo - 1:lo, :] if i > 0 else jnp.zeros((1, GLA_DK), F32)
            live = row < hi
        q_loc = (q[lo:hi, :] * jnp.exp(cum[lo:hi, :] - ref)).astype(BF16)
        k_loc = jnp.where(live, k * jnp.exp(ref - cum), 0.0).astype(BF16)
        strips.append(_dot_nt(q_loc, k_loc))
    scores = jnp.where(keep, jnp.concatenate(strips, axis=0), 0.0).astype(BF16)
    return o + _dot(scores, v), new_state


def _gla_kernel(q_ref, kl_ref, kc_ref, vl_ref, vc_ref, vtl_ref, vtc_ref,
                lafl_ref, lafc_ref, labl_ref, labc_ref, o_ref, sf_ref, sb_ref):
    n = GLA_BLOCK
    sf_ref[...] = jnp.zeros_like(sf_ref)
    sb_ref[...] = jnp.zeros_like(sb_ref)

    for j in range(GLA_CTX_BLOCKS):
        fs = slice(j * n, (j + 1) * n)
        bs = slice((GLA_CTX_BLOCKS - 1 - j) * n, (GLA_CTX_BLOCKS - j) * n)
        _, sf = _gla_block(None, kc_ref[fs, :].astype(F32), None, vtc_ref[:, fs], lafc_ref[fs, :],
                           sf_ref[...], False)
        sf_ref[...] = sf
        _, sb = _gla_block(None, kc_ref[bs, :].astype(F32), None, vtc_ref[:, bs], labc_ref[bs, :],
                           sb_ref[...], True)
        sb_ref[...] = sb

    def step(j, accumulate):
        for reverse, la_ref, s_ref in ((False, lafl_ref, sf_ref), (True, labl_ref, sb_ref)):
            blk = (GLA_LAT_BLOCKS - 1 - j) if reverse else j
            off = pl.multiple_of(blk * n, n)
            rows = pl.ds(off, n)
            o, s_new = _gla_block(q_ref[rows, :].astype(F32), kl_ref[rows, :].astype(F32), vl_ref[rows, :],
                                  vtl_ref[:, rows], la_ref[rows, :], s_ref[...], reverse)
            s_ref[...] = s_new
            if accumulate:
                o_ref[rows, :] += o
            else:
                o_ref[rows, :] = o

    half = GLA_LAT_BLOCKS // 2

    def first(j, _):
        step(j, False)
        return 0

    def second(j, _):
        step(j, True)
        return 0

    lax.fori_loop(0, half, first, 0)
    lax.fori_loop(half, GLA_LAT_BLOCKS, second, 0)


def _gla_scan(q, k, v, vt, laf, lab):
    lat = lambda n: pl.BlockSpec((SEQ, n), lambda b, hd: (b, hd))
    ctx = lambda n: pl.BlockSpec((CTX, n), lambda b, hd: (N_LAT // CTX + b, hd))
    return pl.pallas_call(
        _gla_kernel,
        out_shape=jax.ShapeDtypeStruct((N_LAT, GLA_HEADS * GLA_DV), F32),
        grid=(BATCH, GLA_HEADS),
        in_specs=[lat(GLA_DK), lat(GLA_DK), ctx(GLA_DK), lat(GLA_DV), ctx(GLA_DV),
                  pl.BlockSpec((GLA_DV, SEQ), lambda b, hd: (hd, b)),
                  pl.BlockSpec((GLA_DV, CTX), lambda b, hd: (hd, N_LAT // CTX + b)),
                  lat(GLA_DK), ctx(GLA_DK), lat(GLA_DK), ctx(GLA_DK)],
        out_specs=lat(GLA_DV),
        scratch_shapes=[pltpu.VMEM((GLA_DV, GLA_DK), F32), pltpu.VMEM((GLA_DV, GLA_DK), F32)],
        compiler_params=_params(("parallel", "parallel")),
        name="gla_scan",
    )(q, k, k, v, v, vt, vt, laf, laf, lab, lab)


def _gla_out_kernel(o_ref, sg_ref, on_ref, w_ref, h_ref, m_ref, g_ref, wr_hi_ref, wr_lo_ref, br_ref,
                    h_out_ref, u_ref, comb_ref):
    parts = []
    for hd in range(GLA_HEADS):
        sl = slice(hd * GLA_DV, (hd + 1) * GLA_DV)
        parts.append(_rms(o_ref[:, sl], on_ref[...]) * sg_ref[:, sl].astype(F32))
    y = _dot(jnp.concatenate(parts, axis=-1).astype(BF16), w_ref[...])
    h_new, u = _mixer_epilogue(y, h_ref[...], m_ref, g_ref)
    h_out_ref[...] = h_new
    u_ref[...] = u.astype(BF16)
    logits = _dot3(u, wr_hi_ref[...], wr_lo_ref[...]) + br_ref[...]
    lane = lax.broadcasted_iota(jnp.int32, logits.shape, 1)
    l1 = jnp.max(logits, axis=-1, keepdims=True)
    i1 = jnp.min(jnp.where(logits == l1, lane, LANE), axis=-1, keepdims=True)
    rest = jnp.where(lane == i1, -jnp.inf, logits)
    l2 = jnp.max(rest, axis=-1, keepdims=True)
    i2 = jnp.min(jnp.where(rest == l2, lane, LANE), axis=-1, keepdims=True)
    e2 = jnp.exp(l2 - l1)
    comb_ref[...] = jnp.where(lane == i1, 1.0 / (1.0 + e2), jnp.where(lane == i2, e2 / (1.0 + e2), 0.0))


def _gla_output(o, sg, o_norm, w_out, h, mods, g, wr_hi, wr_lo, br):
    tm = ROW_TILE
    row = lambda n: pl.BlockSpec((tm, n), lambda i: (i, 0))
    hv = GLA_HEADS * GLA_DV
    return pl.pallas_call(
        _gla_out_kernel,
        out_shape=(jax.ShapeDtypeStruct((N_LAT, D), F32), jax.ShapeDtypeStruct((N_LAT, D), BF16),
                   jax.ShapeDtypeStruct((N_LAT, LANE), F32)),
        grid=(N_LAT // tm,),
        in_specs=[row(hv), row(hv), _full((1, GLA_DV)), _full((hv, D)), row(D), _mod_spec(1, tm),
                  _full((4, D)), _full((D, LANE)), _full((D, LANE)), _full((1, LANE))],
        out_specs=(row(D), row(D), row(LANE)),
        compiler_params=_params(("parallel",)),
        name="gla_output",
    )(o, sg, o_norm, w_out, h, mods, g, wr_hi, wr_lo, br)


def _slots(w, n_heads, width):
    k = w.shape[0]
    w = w.reshape(k, n_heads, width)
    return jnp.pad(w, ((0, 0), (0, 0), (0, LANE - width))).reshape(k, n_heads * LANE)


def _attn_weights(w_in, q_norm, w_uq, kv_norm, w_ukv, qk_norm, w_out):
    c = 0
    cols = {}
    for name, n in (("q_lat", MLA_Q_RANK), ("kv_lat", MLA_KV_RANK), ("k_rope", MLA_ROPE),
                    ("q_b", GQA_HEADS * GQA_HEAD_DIM), ("k_b", GQA_KV_HEADS * GQA_HEAD_DIM),
                    ("v_b", GQA_KV_HEADS * GQA_HEAD_DIM)):
        cols[name] = w_in[:, c:c + n]
        c += n
    k_rope = jnp.pad(cols["k_rope"], ((0, 0), (MLA_NOPE, LANE - MLA_NOPE - MLA_ROPE)))
    w_in_p = jnp.concatenate([cols["q_lat"], cols["kv_lat"], k_rope,
                              _slots(cols["q_b"], GQA_HEADS, GQA_HEAD_DIM),
                              _slots(cols["k_b"], GQA_KV_HEADS, GQA_HEAD_DIM),
                              _slots(cols["v_b"], GQA_KV_HEADS, GQA_HEAD_DIM)], axis=1)
    ukv = w_ukv.reshape(MLA_KV_RANK, MLA_HEADS, MLA_NOPE + MLA_V)
    w_uk = _slots(ukv[:, :, :MLA_NOPE].reshape(MLA_KV_RANK, -1), MLA_HEADS, MLA_NOPE)
    w_uv = _slots(ukv[:, :, MLA_NOPE:].reshape(MLA_KV_RANK, -1), MLA_HEADS, MLA_V)
    w_out_a = _slots(w_out[:MLA_HEADS * MLA_V].T, MLA_HEADS, MLA_V).T
    w_out_b = _slots(w_out[MLA_HEADS * MLA_V:].T, GQA_HEADS, GQA_HEAD_DIM).T
    return {
        "w_in": w_in_p.astype(BF16),
        "q_norm": q_norm.reshape(1, -1),
        "w_uq": _slots(w_uq, MLA_HEADS, MLA_NOPE + MLA_ROPE).astype(BF16),
        "kv_norm": kv_norm.reshape(1, -1),
        "w_uk": w_uk.astype(BF16),
        "w_uv": w_uv.astype(BF16),
        "qk_norm": jnp.pad(qk_norm, ((0, 0), (0, LANE - GQA_HEAD_DIM))),
        "w_out": jnp.concatenate([w_out_a, w_out_b], axis=0).astype(BF16),
    }


def _axial_tables(d):
    half, p = d // 2, d // 4
    t = jnp.arange(SEQ, dtype=jnp.int32)
    freqs = ROPE_THETA ** (-jnp.arange(0, half, 2, dtype=F32) / half)
    cos, sa, sb = [], [], []
    for pos in (t // GRID_W, t % GRID_W):
        ang = pos.astype(F32)[:, None] * freqs[None, :]
        c, s, z = jnp.cos(ang), jnp.sin(ang), jnp.zeros_like(ang)
        cos += [c, c]
        sa += [-s, z]
        sb += [z, s]
    return [jnp.concatenate(x, axis=-1) for x in (cos, sa, sb)]


def _rope_tables():
    def embed(tbl, left, fill):
        right = LANE - left - tbl.shape[1]
        tbl = jnp.pad(tbl, ((0, 0), (left, right)), constant_values=fill)
        return jnp.pad(tbl, ((0, ROW_TILE), (0, 0)), constant_values=fill)

    out = []
    for d, left in ((MLA_ROPE, MLA_NOPE), (GQA_HEAD_DIM, 0)):
        cos, sa, sb = _axial_tables(d)
        out += [embed(cos, left, 1.0), embed(sa, left, 0.0), embed(sb, left, 0.0)]
    return out


def _gla_weights(w_in, w_gate2, b_gate):
    hk, hv = GLA_HEADS * GLA_DK, GLA_HEADS * GLA_DV
    r = jnp.pad(w_in[:, 2 * hk + 2 * hv:], ((0, 0), (0, LANE - 2 * GLA_GATE_RANK)))
    w_in_p = jnp.concatenate([w_in[:, :2 * hk + 2 * hv], r], axis=1)
    w_gate = jnp.zeros((LANE, 2 * hk), F32)
    w_gate = w_gate.at[:GLA_GATE_RANK, :hk].set(w_gate2[0])
    w_gate = w_gate.at[GLA_GATE_RANK:2 * GLA_GATE_RANK, hk:].set(w_gate2[1])
    return {
        "w_in": w_in_p.astype(BF16),
        "w_vt": w_in[:, 2 * hk:2 * hk + hv].T.astype(BF16),
        "w_gate": w_gate.astype(BF16),
        "b_gate": b_gate.reshape(1, 2 * hk),
    }


def kernel(x, c, ctx, c_ctx, mod_w, mod_b, norm_g, attn_w_in, attn_q_norm, attn_w_uq, attn_kv_norm, attn_w_ukv,
           attn_qk_norm, attn_w_out, gla_w_in, gla_w_gate2, gla_b_gate, gla_o_norm, gla_w_out, ffn_w_gate,
           ffn_w_up, ffn_w_down, moe_w_router, moe_b_router, moe_w_gate, moe_w_up, moe_w_down):
    assert x.shape == (BATCH, SEQ, D) and ctx.shape == (BATCH, CTX, D)
    h = jnp.concatenate([x.reshape(N_LAT, D), ctx.reshape(N_CTX, D)], axis=0)
    cc = jnp.concatenate([c, c_ctx[None, :], jnp.zeros((MOD_ROWS - BATCH - 1, D), F32)], axis=0)
    mods = _mod_vectors(cc, mod_w, mod_b).reshape(mod_w.shape[0], MOD_ROWS, 6, D)

    aw = _attn_weights(attn_w_in[0], attn_q_norm[0], attn_w_uq[0], attn_kv_norm[0], attn_w_ukv[0],
                       attn_qk_norm[0], attn_w_out[0])
    q, k, v = _attn_project(h, mods, norm_g[0], aw, _rope_tables())
    o = _attention(q, k, v)
    h, u = _attn_output(o, aw["w_out"], h, mods, norm_g[0])
    h = _mlp(u, ffn_w_gate[0:1].astype(BF16), ffn_w_up[0:1].astype(BF16), ffn_w_down[0:1].astype(BF16),
             None, h, mods, norm_g[0], 0, N_ROWS, FFN_TILE)

    gw = _gla_weights(gla_w_in[0], gla_w_gate2[0], gla_b_gate[0])
    gq, gk, gv, gvt, sg, laf, lab = _gla_project(h, mods, norm_g[1], gw)
    go = _gla_scan(gq, gk, gv, gvt, laf, lab)
    wr = jnp.pad(moe_w_router[0], ((0, 0), (0, LANE - N_EXPERTS)))
    wr_hi = wr.astype(BF16)
    wr_lo = (wr - wr_hi.astype(F32)).astype(BF16)
    br = jnp.pad(moe_b_router[0], (0, LANE - N_EXPERTS), constant_values=-jnp.inf).reshape(1, LANE)
    h, u, combine = _gla_output(go, sg, gla_o_norm[0].reshape(1, GLA_DV), gla_w_out[0].astype(BF16), h, mods,
                                norm_g[1], wr_hi, wr_lo, br)
    h = _mlp(u, moe_w_gate[0].astype(BF16), moe_w_up[0].astype(BF16), moe_w_down[0].astype(BF16),
             combine, h, mods, norm_g[1], 1, N_LAT, EXPERT_TILE)
    return h.reshape(BATCH, SEQ, D)
```

```python
import functools

import jax
import jax.numpy as jnp
from jax import lax
from jax.experimental import pallas as pl
from jax.experimental.pallas import tpu as pltpu

F32 = jnp.float32
BF16 = jnp.bfloat16

D = 1024
BATCH = 4
SEQ = 4096
CTX = 256
GRID_W = 64
ROPE_THETA = 10000.0
EPS = 1e-6

N_LAT = BATCH * SEQ
N_CTX = BATCH * CTX
N_ROWS = N_LAT + N_CTX
MOD_ROWS = 8
CTX_MOD_ROW = BATCH

LANE = 128
ROW_TILE = 256

MLA_HEADS = 8
MLA_Q_RANK = 384
MLA_KV_RANK = 256
MLA_NOPE = 64
MLA_ROPE = 32
MLA_V = 64
GQA_HEADS = 8
GQA_KV_HEADS = 2
GQA_GROUP = GQA_HEADS // GQA_KV_HEADS
GQA_HEAD_DIM = 64
N_Q_HEADS = MLA_HEADS + GQA_HEADS

GLA_HEADS = 4
GLA_DK = 128
GLA_DV = 256
GLA_GATE_RANK = 16
GLA_GATE_NORM = 16.0
GLA_BLOCK = 128
GLA_SUB = 32

FFN_DIM = 2816
FFN_TILE = 1408
N_EXPERTS = 8
EXPERT_DIM = 3584
EXPERT_TILE = 896
MLP_ROW_TILE = 512

VMEM_LIMIT = 56 * 1024 * 1024


def _params(sem):
    return pltpu.CompilerParams(dimension_semantics=sem, vmem_limit_bytes=VMEM_LIMIT)


def _rms(x, g):
    return x * lax.rsqrt(jnp.mean(x * x, axis=-1, keepdims=True) + EPS) * g


def _silu(x):
    return x / (1.0 + jnp.exp(-x))


def _split_bf16(x):
    hi = x.astype(BF16)
    lo = (x - hi.astype(F32)).astype(BF16)
    return hi, lo


def _dot(a, b):
    return jnp.dot(a, b, preferred_element_type=F32)


def _dot_nt(a, b):
    return lax.dot_general(a, b, (((1,), (1,)), ((), ())), preferred_element_type=F32)


def _dot3(a, b_hi, b_lo):
    a_hi, a_lo = _split_bf16(a)
    return _dot(a_hi, b_hi) + (_dot(a_hi, b_lo) + _dot(a_lo, b_hi))


def _modulated(h, g_row, shift, scale):
    return _rms(h, g_row) * (1.0 + scale) + shift


def _rope(x, cos, sa, sb, p):
    return x * cos + pltpu.roll(x, LANE - p, 1) * sa + pltpu.roll(x, p, 1) * sb


def _mod_row(i, tile):
    r0 = i * tile
    return jnp.where(r0 < N_LAT, r0 // SEQ, CTX_MOD_ROW)


def _mod_spec(layer, tile):
    return pl.BlockSpec((None, None, 6, D), lambda i, *_: (layer, _mod_row(i, tile), 0, 0))


def _rope_block(i):
    r0 = i * ROW_TILE
    return jnp.where(r0 < N_LAT, (r0 % SEQ) // ROW_TILE, SEQ // ROW_TILE)


def _full(shape):
    return pl.BlockSpec(shape, lambda *_: (0,) * len(shape))


def _mod_kernel(c_ref, w_ref, b_ref, o_ref):
    w_hi, w_lo = _split_bf16(w_ref[...])
    o_ref[...] = _dot3(_silu(c_ref[...]), w_hi, w_lo) + b_ref[...]


def _mod_vectors(cc, mod_w, mod_b):
    depth, _, n = mod_w.shape
    tn = 1536
    return pl.pallas_call(
        _mod_kernel,
        out_shape=jax.ShapeDtypeStruct((depth, MOD_ROWS, n), F32),
        grid=(depth, n // tn),
        in_specs=[
            pl.BlockSpec((MOD_ROWS, D), lambda l, j: (0, 0)),
            pl.BlockSpec((None, D, tn), lambda l, j: (l, 0, j)),
            pl.BlockSpec((None, 1, tn), lambda l, j: (l, 0, j)),
        ],
        out_specs=pl.BlockSpec((None, MOD_ROWS, tn), lambda l, j: (l, 0, j)),
        compiler_params=_params(("parallel", "parallel")),
        name="mod_vectors",
    )(cc, mod_w, mod_b.reshape(depth, 1, n))


Q_LAT0, KV_LAT0, K_ROPE0 = 0, 384, 640
Q_B0 = 768
K_B0 = Q_B0 + GQA_HEADS * LANE
ATTN_IN_P = K_B0 + GQA_KV_HEADS * LANE
ATTN_V = 64
LOG2E = 1.4426950408889634


def _attn_proj_kernel(h_ref, m_ref, g_ref, w_in_ref, qn_ref, w_uq_ref, kvn_ref, w_uk_ref, w_uvt_ref,
                      w_vbt_ref, qkn_ref, ca_ref, saa_ref, sba_ref, cb_ref, sab_ref, sbb_ref,
                      q_ref, k_ref, vt_ref):
    u = _modulated(h_ref[...], g_ref[0:1, :], m_ref[0:1, :], m_ref[1:2, :]).astype(BF16)
    z = _dot(u, w_in_ref[...])
    q_lat = _rms(z[:, Q_LAT0:Q_LAT0 + MLA_Q_RANK], qn_ref[...]).astype(BF16)
    kv_lat = _rms(z[:, KV_LAT0:KV_LAT0 + MLA_KV_RANK], kvn_ref[...]).astype(BF16)
    q_a = _dot(q_lat, w_uq_ref[...])
    k_a = _dot(kv_lat, w_uk_ref[...])
    ca, saa, sba = ca_ref[...], saa_ref[...], sba_ref[...]
    cb, sab, sbb = cb_ref[...], sab_ref[...], sbb_ref[...]
    pa, pb = MLA_ROPE // 4, GQA_HEAD_DIM // 4
    scale_a = (MLA_NOPE + MLA_ROPE) ** -0.5 * LOG2E
    scale_b = GQA_HEAD_DIM ** -0.5 * LOG2E
    k_rope = _rope(z[:, K_ROPE0:K_ROPE0 + LANE], ca, saa, sba, pa)
    for hd in range(MLA_HEADS):
        sl = slice(hd * LANE, (hd + 1) * LANE)
        q_ref[:, sl] = (_rope(q_a[:, sl], ca, saa, sba, pa) * scale_a).astype(BF16)
        k_ref[:, sl] = (k_a[:, sl] + k_rope).astype(BF16)

    def head_norm(x, gain):
        ms = jnp.sum(x * x, axis=-1, keepdims=True) * (1.0 / GQA_HEAD_DIM)
        return x * lax.rsqrt(ms + EPS) * gain

    for hd in range(GQA_HEADS):
        x = head_norm(z[:, Q_B0 + hd * LANE:Q_B0 + (hd + 1) * LANE], qkn_ref[0:1, :])
        sl = slice((MLA_HEADS + hd) * LANE, (MLA_HEADS + hd + 1) * LANE)
        q_ref[:, sl] = (_rope(x, cb, sab, sbb, pb) * scale_b).astype(BF16)
    for kv in range(GQA_KV_HEADS):
        x = head_norm(z[:, K_B0 + kv * LANE:K_B0 + (kv + 1) * LANE], qkn_ref[1:2, :])
        x = _rope(x, cb, sab, sbb, pb).astype(BF16)
        for hd in range(kv * GQA_GROUP, (kv + 1) * GQA_GROUP):
            k_ref[:, (MLA_HEADS + hd) * LANE:(MLA_HEADS + hd + 1) * LANE] = x

    vt_ref[0:MLA_HEADS * ATTN_V, :] = _dot_nt(w_uvt_ref[...], kv_lat).astype(BF16)
    vbt = _dot_nt(w_vbt_ref[...], u).astype(BF16)
    for hd in range(GQA_HEADS):
        kv = hd // GQA_GROUP
        r0 = (MLA_HEADS + hd) * ATTN_V
        vt_ref[r0:r0 + ATTN_V, :] = vbt[kv * ATTN_V:(kv + 1) * ATTN_V, :]


def _attn_project(h, mods, g, w, tables):
    tm = ROW_TILE
    row = lambda n: pl.BlockSpec((tm, n), lambda i: (i, 0))
    tab = pl.BlockSpec((tm, LANE), lambda i: (_rope_block(i), 0))
    return pl.pallas_call(
        _attn_proj_kernel,
        out_shape=(jax.ShapeDtypeStruct((N_ROWS, N_Q_HEADS * LANE), BF16),
                   jax.ShapeDtypeStruct((N_ROWS, N_Q_HEADS * LANE), BF16),
                   jax.ShapeDtypeStruct((N_Q_HEADS * ATTN_V, N_ROWS), BF16)),
        grid=(N_ROWS // tm,),
        in_specs=[row(D), _mod_spec(0, tm), _full((4, D)), _full((D, ATTN_IN_P)),
                  _full((1, MLA_Q_RANK)), _full((MLA_Q_RANK, MLA_HEADS * LANE)),
                  _full((1, MLA_KV_RANK)), _full((MLA_KV_RANK, MLA_HEADS * LANE)),
                  _full((MLA_HEADS * ATTN_V, MLA_KV_RANK)), _full((GQA_KV_HEADS * ATTN_V, D)),
                  _full((2, LANE)), tab, tab, tab, tab, tab, tab],
        out_specs=(row(N_Q_HEADS * LANE), row(N_Q_HEADS * LANE),
                   pl.BlockSpec((N_Q_HEADS * ATTN_V, tm), lambda i: (0, i))),
        compiler_params=_params(("parallel",)),
        name="attn_project",
    )(h, mods, g, w["w_in"], w["q_norm"], w["w_uq"], w["kv_norm"], w["w_uk"], w["w_uvt"], w["w_vbt"],
      w["qk_norm"], *tables)


ATT_TQ = 256
ATT_TK = 512
ATT_LAT_TILES = SEQ // ATT_TQ
ATT_GROUP = 4


def _attn_kernel(q_ref, kc_ref, kl_ref, vtc_ref, vtl_ref, o_ref, sa_ref, sb_ref):
    qi = pl.program_id(2)
    heads = range(ATT_GROUP)
    n_lat = SEQ // ATT_TK

    def scores(s_ref, hd, k_ref, keys, n_keys):
        s = _dot_nt(k_ref[keys, hd * LANE:(hd + 1) * LANE], q_ref[:, hd * LANE:(hd + 1) * LANE])
        s_ref[hd, 0:n_keys, :] = s
        return jnp.max(s, axis=0, keepdims=True)

    def consume(s_ref, hd, vt_ref, keys, n_keys, cmax, carry):
        m, l, acc = carry
        m_new = jnp.maximum(m, cmax)
        alpha = jnp.exp2(m - m_new)
        p = jnp.exp2(s_ref[hd, 0:n_keys, :] - m_new)
        l = alpha * l + jnp.sum(p, axis=0, keepdims=True)
        acc = alpha * acc + _dot(vt_ref[hd * ATTN_V:(hd + 1) * ATTN_V, keys], p.astype(BF16))
        return m_new, l, acc

    def finish(carries):
        o_t = jnp.concatenate([acc / l for _, l, acc in carries], axis=0)
        o_ref[...] = o_t.T.astype(BF16)

    init = (jnp.full((1, ATT_TQ), -jnp.inf, F32), jnp.zeros((1, ATT_TQ), F32), jnp.zeros((ATTN_V, ATT_TQ), F32))
    every = slice(None)

    def lat_keys(j):
        return pl.ds(pl.multiple_of(j * ATT_TK, ATT_TK), ATT_TK)

    def step(cur_ref, nxt_ref, j, nxt_k_ref, nxt_keys, nxt_n, cmax, carries):
        nxt, out = [], []
        for hd in heads:
            nxt.append(scores(nxt_ref, hd, nxt_k_ref, nxt_keys, nxt_n))
            out.append(consume(cur_ref, hd, vtl_ref, lat_keys(j), ATT_TK, cmax[hd], carries[hd]))
        return tuple(nxt), tuple(out)

    @pl.when(qi < ATT_LAT_TILES)
    def _():
        cmax0 = tuple(scores(sa_ref, hd, kl_ref, lat_keys(0), ATT_TK) for hd in heads)

        def body(i, state):
            state = step(sa_ref, sb_ref, 2 * i, kl_ref, lat_keys(2 * i + 1), ATT_TK, *state)
            return step(sb_ref, sa_ref, 2 * i + 1, kl_ref, lat_keys(2 * i + 2), ATT_TK, *state)

        state = lax.fori_loop(0, n_lat // 2 - 1, body, (cmax0, (init,) * ATT_GROUP))
        state = step(sa_ref, sb_ref, n_lat - 2, kl_ref, lat_keys(n_lat - 1), ATT_TK, *state)
        cmax_c, carries = step(sb_ref, sa_ref, n_lat - 1, kc_ref, every, CTX, *state)
        finish([consume(sa_ref, hd, vtc_ref, every, CTX, cmax_c[hd], carries[hd]) for hd in heads])

    @pl.when(qi == ATT_LAT_TILES)
    def _():
        cmax_c = [scores(sa_ref, hd, kc_ref, every, CTX) for hd in heads]
        finish([consume(sa_ref, hd, vtc_ref, every, CTX, cmax_c[hd], init) for hd in heads])


def _attention(q, k, vt):
    gw = ATT_GROUP * LANE
    gv = ATT_GROUP * ATTN_V

    def q_row(b, qi):
        return jnp.where(qi < ATT_LAT_TILES, b * ATT_LAT_TILES + qi, N_LAT // ATT_TQ + b)

    return pl.pallas_call(
        _attn_kernel,
        out_shape=jax.ShapeDtypeStruct((N_ROWS, N_Q_HEADS * ATTN_V), BF16),
        grid=(BATCH, N_Q_HEADS // ATT_GROUP, ATT_LAT_TILES + 1),
        in_specs=[pl.BlockSpec((ATT_TQ, gw), lambda b, g, qi: (q_row(b, qi), g)),
                  pl.BlockSpec((CTX, gw), lambda b, g, qi: (N_LAT // CTX + b, g)),
                  pl.BlockSpec((SEQ, gw), lambda b, g, qi: (b, g)),
                  pl.BlockSpec((gv, CTX), lambda b, g, qi: (g, N_LAT // CTX + b)),
                  pl.BlockSpec((gv, SEQ), lambda b, g, qi: (g, b))],
        out_specs=pl.BlockSpec((ATT_TQ, gv), lambda b, g, qi: (q_row(b, qi), g)),
        scratch_shapes=[pltpu.VMEM((ATT_GROUP, ATT_TK, ATT_TQ), F32)] * 2,
        compiler_params=_params(("parallel", "parallel", "arbitrary")),
        name="attention",
    )(q, k, k, vt, vt)


def _mixer_epilogue(y, h, m_ref, g_ref):
    h_new = h + m_ref[2:3, :] * _rms(y, g_ref[1:2, :])
    u = _modulated(h_new, g_ref[2:3, :], m_ref[3:4, :], m_ref[4:5, :])
    return h_new, u


def _attn_out_kernel(o_ref, w_ref, h_ref, m_ref, g_ref, h_out_ref, u_ref):
    y = _dot(o_ref[...], w_ref[...])
    h_new, u = _mixer_epilogue(y, h_ref[...], m_ref, g_ref)
    h_out_ref[...] = h_new
    u_ref[...] = u.astype(BF16)


def _attn_output(o, w_out, h, mods, g):
    tm = ROW_TILE
    row = lambda n: pl.BlockSpec((tm, n), lambda i: (i, 0))
    return pl.pallas_call(
        _attn_out_kernel,
        out_shape=(jax.ShapeDtypeStruct((N_ROWS, D), F32), jax.ShapeDtypeStruct((N_ROWS, D), BF16)),
        grid=(N_ROWS // tm,),
        in_specs=[row(N_Q_HEADS * ATTN_V), _full((N_Q_HEADS * ATTN_V, D)), row(D), _mod_spec(0, tm),
                  _full((4, D))],
        out_specs=(row(D), row(D)),
        compiler_params=_params(("parallel",)),
        name="attn_output",
    )(o, w_out, h, mods, g)


def _mlp_kernel(x_ref, wg_ref, wu_ref, wd_ref, c_ref, h_ref, m_ref, g_ref, o_ref, acc_ref, *, gated):
    e, f = pl.program_id(1), pl.program_id(2)

    @pl.when((e == 0) & (f == 0))
    def _():
        acc_ref[...] = jnp.zeros_like(acc_ref)

    x = x_ref[...]
    a = _silu(_dot(x, wg_ref[...])) * _dot(x, wu_ref[...])
    y = _dot(a.astype(BF16), wd_ref[...])
    if gated:
        lane = lax.broadcasted_iota(jnp.int32, c_ref.shape, 1)
        y = jnp.sum(jnp.where(lane == e, c_ref[...], 0.0), axis=-1, keepdims=True) * y
    acc_ref[...] += y

    @pl.when((e == pl.num_programs(1) - 1) & (f == pl.num_programs(2) - 1))
    def _():
        o_ref[...] = h_ref[...] + m_ref[5:6, :] * _rms(acc_ref[...], g_ref[3:4, :])


def _mlp(x, wg, wu, wd, combine, h, mods, g, layer, n_rows, f_tile):
    n_e, _, f_dim = wg.shape
    tm = MLP_ROW_TILE
    row = lambda n: pl.BlockSpec((tm, n), lambda i, e, f: (i, 0))
    gated = combine is not None
    if not gated:
        combine = jnp.zeros((tm, LANE), F32)
        c_spec = pl.BlockSpec((tm, LANE), lambda i, e, f: (0, 0))
    else:
        c_spec = row(LANE)
    return pl.pallas_call(
        functools.partial(_mlp_kernel, gated=gated),
        out_shape=jax.ShapeDtypeStruct((n_rows, D), F32),
        grid=(n_rows // tm, n_e, f_dim // f_tile),
        in_specs=[row(D),
                  pl.BlockSpec((None, D, f_tile), lambda i, e, f: (e, 0, f)),
                  pl.BlockSpec((None, D, f_tile), lambda i, e, f: (e, 0, f)),
                  pl.BlockSpec((None, f_tile, D), lambda i, e, f: (e, f, 0)),
                  c_spec, row(D), _mod_spec(layer, tm), _full((4, D))],
        out_specs=row(D),
        scratch_shapes=[pltpu.VMEM((tm, D), F32)],
        compiler_params=_params(("parallel", "arbitrary", "arbitrary")),
        name="moe_mlp" if gated else "ffn_mlp",
    )(x, wg, wu, wd, combine, h, mods, g)


GLA_Q0, GLA_K0, GLA_V0, GLA_G0, GLA_R0 = 0, 512, 1024, 2048, 3072
GLA_IN_P = GLA_R0 + LANE


def _gla_proj_kernel(h_ref, m_ref, g_ref, w_in_ref, w_vt_ref, w_gate_ref, b_gate_ref,
                     q_ref, k_ref, v_ref, vt_ref, sg_ref, laf_ref, lab_ref):
    u = _modulated(h_ref[...], g_ref[0:1, :], m_ref[0:1, :], m_ref[1:2, :]).astype(BF16)
    z = _dot(u, w_in_ref[...])
    q_ref[...] = (z[:, GLA_Q0:GLA_K0] * (GLA_DK ** -0.5)).astype(BF16)
    k_ref[...] = z[:, GLA_K0:GLA_V0].astype(BF16)
    v_ref[...] = z[:, GLA_V0:GLA_G0].astype(BF16)
    vt_ref[...] = _dot_nt(w_vt_ref[...], u).astype(BF16)
    sg_ref[...] = _silu(z[:, GLA_G0:GLA_R0]).astype(BF16)
    zg = _dot(z[:, GLA_R0:].astype(BF16), w_gate_ref[...]) + b_gate_ref[...]
    la = (jnp.minimum(zg, 0.0) - jnp.log(1.0 + jnp.exp(-jnp.abs(zg)))) * (1.0 / GLA_GATE_NORM)
    laf_ref[...] = la[:, :GLA_HEADS * GLA_DK]
    lab_ref[...] = la[:, GLA_HEADS * GLA_DK:]


def _gla_project(h, mods, g, w):
    tm = ROW_TILE
    row = lambda n, dt=None: pl.BlockSpec((tm, n), lambda i: (i, 0))
    hk, hv = GLA_HEADS * GLA_DK, GLA_HEADS * GLA_DV
    sds = jax.ShapeDtypeStruct
    return pl.pallas_call(
        _gla_proj_kernel,
        out_shape=(sds((N_ROWS, hk), BF16), sds((N_ROWS, hk), BF16), sds((N_ROWS, hv), BF16),
                   sds((hv, N_ROWS), BF16), sds((N_ROWS, hv), BF16),
                   sds((N_ROWS, hk), F32), sds((N_ROWS, hk), F32)),
        grid=(N_ROWS // tm,),
        in_specs=[row(D), _mod_spec(1, tm), _full((4, D)), _full((D, GLA_IN_P)), _full((hv, D)),
                  _full((LANE, 2 * hk)), _full((1, 2 * hk))],
        out_specs=(row(hk), row(hk), row(hv), pl.BlockSpec((hv, tm), lambda i: (0, i)), row(hv),
                   row(hk), row(hk)),
        compiler_params=_params(("parallel",)),
        name="gla_project",
    )(h, mods, g, w["w_in"], w["w_vt"], w["w_gate"], w["b_gate"])


GLA_LAT_BLOCKS = SEQ // GLA_BLOCK
GLA_CTX_BLOCKS = CTX // GLA_BLOCK
GLA_STRIPS = GLA_BLOCK // GLA_SUB


def _gla_block(q, k, v, vt, la, state, reverse):
    n = GLA_BLOCK
    r = lax.broadcasted_iota(jnp.int32, (n, n), 0)
    c = lax.broadcasted_iota(jnp.int32, (n, n), 1)
    keep = (c >= r) if reverse else (c <= r)
    tri = jnp.where(keep, 1.0, 0.0).astype(BF16)
    la_hi, la_lo = _split_bf16(la)
    cum = _dot(tri, la_hi) + _dot(tri, la_lo)
    total = cum[0:1, :] if reverse else cum[n - 1:n, :]
    k_state = (k * jnp.exp(total - cum)).astype(BF16)
    new_state = state * jnp.exp(total) + _dot(vt, k_state)
    if q is None:
        return None, new_state
    o = _dot_nt((q * jnp.exp(cum)).astype(BF16), state.astype(BF16))
    row = lax.broadcasted_iota(jnp.int32, (n, GLA_DK), 0)
    strips = []
    for i in range(GLA_STRIPS):
        lo, hi = i * GLA_SUB, (i + 1) * GLA_SUB
        if reverse:
            ref = cum[hi:hi + 1, :] if i < GLA_STRIPS - 1 else jnp.zeros((1, GLA_DK), F32)
            live = row >= lo
        else:
            ref = cum[lo - 1:lo, :] if i > 0 else jnp.zeros((1, GLA_DK), F32)
            live = row < hi
        q_loc = (q[lo:hi, :] * jnp.exp(cum[lo:hi, :] - ref)).astype(BF16)
        k_loc = jnp.where(live, k * jnp.exp(ref - cum), 0.0).astype(BF16)
        strips.append(_dot_nt(q_loc, k_loc))
    scores = jnp.where(keep, jnp.concatenate(strips, axis=0), 0.0).astype(BF16)
    return o + _dot(scores, v), new_state


def _gla_kernel(q_ref, kl_ref, kc_ref, vl_ref, vc_ref, vtl_ref, vtc_ref,
                lafl_ref, lafc_ref, labl_ref, labc_ref, o_ref, sf_ref, sb_ref):
    n = GLA_BLOCK
    sf_ref[...] = jnp.zeros_like(sf_ref)
    sb_ref[...] = jnp.zeros_like(sb_ref)

    for j in range(GLA_CTX_BLOCKS):
        fs = slice(j * n, (j + 1) * n)
        bs = slice((GLA_CTX_BLOCKS - 1 - j) * n, (GLA_CTX_BLOCKS - j) * n)
        _, sf = _gla_block(None, kc_ref[fs, :].astype(F32), None, vtc_ref[:, fs], lafc_ref[fs, :],
                           sf_ref[...], False)
        sf_ref[...] = sf
        _, sb = _gla_block(None, kc_ref[bs, :].astype(F32), None, vtc_ref[:, bs], labc_ref[bs, :],
                           sb_ref[...], True)
        sb_ref[...] = sb

    def step(j, accumulate):
        for reverse, la_ref, s_ref in ((False, lafl_ref, sf_ref), (True, labl_ref, sb_ref)):
            blk = (GLA_LAT_BLOCKS - 1 - j) if reverse else j
            off = pl.multiple_of(blk * n, n)
            rows = pl.ds(off, n)
            o, s_new = _gla_block(q_ref[rows, :].astype(F32), kl_ref[rows, :].astype(F32), vl_ref[rows, :],
                                  vtl_ref[:, rows], la_ref[rows, :], s_ref[...], reverse)
            s_ref[...] = s_new
            if accumulate:
                o_ref[rows, :] += o
            else:
                o_ref[rows, :] = o

    half = GLA_LAT_BLOCKS // 2

    def first(j, _):
        step(j, False)
        return 0

    def second(j, _):
        step(j, True)
        return 0

    lax.fori_loop(0, half, first, 0)
    lax.fori_loop(half, GLA_LAT_BLOCKS, second, 0)


def _gla_scan(q, k, v, vt, laf, lab):
    lat = lambda n: pl.BlockSpec((SEQ, n), lambda b, hd: (b, hd))
    ctx = lambda n: pl.BlockSpec((CTX, n), lambda b, hd: (N_LAT // CTX + b, hd))
    return pl.pallas_call(
        _gla_kernel,
        out_shape=jax.ShapeDtypeStruct((N_LAT, GLA_HEADS * GLA_DV), F32),
        grid=(BATCH, GLA_HEADS),
        in_specs=[lat(GLA_DK), lat(GLA_DK), ctx(GLA_DK), lat(GLA_DV), ctx(GLA_DV),
                  pl.BlockSpec((GLA_DV, SEQ), lambda b, hd: (hd, b)),
                  pl.BlockSpec((GLA_DV, CTX), lambda b, hd: (hd, N_LAT // CTX + b)),
                  lat(GLA_DK), ctx(GLA_DK), lat(GLA_DK), ctx(GLA_DK)],
        out_specs=lat(GLA_DV),
        scratch_shapes=[pltpu.VMEM((GLA_DV, GLA_DK), F32), pltpu.VMEM((GLA_DV, GLA_DK), F32)],
        compiler_params=_params(("parallel", "parallel")),
        name="gla_scan",
    )(q, k, k, v, v, vt, vt, laf, laf, lab, lab)


def _gla_out_kernel(o_ref, sg_ref, on_ref, w_ref, h_ref, m_ref, g_ref, wr_hi_ref, wr_lo_ref, br_ref,
                    h_out_ref, u_ref, comb_ref):
    parts = []
    for hd in range(GLA_HEADS):
        sl = slice(hd * GLA_DV, (hd + 1) * GLA_DV)
        parts.append(_rms(o_ref[:, sl], on_ref[...]) * sg_ref[:, sl].astype(F32))
    y = _dot(jnp.concatenate(parts, axis=-1).astype(BF16), w_ref[...])
    h_new, u = _mixer_epilogue(y, h_ref[...], m_ref, g_ref)
    h_out_ref[...] = h_new
    u_ref[...] = u.astype(BF16)
    logits = _dot3(u, wr_hi_ref[...], wr_lo_ref[...]) + br_ref[...]
    lane = lax.broadcasted_iota(jnp.int32, logits.shape, 1)
    l1 = jnp.max(logits, axis=-1, keepdims=True)
    i1 = jnp.min(jnp.where(logits == l1, lane, LANE), axis=-1, keepdims=True)
    rest = jnp.where(lane == i1, -jnp.inf, logits)
    l2 = jnp.max(rest, axis=-1, keepdims=True)
    i2 = jnp.min(jnp.where(rest == l2, lane, LANE), axis=-1, keepdims=True)
    e2 = jnp.exp(l2 - l1)
    comb_ref[...] = jnp.where(lane == i1, 1.0 / (1.0 + e2), jnp.where(lane == i2, e2 / (1.0 + e2), 0.0))


def _gla_output(o, sg, o_norm, w_out, h, mods, g, wr_hi, wr_lo, br):
    tm = ROW_TILE
    row = lambda n: pl.BlockSpec((tm, n), lambda i: (i, 0))
    hv = GLA_HEADS * GLA_DV
    return pl.pallas_call(
        _gla_out_kernel,
        out_shape=(jax.ShapeDtypeStruct((N_LAT, D), F32), jax.ShapeDtypeStruct((N_LAT, D), BF16),
                   jax.ShapeDtypeStruct((N_LAT, LANE), F32)),
        grid=(N_LAT // tm,),
        in_specs=[row(hv), row(hv), _full((1, GLA_DV)), _full((hv, D)), row(D), _mod_spec(1, tm),
                  _full((4, D)), _full((D, LANE)), _full((D, LANE)), _full((1, LANE))],
        out_specs=(row(D), row(D), row(LANE)),
        compiler_params=_params(("parallel",)),
        name="gla_output",
    )(o, sg, o_norm, w_out, h, mods, g, wr_hi, wr_lo, br)


def _slots(w, n_heads, width):
    k = w.shape[0]
    w = w.reshape(k, n_heads, width)
    return jnp.pad(w, ((0, 0), (0, 0), (0, LANE - width))).reshape(k, n_heads * LANE)


def _attn_weights(w_in, q_norm, w_uq, kv_norm, w_ukv, qk_norm, w_out):
    c = 0
    cols = {}
    for name, n in (("q_lat", MLA_Q_RANK), ("kv_lat", MLA_KV_RANK), ("k_rope", MLA_ROPE),
                    ("q_b", GQA_HEADS * GQA_HEAD_DIM), ("k_b", GQA_KV_HEADS * GQA_HEAD_DIM),
                    ("v_b", GQA_KV_HEADS * GQA_HEAD_DIM)):
        cols[name] = w_in[:, c:c + n]
        c += n
    k_rope = jnp.pad(cols["k_rope"], ((0, 0), (MLA_NOPE, LANE - MLA_NOPE - MLA_ROPE)))
    w_in_p = jnp.concatenate([cols["q_lat"], cols["kv_lat"], k_rope,
                              _slots(cols["q_b"], GQA_HEADS, GQA_HEAD_DIM),
                              _slots(cols["k_b"], GQA_KV_HEADS, GQA_HEAD_DIM)], axis=1)
    ukv = w_ukv.reshape(MLA_KV_RANK, MLA_HEADS, MLA_NOPE + MLA_V)
    w_uk = _slots(ukv[:, :, :MLA_NOPE].reshape(MLA_KV_RANK, -1), MLA_HEADS, MLA_NOPE)
    w_uv = ukv[:, :, MLA_NOPE:].reshape(MLA_KV_RANK, MLA_HEADS * MLA_V)
    return {
        "w_in": w_in_p.astype(BF16),
        "q_norm": q_norm.reshape(1, -1),
        "w_uq": _slots(w_uq, MLA_HEADS, MLA_NOPE + MLA_ROPE).astype(BF16),
        "kv_norm": kv_norm.reshape(1, -1),
        "w_uk": w_uk.astype(BF16),
        "w_uvt": w_uv.T.astype(BF16),
        "w_vbt": cols["v_b"].T.astype(BF16),
        "qk_norm": jnp.pad(qk_norm, ((0, 0), (0, LANE - GQA_HEAD_DIM))),
        "w_out": w_out.astype(BF16),
    }


def _axial_tables(d):
    half, p = d // 2, d // 4
    t = jnp.arange(SEQ, dtype=jnp.int32)
    freqs = ROPE_THETA ** (-jnp.arange(0, half, 2, dtype=F32) / half)
    cos, sa, sb = [], [], []
    for pos in (t // GRID_W, t % GRID_W):
        ang = pos.astype(F32)[:, None] * freqs[None, :]
        c, s, z = jnp.cos(ang), jnp.sin(ang), jnp.zeros_like(ang)
        cos += [c, c]
        sa += [-s, z]
        sb += [z, s]
    return [jnp.concatenate(x, axis=-1) for x in (cos, sa, sb)]


def _rope_tables():
    def embed(tbl, left, fill):
        right = LANE - left - tbl.shape[1]
        tbl = jnp.pad(tbl, ((0, 0), (left, right)), constant_values=fill)
        return jnp.pad(tbl, ((0, ROW_TILE), (0, 0)), constant_values=fill)

    out = []
    for d, left in ((MLA_ROPE, MLA_NOPE), (GQA_HEAD_DIM, 0)):
        cos, sa, sb = _axial_tables(d)
        out += [embed(cos, left, 1.0), embed(sa, left, 0.0), embed(sb, left, 0.0)]
    return out


def _gla_weights(w_in, w_gate2, b_gate):
    hk, hv = GLA_HEADS * GLA_DK, GLA_HEADS * GLA_DV
    r = jnp.pad(w_in[:, 2 * hk + 2 * hv:], ((0, 0), (0, LANE - 2 * GLA_GATE_RANK)))
    w_in_p = jnp.concatenate([w_in[:, :2 * hk + 2 * hv], r], axis=1)
    w_gate = jnp.zeros((LANE, 2 * hk), F32)
    w_gate = w_gate.at[:GLA_GATE_RANK, :hk].set(w_gate2[0])
    w_gate = w_gate.at[GLA_GATE_RANK:2 * GLA_GATE_RANK, hk:].set(w_gate2[1])
    return {
        "w_in": w_in_p.astype(BF16),
        "w_vt": w_in[:, 2 * hk:2 * hk + hv].T.astype(BF16),
        "w_gate": w_gate.astype(BF16),
        "b_gate": b_gate.reshape(1, 2 * hk),
    }


def kernel(x, c, ctx, c_ctx, mod_w, mod_b, norm_g, attn_w_in, attn_q_norm, attn_w_uq, attn_kv_norm, attn_w_ukv,
           attn_qk_norm, attn_w_out, gla_w_in, gla_w_gate2, gla_b_gate, gla_o_norm, gla_w_out, ffn_w_gate,
           ffn_w_up, ffn_w_down, moe_w_router, moe_b_router, moe_w_gate, moe_w_up, moe_w_down):
    assert x.shape == (BATCH, SEQ, D) and ctx.shape == (BATCH, CTX, D)
    h = jnp.concatenate([x.reshape(N_LAT, D), ctx.reshape(N_CTX, D)], axis=0)
    cc = jnp.concatenate([c, c_ctx[None, :], jnp.zeros((MOD_ROWS - BATCH - 1, D), F32)], axis=0)
    mods = _mod_vectors(cc, mod_w, mod_b).reshape(mod_w.shape[0], MOD_ROWS, 6, D)

    aw = _attn_weights(attn_w_in[0], attn_q_norm[0], attn_w_uq[0], attn_kv_norm[0], attn_w_ukv[0],
                       attn_qk_norm[0], attn_w_out[0])
    q, k, vt = _attn_project(h, mods, norm_g[0], aw, _rope_tables())
    o = _attention(q, k, vt)
    h, u = _attn_output(o, aw["w_out"], h, mods, norm_g[0])
    h = _mlp(u, ffn_w_gate[0:1].astype(BF16), ffn_w_up[0:1].astype(BF16), ffn_w_down[0:1].astype(BF16),
             None, h, mods, norm_g[0], 0, N_ROWS, FFN_TILE)

    gw = _gla_weights(gla_w_in[0], gla_w_gate2[0], gla_b_gate[0])
    gq, gk, gv, gvt, sg, laf, lab = _gla_project(h, mods, norm_g[1], gw)
    go = _gla_scan(gq, gk, gv, gvt, laf, lab)
    wr = jnp.pad(moe_w_router[0], ((0, 0), (0, LANE - N_EXPERTS)))
    wr_hi = wr.astype(BF16)
    wr_lo = (wr - wr_hi.astype(F32)).astype(BF16)
    br = jnp.pad(moe_b_router[0], (0, LANE - N_EXPERTS), constant_values=-jnp.inf).reshape(1, LANE)
    h, u, combine = _gla_output(go, sg, gla_o_norm[0].reshape(1, GLA_DV), gla_w_out[0].astype(BF16), h, mods,
                                norm_g[1], wr_hi, wr_lo, br)
    h = _mlp(u, moe_w_gate[0].astype(BF16), moe_w_up[0].astype(BF16), moe_w_down[0].astype(BF16),
             combine, h, mods, norm_g[1], 1, N_LAT, EXPERT_TILE)
    return h.reshape(BATCH, SEQ, D)
```

```python
import jax
import jax.numpy as jnp
from jax import lax
from jax.experimental import pallas as pl
from jax.experimental.pallas import tpu as pltpu

F32 = jnp.float32
BF16 = jnp.bfloat16

D = 1024
BATCH = 4
SEQ = 4096
CTX = 256
GRID_W = 64
ROPE_THETA = 10000.0
EPS = 1e-6

N_LAT = BATCH * SEQ
N_CTX = BATCH * CTX
N_ROWS = N_LAT + N_CTX
MOD_ROWS = 8
CTX_MOD_ROW = BATCH

LANE = 128
ROW_TILE = 256

MLA_HEADS = 8
MLA_Q_RANK = 384
MLA_KV_RANK = 256
MLA_NOPE = 64
MLA_ROPE = 32
MLA_V = 64
GQA_HEADS = 8
GQA_KV_HEADS = 2
GQA_GROUP = GQA_HEADS // GQA_KV_HEADS
GQA_HEAD_DIM = 64
N_Q_HEADS = MLA_HEADS + GQA_HEADS

GLA_HEADS = 4
GLA_DK = 128
GLA_DV = 256
GLA_GATE_RANK = 16
GLA_GATE_NORM = 16.0
GLA_BLOCK = 128
GLA_SUB = 32

FFN_DIM = 2816
FFN_TILE = 1408
N_EXPERTS = 8
EXPERT_DIM = 3584
EXPERT_TILE = 896
MLP_ROW_TILE = 512

VMEM_LIMIT = 56 * 1024 * 1024


def _params(sem):
    return pltpu.CompilerParams(dimension_semantics=sem, vmem_limit_bytes=VMEM_LIMIT)


def _rms(x, g):
    return x * lax.rsqrt(jnp.mean(x * x, axis=-1, keepdims=True) + EPS) * g


def _silu(x):
    return x / (1.0 + jnp.exp(-x))


def _split_bf16(x):
    hi = x.astype(BF16)
    lo = (x - hi.astype(F32)).astype(BF16)
    return hi, lo


def _dot(a, b):
    return jnp.dot(a, b, preferred_element_type=F32)


def _dot_nt(a, b):
    return lax.dot_general(a, b, (((1,), (1,)), ((), ())), preferred_element_type=F32)


def _dot3(a, b_hi, b_lo):
    a_hi, a_lo = _split_bf16(a)
    return _dot(a_hi, b_hi) + (_dot(a_hi, b_lo) + _dot(a_lo, b_hi))


def _modulated(h, g_row, shift, scale):
    return _rms(h, g_row) * (1.0 + scale) + shift


def _rope(x, cos, sa, sb, p):
    return x * cos + pltpu.roll(x, LANE - p, 1) * sa + pltpu.roll(x, p, 1) * sb


def _mod_row(i, tile):
    r0 = i * tile
    return jnp.where(r0 < N_LAT, r0 // SEQ, CTX_MOD_ROW)


def _mod_spec(layer, tile):
    return pl.BlockSpec((None, None, 6, D), lambda i, *_: (layer, _mod_row(i, tile), 0, 0))


def _rope_block(i):
    r0 = i * ROW_TILE
    return jnp.where(r0 < N_LAT, (r0 % SEQ) // ROW_TILE, SEQ // ROW_TILE)


def _full(shape):
    return pl.BlockSpec(shape, lambda *_: (0,) * len(shape))


def _mod_kernel(c_ref, w_ref, b_ref, o_ref):
    w_hi, w_lo = _split_bf16(w_ref[...])
    o_ref[...] = _dot3(_silu(c_ref[...]), w_hi, w_lo) + b_ref[...]


def _mod_vectors(cc, mod_w, mod_b):
    depth, _, n = mod_w.shape
    tn = 1536
    return pl.pallas_call(
        _mod_kernel,
        out_shape=jax.ShapeDtypeStruct((depth, MOD_ROWS, n), F32),
        grid=(depth, n // tn),
        in_specs=[
            pl.BlockSpec((MOD_ROWS, D), lambda l, j: (0, 0)),
            pl.BlockSpec((None, D, tn), lambda l, j: (l, 0, j)),
            pl.BlockSpec((None, 1, tn), lambda l, j: (l, 0, j)),
        ],
        out_specs=pl.BlockSpec((None, MOD_ROWS, tn), lambda l, j: (l, 0, j)),
        compiler_params=_params(("parallel", "parallel")),
        name="mod_vectors",
    )(cc, mod_w, mod_b.reshape(depth, 1, n))


Q_LAT0, KV_LAT0, K_ROPE0 = 0, 384, 640
Q_B0 = 768
K_B0 = Q_B0 + GQA_HEADS * LANE
ATTN_IN_P = K_B0 + GQA_KV_HEADS * LANE
ATTN_V = 64
LOG2E = 1.4426950408889634


def _attn_proj_kernel(h_ref, m_ref, g_ref, w_in_ref, qn_ref, w_uq_ref, kvn_ref, w_uk_ref, w_uvt_ref,
                      w_vbt_ref, qkn_ref, ca_ref, saa_ref, sba_ref, cb_ref, sab_ref, sbb_ref,
                      q_ref, k_ref, vt_ref):
    u = _modulated(h_ref[...], g_ref[0:1, :], m_ref[0:1, :], m_ref[1:2, :]).astype(BF16)
    z = _dot(u, w_in_ref[...])
    q_lat = _rms(z[:, Q_LAT0:Q_LAT0 + MLA_Q_RANK], qn_ref[...]).astype(BF16)
    kv_lat = _rms(z[:, KV_LAT0:KV_LAT0 + MLA_KV_RANK], kvn_ref[...]).astype(BF16)
    q_a = _dot(q_lat, w_uq_ref[...])
    k_a = _dot(kv_lat, w_uk_ref[...])
    ca, saa, sba = ca_ref[...], saa_ref[...], sba_ref[...]
    cb, sab, sbb = cb_ref[...], sab_ref[...], sbb_ref[...]
    pa, pb = MLA_ROPE // 4, GQA_HEAD_DIM // 4
    scale_a = (MLA_NOPE + MLA_ROPE) ** -0.5 * LOG2E
    scale_b = GQA_HEAD_DIM ** -0.5 * LOG2E
    k_rope = _rope(z[:, K_ROPE0:K_ROPE0 + LANE], ca, saa, sba, pa)
    for hd in range(MLA_HEADS):
        sl = slice(hd * LANE, (hd + 1) * LANE)
        q_ref[:, sl] = (_rope(q_a[:, sl], ca, saa, sba, pa) * scale_a).astype(BF16)
        k_ref[:, sl] = (k_a[:, sl] + k_rope).astype(BF16)

    def head_norm(x, gain):
        ms = jnp.sum(x * x, axis=-1, keepdims=True) * (1.0 / GQA_HEAD_DIM)
        return x * lax.rsqrt(ms + EPS) * gain

    for hd in range(GQA_HEADS):
        x = head_norm(z[:, Q_B0 + hd * LANE:Q_B0 + (hd + 1) * LANE], qkn_ref[0:1, :])
        sl = slice((MLA_HEADS + hd) * LANE, (MLA_HEADS + hd + 1) * LANE)
        q_ref[:, sl] = (_rope(x, cb, sab, sbb, pb) * scale_b).astype(BF16)
    for kv in range(GQA_KV_HEADS):
        x = head_norm(z[:, K_B0 + kv * LANE:K_B0 + (kv + 1) * LANE], qkn_ref[1:2, :])
        x = _rope(x, cb, sab, sbb, pb).astype(BF16)
        for hd in range(kv * GQA_GROUP, (kv + 1) * GQA_GROUP):
            k_ref[:, (MLA_HEADS + hd) * LANE:(MLA_HEADS + hd + 1) * LANE] = x

    vt_ref[0:MLA_HEADS * ATTN_V, :] = _dot_nt(w_uvt_ref[...], kv_lat).astype(BF16)
    vbt = _dot_nt(w_vbt_ref[...], u).astype(BF16)
    for hd in range(GQA_HEADS):
        kv = hd // GQA_GROUP
        r0 = (MLA_HEADS + hd) * ATTN_V
        vt_ref[r0:r0 + ATTN_V, :] = vbt[kv * ATTN_V:(kv + 1) * ATTN_V, :]


def _attn_project(h, mods, g, w, tables):
    tm = ROW_TILE
    row = lambda n: pl.BlockSpec((tm, n), lambda i: (i, 0))
    tab = pl.BlockSpec((tm, LANE), lambda i: (_rope_block(i), 0))
    return pl.pallas_call(
        _attn_proj_kernel,
        out_shape=(jax.ShapeDtypeStruct((N_ROWS, N_Q_HEADS * LANE), BF16),
                   jax.ShapeDtypeStruct((N_ROWS, N_Q_HEADS * LANE), BF16),
                   jax.ShapeDtypeStruct((N_Q_HEADS * ATTN_V, N_ROWS), BF16)),
        grid=(N_ROWS // tm,),
        in_specs=[row(D), _mod_spec(0, tm), _full((4, D)), _full((D, ATTN_IN_P)),
                  _full((1, MLA_Q_RANK)), _full((MLA_Q_RANK, MLA_HEADS * LANE)),
                  _full((1, MLA_KV_RANK)), _full((MLA_KV_RANK, MLA_HEADS * LANE)),
                  _full((MLA_HEADS * ATTN_V, MLA_KV_RANK)), _full((GQA_KV_HEADS * ATTN_V, D)),
                  _full((2, LANE)), tab, tab, tab, tab, tab, tab],
        out_specs=(row(N_Q_HEADS * LANE), row(N_Q_HEADS * LANE),
                   pl.BlockSpec((N_Q_HEADS * ATTN_V, tm), lambda i: (0, i))),
        compiler_params=_params(("parallel",)),
        name="attn_project",
    )(h, mods, g, w["w_in"], w["q_norm"], w["w_uq"], w["kv_norm"], w["w_uk"], w["w_uvt"], w["w_vbt"],
      w["qk_norm"], *tables)


ATT_TQ = 256
ATT_TK = 512
ATT_LAT_TILES = SEQ // ATT_TQ
ATT_GROUP = 4


def _attn_kernel(q_ref, kc_ref, kl_ref, vtc_ref, vtl_ref, o_ref, sa_ref, sb_ref):
    qi = pl.program_id(2)
    heads = range(ATT_GROUP)
    n_lat = SEQ // ATT_TK

    def scores(s_ref, hd, k_ref, keys, n_keys):
        s = _dot_nt(k_ref[keys, hd * LANE:(hd + 1) * LANE], q_ref[:, hd * LANE:(hd + 1) * LANE])
        s_ref[hd, 0:n_keys, :] = s
        return jnp.max(s, axis=0, keepdims=True)

    def consume(s_ref, hd, vt_ref, keys, n_keys, cmax, carry):
        m, l, acc = carry
        m_new = jnp.maximum(m, cmax)
        alpha = jnp.exp2(m - m_new)
        p = jnp.exp2(s_ref[hd, 0:n_keys, :] - m_new)
        l = alpha * l + jnp.sum(p, axis=0, keepdims=True)
        acc = alpha * acc + _dot(vt_ref[hd * ATTN_V:(hd + 1) * ATTN_V, keys], p.astype(BF16))
        return m_new, l, acc

    def finish(carries):
        o_t = jnp.concatenate([acc / l for _, l, acc in carries], axis=0)
        o_ref[...] = o_t.T.astype(BF16)

    init = (jnp.full((1, ATT_TQ), -jnp.inf, F32), jnp.zeros((1, ATT_TQ), F32), jnp.zeros((ATTN_V, ATT_TQ), F32))
    every = slice(None)

    def lat_keys(j):
        return pl.ds(pl.multiple_of(j * ATT_TK, ATT_TK), ATT_TK)

    def step(cur_ref, nxt_ref, j, nxt_k_ref, nxt_keys, nxt_n, cmax, carries):
        nxt, out = [], []
        for hd in heads:
            nxt.append(scores(nxt_ref, hd, nxt_k_ref, nxt_keys, nxt_n))
            out.append(consume(cur_ref, hd, vtl_ref, lat_keys(j), ATT_TK, cmax[hd], carries[hd]))
        return tuple(nxt), tuple(out)

    @pl.when(qi < ATT_LAT_TILES)
    def _():
        cmax0 = tuple(scores(sa_ref, hd, kl_ref, lat_keys(0), ATT_TK) for hd in heads)

        def body(i, state):
            state = step(sa_ref, sb_ref, 2 * i, kl_ref, lat_keys(2 * i + 1), ATT_TK, *state)
            return step(sb_ref, sa_ref, 2 * i + 1, kl_ref, lat_keys(2 * i + 2), ATT_TK, *state)

        state = lax.fori_loop(0, n_lat // 2 - 1, body, (cmax0, (init,) * ATT_GROUP))
        state = step(sa_ref, sb_ref, n_lat - 2, kl_ref, lat_keys(n_lat - 1), ATT_TK, *state)
        cmax_c, carries = step(sb_ref, sa_ref, n_lat - 1, kc_ref, every, CTX, *state)
        finish([consume(sa_ref, hd, vtc_ref, every, CTX, cmax_c[hd], carries[hd]) for hd in heads])

    @pl.when(qi == ATT_LAT_TILES)
    def _():
        cmax_c = [scores(sa_ref, hd, kc_ref, every, CTX) for hd in heads]
        finish([consume(sa_ref, hd, vtc_ref, every, CTX, cmax_c[hd], init) for hd in heads])


def _attention(q, k, vt):
    gw = ATT_GROUP * LANE
    gv = ATT_GROUP * ATTN_V

    def q_row(b, qi):
        return jnp.where(qi < ATT_LAT_TILES, b * ATT_LAT_TILES + qi, N_LAT // ATT_TQ + b)

    return pl.pallas_call(
        _attn_kernel,
        out_shape=jax.ShapeDtypeStruct((N_ROWS, N_Q_HEADS * ATTN_V), BF16),
        grid=(BATCH, N_Q_HEADS // ATT_GROUP, ATT_LAT_TILES + 1),
        in_specs=[pl.BlockSpec((ATT_TQ, gw), lambda b, g, qi: (q_row(b, qi), g)),
                  pl.BlockSpec((CTX, gw), lambda b, g, qi: (N_LAT // CTX + b, g)),
                  pl.BlockSpec((SEQ, gw), lambda b, g, qi: (b, g)),
                  pl.BlockSpec((gv, CTX), lambda b, g, qi: (g, N_LAT // CTX + b)),
                  pl.BlockSpec((gv, SEQ), lambda b, g, qi: (g, b))],
        out_specs=pl.BlockSpec((ATT_TQ, gv), lambda b, g, qi: (q_row(b, qi), g)),
        scratch_shapes=[pltpu.VMEM((ATT_GROUP, ATT_TK, ATT_TQ), F32)] * 2,
        compiler_params=_params(("parallel", "parallel", "arbitrary")),
        name="attention",
    )(q, k, k, vt, vt)


def _mixer_epilogue(y, h, m_ref, g_ref):
    h_new = h + m_ref[2:3, :] * _rms(y, g_ref[1:2, :])
    u = _modulated(h_new, g_ref[2:3, :], m_ref[3:4, :], m_ref[4:5, :])
    return h_new, u


def _attn_out_kernel(o_ref, w_ref, h_ref, m_ref, g_ref, h_out_ref, u_ref):
    y = _dot(o_ref[...], w_ref[...])
    h_new, u = _mixer_epilogue(y, h_ref[...], m_ref, g_ref)
    h_out_ref[...] = h_new
    u_ref[...] = u.astype(BF16)


def _attn_output(o, w_out, h, mods, g):
    tm = ROW_TILE
    row = lambda n: pl.BlockSpec((tm, n), lambda i: (i, 0))
    return pl.pallas_call(
        _attn_out_kernel,
        out_shape=(jax.ShapeDtypeStruct((N_ROWS, D), F32), jax.ShapeDtypeStruct((N_ROWS, D), BF16)),
        grid=(N_ROWS // tm,),
        in_specs=[row(N_Q_HEADS * ATTN_V), _full((N_Q_HEADS * ATTN_V, D)), row(D), _mod_spec(0, tm),
                  _full((4, D))],
        out_specs=(row(D), row(D)),
        compiler_params=_params(("parallel",)),
        name="attn_output",
    )(o, w_out, h, mods, g)


def _swiglu_partial(x, wg_ref, wu_ref, wd_ref):
    a = _silu(_dot(x, wg_ref[...])) * _dot(x, wu_ref[...])
    return _dot(a.astype(BF16), wd_ref[...])


def _ffn_kernel(x_ref, wg_ref, wu_ref, wd_ref, h_ref, m_ref, g_ref, o_ref, acc_ref):
    f = pl.program_id(1)
    y = _swiglu_partial(x_ref[...], wg_ref, wu_ref, wd_ref)

    @pl.when(f == 0)
    def _():
        acc_ref[...] = y

    @pl.when(f > 0)
    def _():
        acc_ref[...] += y

    @pl.when(f == pl.num_programs(1) - 1)
    def _():
        o_ref[...] = h_ref[...] + m_ref[5:6, :] * _rms(acc_ref[...], g_ref[3:4, :])


def _ffn(x, wg, wu, wd, h, mods, g):
    tm = MLP_ROW_TILE
    row = lambda n: pl.BlockSpec((tm, n), lambda i, f: (i, 0))
    return pl.pallas_call(
        _ffn_kernel,
        out_shape=jax.ShapeDtypeStruct((N_ROWS, D), F32),
        grid=(N_ROWS // tm, FFN_DIM // FFN_TILE),
        in_specs=[row(D),
                  pl.BlockSpec((D, FFN_TILE), lambda i, f: (0, f)),
                  pl.BlockSpec((D, FFN_TILE), lambda i, f: (0, f)),
                  pl.BlockSpec((FFN_TILE, D), lambda i, f: (f, 0)),
                  row(D), _mod_spec(0, tm), _full((4, D))],
        out_specs=row(D),
        scratch_shapes=[pltpu.VMEM((tm, D), F32)],
        compiler_params=_params(("parallel", "arbitrary")),
        name="ffn_mlp",
    )(x, wg, wu, wd, h, mods, g)


N_ASSIGN = 2 * N_LAT
N_SORTED = N_ASSIGN + N_EXPERTS * MLP_ROW_TILE
N_SORTED_TILES = N_SORTED // MLP_ROW_TILE
ROUTE_TILE = 256


def _row_copy(src_ref, src_row, dst_ref, dst_row, sem):
    return pltpu.make_async_copy(src_ref.at[pl.ds(src_row, 1), :], dst_ref.at[pl.ds(dst_row, 1), :], sem)


def _dispatch_kernel(d1_ref, d2_ref, u_ref, zeros_ref, xs_ref, sem):
    del zeros_ref

    def start(t, _):
        _row_copy(u_ref, t, xs_ref, d1_ref[0, t], sem).start()
        _row_copy(u_ref, t, xs_ref, d2_ref[0, t], sem).start()
        return 0

    def wait(t, _):
        _row_copy(u_ref, t, xs_ref, d1_ref[0, t], sem).wait()
        _row_copy(u_ref, t, xs_ref, d2_ref[0, t], sem).wait()
        return 0

    lax.fori_loop(0, ROUTE_TILE, start, 0)
    lax.fori_loop(0, ROUTE_TILE, wait, 0)


def _route_spec():
    return pl.BlockSpec((None, 1, ROUTE_TILE), lambda i: (i, 0, 0), memory_space=pltpu.SMEM)


def _dispatch(d1, d2, u):
    tm = ROUTE_TILE
    return pl.pallas_call(
        _dispatch_kernel,
        out_shape=jax.ShapeDtypeStruct((N_SORTED, D), F32),
        grid=(N_LAT // tm,),
        in_specs=[_route_spec(), _route_spec(), pl.BlockSpec((tm, D), lambda i: (i, 0)),
                  pl.BlockSpec(memory_space=pl.ANY)],
        out_specs=pl.BlockSpec(memory_space=pl.ANY),
        scratch_shapes=[pltpu.SemaphoreType.DMA(())],
        input_output_aliases={3: 0},
        compiler_params=_params(("arbitrary",)),
        name="moe_dispatch",
    )(d1, d2, u, jnp.zeros((N_SORTED, D), F32))


def _experts_kernel(te_ref, nt_ref, x_ref, wg_ref, wu_ref, wd_ref, o_ref, xb_ref):
    del te_ref
    i, f = pl.program_id(0), pl.program_id(1)
    live = i < nt_ref[0]

    @pl.when(live & (f == 0))
    def _():
        xb_ref[...] = x_ref[...].astype(BF16)

    @pl.when(live)
    def _():
        y = _swiglu_partial(xb_ref[...], wg_ref, wu_ref, wd_ref)

        @pl.when(f == 0)
        def _():
            o_ref[...] = y

        @pl.when(f > 0)
        def _():
            o_ref[...] += y

    @pl.when(jnp.logical_not(live) & (f == 0))
    def _():
        o_ref[...] = jnp.zeros_like(o_ref)


def _experts(tile_expert, n_tiles, xs, wg, wu, wd):
    tm = MLP_ROW_TILE
    nf = EXPERT_DIM // EXPERT_TILE

    def f_blk(i, f, nt):
        return jnp.where(i < nt[0], f, nf - 1)

    grid_spec = pltpu.PrefetchScalarGridSpec(
        num_scalar_prefetch=2,
        grid=(N_SORTED_TILES, nf),
        in_specs=[pl.BlockSpec((tm, D), lambda i, f, te, nt: (jnp.minimum(i, nt[0] - 1), 0)),
                  pl.BlockSpec((None, D, EXPERT_TILE), lambda i, f, te, nt: (te[i], 0, f_blk(i, f, nt))),
                  pl.BlockSpec((None, D, EXPERT_TILE), lambda i, f, te, nt: (te[i], 0, f_blk(i, f, nt))),
                  pl.BlockSpec((None, EXPERT_TILE, D), lambda i, f, te, nt: (te[i], f_blk(i, f, nt), 0))],
        out_specs=pl.BlockSpec((tm, D), lambda i, f, te, nt: (i, 0)),
        scratch_shapes=[pltpu.VMEM((tm, D), BF16)],
    )
    return pl.pallas_call(
        _experts_kernel,
        out_shape=jax.ShapeDtypeStruct((N_SORTED, D), F32),
        grid_spec=grid_spec,
        compiler_params=_params(("arbitrary", "arbitrary")),
        name="moe_experts",
    )(tile_expert, n_tiles, xs, wg, wu, wd)


def _combine_kernel(d1_ref, d2_ref, w_ref, h_ref, m_ref, g_ref, ys_ref, o_ref, buf_ref, sem):
    def start(t, _):
        _row_copy(ys_ref, d1_ref[0, t], buf_ref.at[0], t, sem).start()
        _row_copy(ys_ref, d2_ref[0, t], buf_ref.at[1], t, sem).start()
        return 0

    def wait(t, _):
        _row_copy(ys_ref, d1_ref[0, t], buf_ref.at[0], t, sem).wait()
        _row_copy(ys_ref, d2_ref[0, t], buf_ref.at[1], t, sem).wait()
        return 0

    lax.fori_loop(0, ROUTE_TILE, start, 0)
    lax.fori_loop(0, ROUTE_TILE, wait, 0)
    w = w_ref[...]
    lane = lax.broadcasted_iota(jnp.int32, w.shape, 1)
    w1 = jnp.sum(jnp.where(lane == 0, w, 0.0), axis=-1, keepdims=True)
    w2 = jnp.sum(jnp.where(lane == 1, w, 0.0), axis=-1, keepdims=True)
    y = w1 * buf_ref[0] + w2 * buf_ref[1]
    o_ref[...] = h_ref[...] + m_ref[5:6, :] * _rms(y, g_ref[3:4, :])


def _combine(d1, d2, wts, h, mods, g, ys):
    tm = ROUTE_TILE
    row = lambda n: pl.BlockSpec((tm, n), lambda i: (i, 0))
    return pl.pallas_call(
        _combine_kernel,
        out_shape=jax.ShapeDtypeStruct((N_LAT, D), F32),
        grid=(N_LAT // tm,),
        in_specs=[_route_spec(), _route_spec(), row(LANE), row(D), _mod_spec(1, tm), _full((4, D)),
                  pl.BlockSpec(memory_space=pl.ANY)],
        out_specs=row(D),
        scratch_shapes=[pltpu.VMEM((2, tm, D), F32), pltpu.SemaphoreType.DMA(())],
        compiler_params=_params(("arbitrary",)),
        name="moe_combine",
    )(d1, d2, wts, h, mods, g, ys)


def _routing(route):
    tm = MLP_ROW_TILE
    e1, e2 = route[:, 0], route[:, 1]
    experts = jnp.arange(N_EXPERTS, dtype=jnp.int32)
    hit = (e1[:, None] == experts).astype(jnp.int32) + (e2[:, None] == experts).astype(jnp.int32)
    pos = jnp.cumsum(hit, axis=0) - hit
    padded = (jnp.sum(hit, axis=0) + tm - 1) // tm * tm
    ends = jnp.cumsum(padded)
    starts = ends - padded
    d1 = starts[e1] + jnp.take_along_axis(pos, e1[:, None], axis=1)[:, 0]
    d2 = starts[e2] + jnp.take_along_axis(pos, e2[:, None], axis=1)[:, 0]
    n_tiles = ends[-1] // tm
    tile_start = jnp.minimum(jnp.arange(N_SORTED_TILES, dtype=jnp.int32), n_tiles - 1) * tm
    tile_expert = jnp.sum((tile_start[:, None] >= ends[None, :]).astype(jnp.int32), axis=1)
    shape = (N_LAT // ROUTE_TILE, 1, ROUTE_TILE)
    return (d1.astype(jnp.int32).reshape(shape), d2.astype(jnp.int32).reshape(shape),
            tile_expert.astype(jnp.int32), n_tiles.astype(jnp.int32).reshape(1))


GLA_Q0, GLA_K0, GLA_V0, GLA_G0, GLA_R0 = 0, 512, 1024, 2048, 3072
GLA_IN_P = GLA_R0 + LANE


def _gla_proj_kernel(h_ref, m_ref, g_ref, w_in_ref, w_vt_ref, w_gate_ref, b_gate_ref,
                     q_ref, k_ref, v_ref, vt_ref, sg_ref, laf_ref, lab_ref):
    u = _modulated(h_ref[...], g_ref[0:1, :], m_ref[0:1, :], m_ref[1:2, :]).astype(BF16)
    z = _dot(u, w_in_ref[...])
    q_ref[...] = (z[:, GLA_Q0:GLA_K0] * (GLA_DK ** -0.5)).astype(BF16)
    k_ref[...] = z[:, GLA_K0:GLA_V0].astype(BF16)
    v_ref[...] = z[:, GLA_V0:GLA_G0].astype(BF16)
    vt_ref[...] = _dot_nt(w_vt_ref[...], u).astype(BF16)
    sg_ref[...] = _silu(z[:, GLA_G0:GLA_R0]).astype(BF16)
    zg = _dot(z[:, GLA_R0:].astype(BF16), w_gate_ref[...]) + b_gate_ref[...]
    la = (jnp.minimum(zg, 0.0) - jnp.log(1.0 + jnp.exp(-jnp.abs(zg)))) * (1.0 / GLA_GATE_NORM)
    laf_ref[...] = la[:, :GLA_HEADS * GLA_DK]
    lab_ref[...] = la[:, GLA_HEADS * GLA_DK:]


def _gla_project(h, mods, g, w):
    tm = ROW_TILE
    row = lambda n, dt=None: pl.BlockSpec((tm, n), lambda i: (i, 0))
    hk, hv = GLA_HEADS * GLA_DK, GLA_HEADS * GLA_DV
    sds = jax.ShapeDtypeStruct
    return pl.pallas_call(
        _gla_proj_kernel,
        out_shape=(sds((N_ROWS, hk), BF16), sds((N_ROWS, hk), BF16), sds((N_ROWS, hv), BF16),
                   sds((hv, N_ROWS), BF16), sds((N_ROWS, hv), BF16),
                   sds((N_ROWS, hk), F32), sds((N_ROWS, hk), F32)),
        grid=(N_ROWS // tm,),
        in_specs=[row(D), _mod_spec(1, tm), _full((4, D)), _full((D, GLA_IN_P)), _full((hv, D)),
                  _full((LANE, 2 * hk)), _full((1, 2 * hk))],
        out_specs=(row(hk), row(hk), row(hv), pl.BlockSpec((hv, tm), lambda i: (0, i)), row(hv),
                   row(hk), row(hk)),
        compiler_params=_params(("parallel",)),
        name="gla_project",
    )(h, mods, g, w["w_in"], w["w_vt"], w["w_gate"], w["b_gate"])


GLA_LAT_BLOCKS = SEQ // GLA_BLOCK
GLA_CTX_BLOCKS = CTX // GLA_BLOCK
GLA_STRIPS = GLA_BLOCK // GLA_SUB


def _gla_block(q, k, v, vt, la, state, reverse):
    n = GLA_BLOCK
    r = lax.broadcasted_iota(jnp.int32, (n, n), 0)
    c = lax.broadcasted_iota(jnp.int32, (n, n), 1)
    keep = (c >= r) if reverse else (c <= r)
    tri = jnp.where(keep, 1.0, 0.0).astype(BF16)
    la_hi, la_lo = _split_bf16(la)
    cum = _dot(tri, la_hi) + _dot(tri, la_lo)
    total = cum[0:1, :] if reverse else cum[n - 1:n, :]
    k_state = (k * jnp.exp(total - cum)).astype(BF16)
    new_state = state * jnp.exp(total) + _dot(vt, k_state)
    if q is None:
        return None, new_state
    o = _dot_nt((q * jnp.exp(cum)).astype(BF16), state.astype(BF16))
    row = lax.broadcasted_iota(jnp.int32, (n, GLA_DK), 0)
    strips = []
    for i in range(GLA_STRIPS):
        lo, hi = i * GLA_SUB, (i + 1) * GLA_SUB
        if reverse:
            ref = cum[hi:hi + 1, :] if i < GLA_STRIPS - 1 else jnp.zeros((1, GLA_DK), F32)
            live = row >= lo
        else:
            ref = cum[lo - 1:lo, :] if i > 0 else jnp.zeros((1, GLA_DK), F32)
            live = row < hi
        q_loc = (q[lo:hi, :] * jnp.exp(cum[lo:hi, :] - ref)).astype(BF16)
        k_loc = jnp.where(live, k * jnp.exp(ref - cum), 0.0).astype(BF16)
        strips.append(_dot_nt(q_loc, k_loc))
    scores = jnp.where(keep, jnp.concatenate(strips, axis=0), 0.0).astype(BF16)
    return o + _dot(scores, v), new_state


def _gla_kernel(q_ref, kl_ref, kc_ref, vl_ref, vc_ref, vtl_ref, vtc_ref,
                lafl_ref, lafc_ref, labl_ref, labc_ref, o_ref, sf_ref, sb_ref):
    n = GLA_BLOCK
    sf_ref[...] = jnp.zeros_like(sf_ref)
    sb_ref[...] = jnp.zeros_like(sb_ref)

    for j in range(GLA_CTX_BLOCKS):
        fs = slice(j * n, (j + 1) * n)
        bs = slice((GLA_CTX_BLOCKS - 1 - j) * n, (GLA_CTX_BLOCKS - j) * n)
        _, sf = _gla_block(None, kc_ref[fs, :].astype(F32), None, vtc_ref[:, fs], lafc_ref[fs, :],
                           sf_ref[...], False)
        sf_ref[...] = sf
        _, sb = _gla_block(None, kc_ref[bs, :].astype(F32), None, vtc_ref[:, bs], labc_ref[bs, :],
                           sb_ref[...], True)
        sb_ref[...] = sb

    def step(j, accumulate):
        for reverse, la_ref, s_ref in ((False, lafl_ref, sf_ref), (True, labl_ref, sb_ref)):
            blk = (GLA_LAT_BLOCKS - 1 - j) if reverse else j
            off = pl.multiple_of(blk * n, n)
            rows = pl.ds(off, n)
            o, s_new = _gla_block(q_ref[rows, :].astype(F32), kl_ref[rows, :].astype(F32), vl_ref[rows, :],
                                  vtl_ref[:, rows], la_ref[rows, :], s_ref[...], reverse)
            s_ref[...] = s_new
            if accumulate:
                o_ref[rows, :] += o
            else:
                o_ref[rows, :] = o

    half = GLA_LAT_BLOCKS // 2

    def first(j, _):
        step(j, False)
        return 0

    def second(j, _):
        step(j, True)
        return 0

    lax.fori_loop(0, half, first, 0)
    lax.fori_loop(half, GLA_LAT_BLOCKS, second, 0)


def _gla_scan(q, k, v, vt, laf, lab):
    lat = lambda n: pl.BlockSpec((SEQ, n), lambda b, hd: (b, hd))
    ctx = lambda n: pl.BlockSpec((CTX, n), lambda b, hd: (N_LAT // CTX + b, hd))
    return pl.pallas_call(
        _gla_kernel,
        out_shape=jax.ShapeDtypeStruct((N_LAT, GLA_HEADS * GLA_DV), F32),
        grid=(BATCH, GLA_HEADS),
        in_specs=[lat(GLA_DK), lat(GLA_DK), ctx(GLA_DK), lat(GLA_DV), ctx(GLA_DV),
                  pl.BlockSpec((GLA_DV, SEQ), lambda b, hd: (hd, b)),
                  pl.BlockSpec((GLA_DV, CTX), lambda b, hd: (hd, N_LAT // CTX + b)),
                  lat(GLA_DK), ctx(GLA_DK), lat(GLA_DK), ctx(GLA_DK)],
        out_specs=lat(GLA_DV),
        scratch_shapes=[pltpu.VMEM((GLA_DV, GLA_DK), F32), pltpu.VMEM((GLA_DV, GLA_DK), F32)],
        compiler_params=_params(("parallel", "parallel")),
        name="gla_scan",
    )(q, k, k, v, v, vt, vt, laf, laf, lab, lab)


def _gla_out_kernel(o_ref, sg_ref, on_ref, w_ref, h_ref, m_ref, g_ref, wr_hi_ref, wr_lo_ref, br_ref,
                    h_out_ref, u_ref, route_ref, wts_ref):
    parts = []
    for hd in range(GLA_HEADS):
        sl = slice(hd * GLA_DV, (hd + 1) * GLA_DV)
        parts.append(_rms(o_ref[:, sl], on_ref[...]) * sg_ref[:, sl].astype(F32))
    y = _dot(jnp.concatenate(parts, axis=-1).astype(BF16), w_ref[...])
    h_new, u = _mixer_epilogue(y, h_ref[...], m_ref, g_ref)
    h_out_ref[...] = h_new
    u_ref[...] = u
    logits = _dot3(u, wr_hi_ref[...], wr_lo_ref[...]) + br_ref[...]
    lane = lax.broadcasted_iota(jnp.int32, logits.shape, 1)
    l1 = jnp.max(logits, axis=-1, keepdims=True)
    i1 = jnp.min(jnp.where(logits == l1, lane, LANE), axis=-1, keepdims=True)
    rest = jnp.where(lane == i1, -jnp.inf, logits)
    l2 = jnp.max(rest, axis=-1, keepdims=True)
    i2 = jnp.min(jnp.where(rest == l2, lane, LANE), axis=-1, keepdims=True)
    e2 = jnp.exp(l2 - l1)
    route_ref[...] = jnp.where(lane == 0, i1, jnp.where(lane == 1, i2, 0))
    wts_ref[...] = jnp.where(lane == 0, 1.0 / (1.0 + e2), jnp.where(lane == 1, e2 / (1.0 + e2), 0.0))


def _gla_output(o, sg, o_norm, w_out, h, mods, g, wr_hi, wr_lo, br):
    tm = ROW_TILE
    row = lambda n: pl.BlockSpec((tm, n), lambda i: (i, 0))
    hv = GLA_HEADS * GLA_DV
    return pl.pallas_call(
        _gla_out_kernel,
        out_shape=(jax.ShapeDtypeStruct((N_LAT, D), F32), jax.ShapeDtypeStruct((N_LAT, D), F32),
                   jax.ShapeDtypeStruct((N_LAT, LANE), jnp.int32), jax.ShapeDtypeStruct((N_LAT, LANE), F32)),
        grid=(N_LAT // tm,),
        in_specs=[row(hv), row(hv), _full((1, GLA_DV)), _full((hv, D)), row(D), _mod_spec(1, tm),
                  _full((4, D)), _full((D, LANE)), _full((D, LANE)), _full((1, LANE))],
        out_specs=(row(D), row(D), row(LANE), row(LANE)),
        compiler_params=_params(("parallel",)),
        name="gla_output",
    )(o, sg, o_norm, w_out, h, mods, g, wr_hi, wr_lo, br)


def _slots(w, n_heads, width):
    k = w.shape[0]
    w = w.reshape(k, n_heads, width)
    return jnp.pad(w, ((0, 0), (0, 0), (0, LANE - width))).reshape(k, n_heads * LANE)


def _attn_weights(w_in, q_norm, w_uq, kv_norm, w_ukv, qk_norm, w_out):
    c = 0
    cols = {}
    for name, n in (("q_lat", MLA_Q_RANK), ("kv_lat", MLA_KV_RANK), ("k_rope", MLA_ROPE),
                    ("q_b", GQA_HEADS * GQA_HEAD_DIM), ("k_b", GQA_KV_HEADS * GQA_HEAD_DIM),
                    ("v_b", GQA_KV_HEADS * GQA_HEAD_DIM)):
        cols[name] = w_in[:, c:c + n]
        c += n
    k_rope = jnp.pad(cols["k_rope"], ((0, 0), (MLA_NOPE, LANE - MLA_NOPE - MLA_ROPE)))
    w_in_p = jnp.concatenate([cols["q_lat"], cols["kv_lat"], k_rope,
                              _slots(cols["q_b"], GQA_HEADS, GQA_HEAD_DIM),
                              _slots(cols["k_b"], GQA_KV_HEADS, GQA_HEAD_DIM)], axis=1)
    ukv = w_ukv.reshape(MLA_KV_RANK, MLA_HEADS, MLA_NOPE + MLA_V)
    w_uk = _slots(ukv[:, :, :MLA_NOPE].reshape(MLA_KV_RANK, -1), MLA_HEADS, MLA_NOPE)
    w_uv = ukv[:, :, MLA_NOPE:].reshape(MLA_KV_RANK, MLA_HEADS * MLA_V)
    return {
        "w_in": w_in_p.astype(BF16),
        "q_norm": q_norm.reshape(1, -1),
        "w_uq": _slots(w_uq, MLA_HEADS, MLA_NOPE + MLA_ROPE).astype(BF16),
        "kv_norm": kv_norm.reshape(1, -1),
        "w_uk": w_uk.astype(BF16),
        "w_uvt": w_uv.T.astype(BF16),
        "w_vbt": cols["v_b"].T.astype(BF16),
        "qk_norm": jnp.pad(qk_norm, ((0, 0), (0, LANE - GQA_HEAD_DIM))),
        "w_out": w_out.astype(BF16),
    }


def _axial_tables(d):
    half, p = d // 2, d // 4
    t = jnp.arange(SEQ, dtype=jnp.int32)
    freqs = ROPE_THETA ** (-jnp.arange(0, half, 2, dtype=F32) / half)
    cos, sa, sb = [], [], []
    for pos in (t // GRID_W, t % GRID_W):
        ang = pos.astype(F32)[:, None] * freqs[None, :]
        c, s, z = jnp.cos(ang), jnp.sin(ang), jnp.zeros_like(ang)
        cos += [c, c]
        sa += [-s, z]
        sb += [z, s]
    return [jnp.concatenate(x, axis=-1) for x in (cos, sa, sb)]


def _rope_tables():
    def embed(tbl, left, fill):
        right = LANE - left - tbl.shape[1]
        tbl = jnp.pad(tbl, ((0, 0), (left, right)), constant_values=fill)
        return jnp.pad(tbl, ((0, ROW_TILE), (0, 0)), constant_values=fill)

    out = []
    for d, left in ((MLA_ROPE, MLA_NOPE), (GQA_HEAD_DIM, 0)):
        cos, sa, sb = _axial_tables(d)
        out += [embed(cos, left, 1.0), embed(sa, left, 0.0), embed(sb, left, 0.0)]
    return out


def _gla_weights(w_in, w_gate2, b_gate):
    hk, hv = GLA_HEADS * GLA_DK, GLA_HEADS * GLA_DV
    r = jnp.pad(w_in[:, 2 * hk + 2 * hv:], ((0, 0), (0, LANE - 2 * GLA_GATE_RANK)))
    w_in_p = jnp.concatenate([w_in[:, :2 * hk + 2 * hv], r], axis=1)
    w_gate = jnp.zeros((LANE, 2 * hk), F32)
    w_gate = w_gate.at[:GLA_GATE_RANK, :hk].set(w_gate2[0])
    w_gate = w_gate.at[GLA_GATE_RANK:2 * GLA_GATE_RANK, hk:].set(w_gate2[1])
    return {
        "w_in": w_in_p.astype(BF16),
        "w_vt": w_in[:, 2 * hk:2 * hk + hv].T.astype(BF16),
        "w_gate": w_gate.astype(BF16),
        "b_gate": b_gate.reshape(1, 2 * hk),
    }


def kernel(x, c, ctx, c_ctx, mod_w, mod_b, norm_g, attn_w_in, attn_q_norm, attn_w_uq, attn_kv_norm, attn_w_ukv,
           attn_qk_norm, attn_w_out, gla_w_in, gla_w_gate2, gla_b_gate, gla_o_norm, gla_w_out, ffn_w_gate,
           ffn_w_up, ffn_w_down, moe_w_router, moe_b_router, moe_w_gate, moe_w_up, moe_w_down):
    assert x.shape == (BATCH, SEQ, D) and ctx.shape == (BATCH, CTX, D)
    h = jnp.concatenate([x.reshape(N_LAT, D), ctx.reshape(N_CTX, D)], axis=0)
    cc = jnp.concatenate([c, c_ctx[None, :], jnp.zeros((MOD_ROWS - BATCH - 1, D), F32)], axis=0)
    mods = _mod_vectors(cc, mod_w, mod_b).reshape(mod_w.shape[0], MOD_ROWS, 6, D)

    aw = _attn_weights(attn_w_in[0], attn_q_norm[0], attn_w_uq[0], attn_kv_norm[0], attn_w_ukv[0],
                       attn_qk_norm[0], attn_w_out[0])
    q, k, vt = _attn_project(h, mods, norm_g[0], aw, _rope_tables())
    o = _attention(q, k, vt)
    h, u = _attn_output(o, aw["w_out"], h, mods, norm_g[0])
    h = _ffn(u, ffn_w_gate[0].astype(BF16), ffn_w_up[0].astype(BF16), ffn_w_down[0].astype(BF16), h, mods,
             norm_g[0])

    gw = _gla_weights(gla_w_in[0], gla_w_gate2[0], gla_b_gate[0])
    gq, gk, gv, gvt, sg, laf, lab = _gla_project(h, mods, norm_g[1], gw)
    go = _gla_scan(gq, gk, gv, gvt, laf, lab)
    wr = jnp.pad(moe_w_router[0], ((0, 0), (0, LANE - N_EXPERTS)))
    wr_hi = wr.astype(BF16)
    wr_lo = (wr - wr_hi.astype(F32)).astype(BF16)
    br = jnp.pad(moe_b_router[0], (0, LANE - N_EXPERTS), constant_values=-jnp.inf).reshape(1, LANE)
    h, u, route, wts = _gla_output(go, sg, gla_o_norm[0].reshape(1, GLA_DV), gla_w_out[0].astype(BF16), h,
                                   mods, norm_g[1], wr_hi, wr_lo, br)
    d1, d2, tile_expert, n_tiles = _routing(route)
    xs = _dispatch(d1, d2, u)
    ys = _experts(tile_expert, n_tiles, xs, moe_w_gate[0].astype(BF16), moe_w_up[0].astype(BF16),
                  moe_w_down[0].astype(BF16))
    h = _combine(d1, d2, wts, h, mods, norm_g[1], ys)
    return h.reshape(BATCH, SEQ, D)
```

```python
import jax
import jax.numpy as jnp
from jax import lax
from jax.experimental import pallas as pl
from jax.experimental.pallas import tpu as pltpu

F32 = jnp.float32
BF16 = jnp.bfloat16

D = 1024
BATCH = 4
SEQ = 4096
CTX = 256
GRID_W = 64
ROPE_THETA = 10000.0
EPS = 1e-6

N_LAT = BATCH * SEQ
N_CTX = BATCH * CTX
N_ROWS = N_LAT + N_CTX
MOD_ROWS = 8
CTX_MOD_ROW = BATCH

LANE = 128
ROW_TILE = 256

MLA_HEADS = 8
MLA_Q_RANK = 384
MLA_KV_RANK = 256
MLA_NOPE = 64
MLA_ROPE = 32
MLA_V = 64
GQA_HEADS = 8
GQA_KV_HEADS = 2
GQA_GROUP = GQA_HEADS // GQA_KV_HEADS
GQA_HEAD_DIM = 64
N_Q_HEADS = MLA_HEADS + GQA_HEADS

GLA_HEADS = 4
GLA_DK = 128
GLA_DV = 256
GLA_GATE_RANK = 16
GLA_GATE_NORM = 16.0
GLA_BLOCK = 128
GLA_SUB = 32

MXU_WIDTH = 256
FFN_DIM = 2816
N_EXPERTS = 8
EXPERT_DIM = 3584
EXPERT_TILE = EXPERT_DIM // 2
MLP_CHUNK = 2 * MXU_WIDTH
MLP_ROW_TILE = 512

VMEM_LIMIT = 56 * 1024 * 1024


def _params(sem):
    return pltpu.CompilerParams(dimension_semantics=sem, vmem_limit_bytes=VMEM_LIMIT)


def _rms(x, g):
    return x * lax.rsqrt(jnp.mean(x * x, axis=-1, keepdims=True) + EPS) * g


def _silu(x):
    return x / (1.0 + jnp.exp(-x))


def _split_bf16(x):
    hi = x.astype(BF16)
    lo = (x - hi.astype(F32)).astype(BF16)
    return hi, lo


def _dot(a, b):
    return jnp.dot(a, b, preferred_element_type=F32)


def _dot_nt(a, b):
    return lax.dot_general(a, b, (((1,), (1,)), ((), ())), preferred_element_type=F32)


def _dot3(a, b_hi, b_lo):
    a_hi, a_lo = _split_bf16(a)
    return _dot(a_hi, b_hi) + (_dot(a_hi, b_lo) + _dot(a_lo, b_hi))


def _modulated(h, g_row, shift, scale):
    return _rms(h, g_row) * (1.0 + scale) + shift


def _rope(x, cos, sa, sb, p):
    return x * cos + pltpu.roll(x, LANE - p, 1) * sa + pltpu.roll(x, p, 1) * sb


def _mod_row(i, tile):
    r0 = i * tile
    return jnp.where(r0 < N_LAT, r0 // SEQ, CTX_MOD_ROW)


def _mod_spec(layer, tile):
    return pl.BlockSpec((None, None, 6, D), lambda i, *_: (layer, _mod_row(i, tile), 0, 0))


def _rope_block(i):
    r0 = i * ROW_TILE
    return jnp.where(r0 < N_LAT, (r0 % SEQ) // ROW_TILE, SEQ // ROW_TILE)


def _full(shape):
    return pl.BlockSpec(shape, lambda *_: (0,) * len(shape))


def _mod_kernel(c_ref, w_ref, b_ref, o_ref):
    w_hi, w_lo = _split_bf16(w_ref[...])
    o_ref[...] = _dot3(_silu(c_ref[...]), w_hi, w_lo) + b_ref[...]


def _mod_vectors(cc, mod_w, mod_b):
    depth, _, n = mod_w.shape
    tn = 1536
    return pl.pallas_call(
        _mod_kernel,
        out_shape=jax.ShapeDtypeStruct((depth, MOD_ROWS, n), F32),
        grid=(depth, n // tn),
        in_specs=[
            pl.BlockSpec((MOD_ROWS, D), lambda l, j: (0, 0)),
            pl.BlockSpec((None, D, tn), lambda l, j: (l, 0, j)),
            pl.BlockSpec((None, 1, tn), lambda l, j: (l, 0, j)),
        ],
        out_specs=pl.BlockSpec((None, MOD_ROWS, tn), lambda l, j: (l, 0, j)),
        compiler_params=_params(("parallel", "parallel")),
        name="mod_vectors",
    )(cc, mod_w, mod_b.reshape(depth, 1, n))


Q_LAT0, KV_LAT0, K_ROPE0 = 0, 384, 640
Q_B0 = 768
K_B0 = Q_B0 + GQA_HEADS * LANE
ATTN_IN_P = K_B0 + GQA_KV_HEADS * LANE
ATTN_V = 64
ATTN_VA = ATTN_V + 16
LOG2E = 1.4426950408889634


def _attn_proj_kernel(h_ref, m_ref, g_ref, w_in_ref, qn_ref, w_uq_ref, kvn_ref, w_uk_ref, w_uvt_ref,
                      w_vbt_ref, qkn_ref, ca_ref, saa_ref, sba_ref, cb_ref, sab_ref, sbb_ref,
                      q_ref, k_ref, vt_ref):
    u = _modulated(h_ref[...], g_ref[0:1, :], m_ref[0:1, :], m_ref[1:2, :]).astype(BF16)
    z = _dot(u, w_in_ref[...])
    q_lat = _rms(z[:, Q_LAT0:Q_LAT0 + MLA_Q_RANK], qn_ref[...]).astype(BF16)
    kv_lat = _rms(z[:, KV_LAT0:KV_LAT0 + MLA_KV_RANK], kvn_ref[...]).astype(BF16)
    q_a = _dot(q_lat, w_uq_ref[...])
    k_a = _dot(kv_lat, w_uk_ref[...])
    ca, saa, sba = ca_ref[...], saa_ref[...], sba_ref[...]
    cb, sab, sbb = cb_ref[...], sab_ref[...], sbb_ref[...]
    pa, pb = MLA_ROPE // 4, GQA_HEAD_DIM // 4
    scale_a = (MLA_NOPE + MLA_ROPE) ** -0.5 * LOG2E
    scale_b = GQA_HEAD_DIM ** -0.5 * LOG2E
    k_rope = _rope(z[:, K_ROPE0:K_ROPE0 + LANE], ca, saa, sba, pa)
    for hd in range(MLA_HEADS):
        sl = slice(hd * LANE, (hd + 1) * LANE)
        q_ref[:, sl] = (_rope(q_a[:, sl], ca, saa, sba, pa) * scale_a).astype(BF16)
        k_ref[:, sl] = (k_a[:, sl] + k_rope).astype(BF16)

    def head_norm(x, gain):
        ms = jnp.sum(x * x, axis=-1, keepdims=True) * (1.0 / GQA_HEAD_DIM)
        return x * lax.rsqrt(ms + EPS) * gain

    for hd in range(GQA_HEADS):
        x = head_norm(z[:, Q_B0 + hd * LANE:Q_B0 + (hd + 1) * LANE], qkn_ref[0:1, :])
        sl = slice((MLA_HEADS + hd) * LANE, (MLA_HEADS + hd + 1) * LANE)
        q_ref[:, sl] = (_rope(x, cb, sab, sbb, pb) * scale_b).astype(BF16)
    for kv in range(GQA_KV_HEADS):
        x = head_norm(z[:, K_B0 + kv * LANE:K_B0 + (kv + 1) * LANE], qkn_ref[1:2, :])
        x = _rope(x, cb, sab, sbb, pb).astype(BF16)
        for hd in range(kv * GQA_GROUP, (kv + 1) * GQA_GROUP):
            k_ref[:, (MLA_HEADS + hd) * LANE:(MLA_HEADS + hd + 1) * LANE] = x

    vat = _dot_nt(w_uvt_ref[...], kv_lat).astype(BF16)
    vbt = _dot_nt(w_vbt_ref[...], u).astype(BF16)
    ones = jnp.ones((ATTN_VA - ATTN_V, vat.shape[1]), BF16)
    for hd in range(N_Q_HEADS):
        if hd < MLA_HEADS:
            v_t = vat[hd * ATTN_V:(hd + 1) * ATTN_V, :]
        else:
            kv = (hd - MLA_HEADS) // GQA_GROUP
            v_t = vbt[kv * ATTN_V:(kv + 1) * ATTN_V, :]
        vt_ref[hd * ATTN_VA:hd * ATTN_VA + ATTN_V, :] = v_t
        vt_ref[hd * ATTN_VA + ATTN_V:(hd + 1) * ATTN_VA, :] = ones


def _attn_project(h, mods, g, w, tables):
    tm = ROW_TILE
    row = lambda n: pl.BlockSpec((tm, n), lambda i: (i, 0))
    tab = pl.BlockSpec((tm, LANE), lambda i: (_rope_block(i), 0))
    return pl.pallas_call(
        _attn_proj_kernel,
        out_shape=(jax.ShapeDtypeStruct((N_ROWS, N_Q_HEADS * LANE), BF16),
                   jax.ShapeDtypeStruct((N_ROWS, N_Q_HEADS * LANE), BF16),
                   jax.ShapeDtypeStruct((N_Q_HEADS * ATTN_VA, N_ROWS), BF16)),
        grid=(N_ROWS // tm,),
        in_specs=[row(D), _mod_spec(0, tm), _full((4, D)), _full((D, ATTN_IN_P)),
                  _full((1, MLA_Q_RANK)), _full((MLA_Q_RANK, MLA_HEADS * LANE)),
                  _full((1, MLA_KV_RANK)), _full((MLA_KV_RANK, MLA_HEADS * LANE)),
                  _full((MLA_HEADS * ATTN_V, MLA_KV_RANK)), _full((GQA_KV_HEADS * ATTN_V, D)),
                  _full((2, LANE)), tab, tab, tab, tab, tab, tab],
        out_specs=(row(N_Q_HEADS * LANE), row(N_Q_HEADS * LANE),
                   pl.BlockSpec((N_Q_HEADS * ATTN_VA, tm), lambda i: (0, i))),
        compiler_params=_params(("parallel",)),
        name="attn_project",
    )(h, mods, g, w["w_in"], w["q_norm"], w["w_uq"], w["kv_norm"], w["w_uk"], w["w_uvt"], w["w_vbt"],
      w["qk_norm"], *tables)


ATT_TQ = 256
ATT_TK = 512
ATT_LAT_TILES = SEQ // ATT_TQ
ATT_GROUP = 4


def _attn_kernel(q_ref, kc_ref, kl_ref, vtc_ref, vtl_ref, o_ref, sa_ref, sb_ref):
    qi = pl.program_id(2)
    heads = range(ATT_GROUP)
    n_lat = SEQ // ATT_TK

    def scores(s_ref, hd, k_ref, keys, n_keys):
        s = _dot_nt(k_ref[keys, hd * LANE:(hd + 1) * LANE], q_ref[:, hd * LANE:(hd + 1) * LANE])
        s_ref[hd, 0:n_keys, :] = s
        return jnp.max(s, axis=0, keepdims=True)

    def consume(s_ref, hd, vt_ref, keys, n_keys, cmax, carry):
        m, acc = carry
        m_new = jnp.maximum(m, cmax)
        p = jnp.exp2(s_ref[hd, 0:n_keys, :] - m_new).astype(BF16)
        acc = jnp.exp2(m - m_new) * acc + _dot(vt_ref[hd * ATTN_VA:(hd + 1) * ATTN_VA, keys], p)
        return m_new, acc

    def finish(carries):
        o_t = jnp.concatenate([acc[0:ATTN_V, :] / acc[ATTN_V:ATTN_V + 1, :] for _, acc in carries], axis=0)
        o_ref[...] = o_t.T.astype(BF16)

    init = (jnp.full((1, ATT_TQ), -jnp.inf, F32), jnp.zeros((ATTN_VA, ATT_TQ), F32))
    every = slice(None)

    def lat_keys(j):
        return pl.ds(pl.multiple_of(j * ATT_TK, ATT_TK), ATT_TK)

    def step(cur_ref, nxt_ref, j, nxt_k_ref, nxt_keys, nxt_n, cmax, carries):
        nxt, out = [], []
        for hd in heads:
            nxt.append(scores(nxt_ref, hd, nxt_k_ref, nxt_keys, nxt_n))
            out.append(consume(cur_ref, hd, vtl_ref, lat_keys(j), ATT_TK, cmax[hd], carries[hd]))
        return tuple(nxt), tuple(out)

    @pl.when(qi < ATT_LAT_TILES)
    def _():
        cmax0 = tuple(scores(sa_ref, hd, kl_ref, lat_keys(0), ATT_TK) for hd in heads)

        def body(i, state):
            state = step(sa_ref, sb_ref, 2 * i, kl_ref, lat_keys(2 * i + 1), ATT_TK, *state)
            return step(sb_ref, sa_ref, 2 * i + 1, kl_ref, lat_keys(2 * i + 2), ATT_TK, *state)

        state = lax.fori_loop(0, n_lat // 2 - 1, body, (cmax0, (init,) * ATT_GROUP))
        state = step(sa_ref, sb_ref, n_lat - 2, kl_ref, lat_keys(n_lat - 1), ATT_TK, *state)
        cmax_c, carries = step(sb_ref, sa_ref, n_lat - 1, kc_ref, every, CTX, *state)
        finish([consume(sa_ref, hd, vtc_ref, every, CTX, cmax_c[hd], carries[hd]) for hd in heads])

    @pl.when(qi == ATT_LAT_TILES)
    def _():
        cmax_c = [scores(sa_ref, hd, kc_ref, every, CTX) for hd in heads]
        finish([consume(sa_ref, hd, vtc_ref, every, CTX, cmax_c[hd], init) for hd in heads])


def _attention(q, k, vt):
    gw = ATT_GROUP * LANE
    gv = ATT_GROUP * ATTN_V
    gva = ATT_GROUP * ATTN_VA

    def q_row(b, qi):
        return jnp.where(qi < ATT_LAT_TILES, b * ATT_LAT_TILES + qi, N_LAT // ATT_TQ + b)

    return pl.pallas_call(
        _attn_kernel,
        out_shape=jax.ShapeDtypeStruct((N_ROWS, N_Q_HEADS * ATTN_V), BF16),
        grid=(BATCH, N_Q_HEADS // ATT_GROUP, ATT_LAT_TILES + 1),
        in_specs=[pl.BlockSpec((ATT_TQ, gw), lambda b, g, qi: (q_row(b, qi), g)),
                  pl.BlockSpec((CTX, gw), lambda b, g, qi: (N_LAT // CTX + b, g)),
                  pl.BlockSpec((SEQ, gw), lambda b, g, qi: (b, g)),
                  pl.BlockSpec((gva, CTX), lambda b, g, qi: (g, N_LAT // CTX + b)),
                  pl.BlockSpec((gva, SEQ), lambda b, g, qi: (g, b))],
        out_specs=pl.BlockSpec((ATT_TQ, gv), lambda b, g, qi: (q_row(b, qi), g)),
        scratch_shapes=[pltpu.VMEM((ATT_GROUP, ATT_TK, ATT_TQ), F32)] * 2,
        compiler_params=_params(("parallel", "parallel", "arbitrary")),
        name="attention",
    )(q, k, k, vt, vt)


def _mixer_epilogue(y, h, m_ref, g_ref):
    h_new = h + m_ref[2:3, :] * _rms(y, g_ref[1:2, :])
    u = _modulated(h_new, g_ref[2:3, :], m_ref[3:4, :], m_ref[4:5, :])
    return h_new, u


def _attn_out_kernel(o_ref, w_ref, h_ref, m_ref, g_ref, h_out_ref, u_ref):
    y = _dot(o_ref[...], w_ref[...])
    h_new, u = _mixer_epilogue(y, h_ref[...], m_ref, g_ref)
    h_out_ref[...] = h_new
    u_ref[...] = u.astype(BF16)


def _attn_output(o, w_out, h, mods, g):
    tm = ROW_TILE
    row = lambda n: pl.BlockSpec((tm, n), lambda i: (i, 0))
    return pl.pallas_call(
        _attn_out_kernel,
        out_shape=(jax.ShapeDtypeStruct((N_ROWS, D), F32), jax.ShapeDtypeStruct((N_ROWS, D), BF16)),
        grid=(N_ROWS // tm,),
        in_specs=[row(N_Q_HEADS * ATTN_V), _full((N_Q_HEADS * ATTN_V, D)), row(D), _mod_spec(0, tm),
                  _full((4, D))],
        out_specs=(row(D), row(D)),
        compiler_params=_params(("parallel",)),
        name="attn_output",
    )(o, w_out, h, mods, g)


def _swiglu(x, wg_ref, wu_ref, wd_ref, f_dim, chunk):
    y = None
    for c0 in range(0, f_dim, chunk):
        cols = slice(c0, min(c0 + chunk, f_dim))
        a = _silu(_dot(x, wg_ref[:, cols])) * _dot(x, wu_ref[:, cols])
        part = _dot(a.astype(BF16), wd_ref[cols, :])
        y = part if y is None else y + part
    return y


def _ffn_kernel(x_ref, wg_ref, wu_ref, wd_ref, h_ref, m_ref, g_ref, o_ref):
    y = _swiglu(x_ref[...], wg_ref, wu_ref, wd_ref, FFN_DIM, MLP_CHUNK)
    o_ref[...] = h_ref[...] + m_ref[5:6, :] * _rms(y, g_ref[3:4, :])


def _resident(shape):
    return pl.BlockSpec(shape, lambda *_: (0,) * len(shape), pipeline_mode=pl.Buffered(1))


def _ffn(x, wg, wu, wd, h, mods, g):
    tm = MLP_ROW_TILE
    row = lambda n: pl.BlockSpec((tm, n), lambda i: (i, 0))
    return pl.pallas_call(
        _ffn_kernel,
        out_shape=jax.ShapeDtypeStruct((N_ROWS, D), F32),
        grid=(N_ROWS // tm,),
        in_specs=[row(D), _resident((D, FFN_DIM)), _resident((D, FFN_DIM)), _resident((FFN_DIM, D)),
                  row(D), _mod_spec(0, tm), _full((4, D))],
        out_specs=row(D),
        compiler_params=_params(("parallel",)),
        name="ffn_mlp",
    )(x, wg, wu, wd, h, mods, g)


N_ASSIGN = 2 * N_LAT
N_SORTED = N_ASSIGN + N_EXPERTS * MLP_ROW_TILE
N_SORTED_TILES = N_SORTED // MLP_ROW_TILE
ROUTE_TILE = 256
DMA_UNROLL = 8


def _row_copy(src_ref, src_row, dst_ref, dst_row, sem):
    return pltpu.make_async_copy(src_ref.at[pl.ds(src_row, 1), :], dst_ref.at[pl.ds(dst_row, 1), :], sem)


def _dispatch_kernel(d1_ref, d2_ref, u_ref, zeros_ref, xs_ref, sem):
    del zeros_ref

    def start(t, _):
        _row_copy(u_ref, t, xs_ref, d1_ref[0, t], sem).start()
        _row_copy(u_ref, t, xs_ref, d2_ref[0, t], sem).start()
        return 0

    def wait(t, _):
        _row_copy(u_ref, t, xs_ref, d1_ref[0, t], sem).wait()
        _row_copy(u_ref, t, xs_ref, d2_ref[0, t], sem).wait()
        return 0

    lax.fori_loop(0, ROUTE_TILE, start, 0, unroll=DMA_UNROLL)
    lax.fori_loop(0, ROUTE_TILE, wait, 0, unroll=DMA_UNROLL)


def _route_spec():
    return pl.BlockSpec((None, 1, ROUTE_TILE), lambda i: (i, 0, 0), memory_space=pltpu.SMEM)


def _dispatch(d1, d2, u):
    tm = ROUTE_TILE
    return pl.pallas_call(
        _dispatch_kernel,
        out_shape=jax.ShapeDtypeStruct((N_SORTED, D), F32),
        grid=(N_LAT // tm,),
        in_specs=[_route_spec(), _route_spec(), pl.BlockSpec((tm, D), lambda i: (i, 0)),
                  pl.BlockSpec(memory_space=pl.ANY)],
        out_specs=pl.BlockSpec(memory_space=pl.ANY),
        scratch_shapes=[pltpu.SemaphoreType.DMA(())],
        input_output_aliases={3: 0},
        compiler_params=_params(("arbitrary",)),
        name="moe_dispatch",
    )(d1, d2, u, jnp.zeros((N_SORTED, D), F32))


def _experts_kernel(te_ref, nt_ref, x_ref, wg_ref, wu_ref, wd_ref, o_ref, xb_ref):
    del te_ref
    i, f = pl.program_id(0), pl.program_id(1)
    live = i < nt_ref[0]

    @pl.when(live & (f == 0))
    def _():
        xb_ref[...] = x_ref[...].astype(BF16)

    @pl.when(live)
    def _():
        y = _swiglu(xb_ref[...], wg_ref, wu_ref, wd_ref, EXPERT_TILE, MLP_CHUNK)

        @pl.when(f == 0)
        def _():
            o_ref[...] = y

        @pl.when(f > 0)
        def _():
            o_ref[...] += y

    @pl.when(jnp.logical_not(live) & (f == 0))
    def _():
        o_ref[...] = jnp.zeros_like(o_ref)


def _experts(tile_expert, n_tiles, xs, wg, wu, wd):
    tm = MLP_ROW_TILE
    nf = EXPERT_DIM // EXPERT_TILE

    def f_blk(i, f, nt):
        return jnp.where(i < nt[0], f, nf - 1)

    grid_spec = pltpu.PrefetchScalarGridSpec(
        num_scalar_prefetch=2,
        grid=(N_SORTED_TILES, nf),
        in_specs=[pl.BlockSpec((tm, D), lambda i, f, te, nt: (jnp.minimum(i, nt[0] - 1), 0)),
                  pl.BlockSpec((None, D, EXPERT_TILE), lambda i, f, te, nt: (te[i], 0, f_blk(i, f, nt))),
                  pl.BlockSpec((None, D, EXPERT_TILE), lambda i, f, te, nt: (te[i], 0, f_blk(i, f, nt))),
                  pl.BlockSpec((None, EXPERT_TILE, D), lambda i, f, te, nt: (te[i], f_blk(i, f, nt), 0))],
        out_specs=pl.BlockSpec((tm, D), lambda i, f, te, nt: (i, 0)),
        scratch_shapes=[pltpu.VMEM((tm, D), BF16)],
    )
    return pl.pallas_call(
        _experts_kernel,
        out_shape=jax.ShapeDtypeStruct((N_SORTED, D), F32),
        grid_spec=grid_spec,
        compiler_params=_params(("arbitrary", "arbitrary")),
        name="moe_experts",
    )(tile_expert, n_tiles, xs, wg, wu, wd)


def _combine_kernel(d1_ref, d2_ref, n1_ref, n2_ref, w_ref, h_ref, m_ref, g_ref, ys_ref, o_ref, buf_ref, sems):
    i, n = pl.program_id(0), pl.num_programs(0)
    slot = i % 2

    def gather(a_ref, b_ref, slot, start):
        def body(t, _):
            for choice, d_ref in enumerate((a_ref, b_ref)):
                copy = _row_copy(ys_ref, d_ref[0, t], buf_ref.at[slot, choice], t, sems.at[slot])
                copy.start() if start else copy.wait()
            return 0

        lax.fori_loop(0, ROUTE_TILE, body, 0, unroll=DMA_UNROLL)

    @pl.when(i == 0)
    def _():
        gather(d1_ref, d2_ref, 0, True)

    @pl.when(i + 1 < n)
    def _():
        gather(n1_ref, n2_ref, 1 - slot, True)

    gather(d1_ref, d2_ref, slot, False)
    w = w_ref[...]
    lane = lax.broadcasted_iota(jnp.int32, w.shape, 1)
    w1 = jnp.sum(jnp.where(lane == 0, w, 0.0), axis=-1, keepdims=True)
    w2 = jnp.sum(jnp.where(lane == 1, w, 0.0), axis=-1, keepdims=True)
    y = w1 * buf_ref[slot, 0] + w2 * buf_ref[slot, 1]
    o_ref[...] = h_ref[...] + m_ref[5:6, :] * _rms(y, g_ref[3:4, :])


def _combine(d1, d2, wts, h, mods, g, ys):
    tm = ROUTE_TILE
    n = N_LAT // tm
    row = lambda w: pl.BlockSpec((tm, w), lambda i: (i, 0))
    nxt = pl.BlockSpec((None, 1, tm), lambda i: (jnp.minimum(i + 1, n - 1), 0, 0), memory_space=pltpu.SMEM)
    return pl.pallas_call(
        _combine_kernel,
        out_shape=jax.ShapeDtypeStruct((N_LAT, D), F32),
        grid=(n,),
        in_specs=[_route_spec(), _route_spec(), nxt, nxt, row(LANE), row(D), _mod_spec(1, tm), _full((4, D)),
                  pl.BlockSpec(memory_space=pl.ANY)],
        out_specs=row(D),
        scratch_shapes=[pltpu.VMEM((2, 2, tm, D), F32), pltpu.SemaphoreType.DMA((2,))],
        compiler_params=_params(("arbitrary",)),
        name="moe_combine",
    )(d1, d2, d1, d2, wts, h, mods, g, ys)


def _routing(route):
    tm = MLP_ROW_TILE
    e1, e2 = route[:, 0], route[:, 1]
    experts = jnp.arange(N_EXPERTS, dtype=jnp.int32)
    hit = (e1[:, None] == experts).astype(jnp.int32) + (e2[:, None] == experts).astype(jnp.int32)
    pos = jnp.cumsum(hit, axis=0) - hit
    padded = (jnp.sum(hit, axis=0) + tm - 1) // tm * tm
    ends = jnp.cumsum(padded)
    starts = ends - padded
    d1 = starts[e1] + jnp.take_along_axis(pos, e1[:, None], axis=1)[:, 0]
    d2 = starts[e2] + jnp.take_along_axis(pos, e2[:, None], axis=1)[:, 0]
    n_tiles = ends[-1] // tm
    tile_start = jnp.minimum(jnp.arange(N_SORTED_TILES, dtype=jnp.int32), n_tiles - 1) * tm
    tile_expert = jnp.sum((tile_start[:, None] >= ends[None, :]).astype(jnp.int32), axis=1)
    shape = (N_LAT // ROUTE_TILE, 1, ROUTE_TILE)
    return (d1.astype(jnp.int32).reshape(shape), d2.astype(jnp.int32).reshape(shape),
            tile_expert.astype(jnp.int32), n_tiles.astype(jnp.int32).reshape(1))


GLA_Q0, GLA_K0, GLA_V0, GLA_G0, GLA_R0 = 0, 512, 1024, 2048, 3072
GLA_IN_P = GLA_R0 + LANE


def _gla_proj_kernel(h_ref, m_ref, g_ref, w_in_ref, w_vt_ref, w_gate_ref, b_gate_ref,
                     q_ref, k_ref, v_ref, vt_ref, sg_ref, laf_ref, lab_ref):
    u = _modulated(h_ref[...], g_ref[0:1, :], m_ref[0:1, :], m_ref[1:2, :]).astype(BF16)
    z = _dot(u, w_in_ref[...])
    q_ref[...] = (z[:, GLA_Q0:GLA_K0] * (GLA_DK ** -0.5)).astype(BF16)
    k_ref[...] = z[:, GLA_K0:GLA_V0].astype(BF16)
    v_ref[...] = z[:, GLA_V0:GLA_G0].astype(BF16)
    vt_ref[...] = _dot_nt(w_vt_ref[...], u).astype(BF16)
    sg_ref[...] = _silu(z[:, GLA_G0:GLA_R0]).astype(BF16)
    zg = _dot(z[:, GLA_R0:].astype(BF16), w_gate_ref[...]) + b_gate_ref[...]
    la = (jnp.minimum(zg, 0.0) - jnp.log(1.0 + jnp.exp(-jnp.abs(zg)))) * (1.0 / GLA_GATE_NORM)
    laf_ref[...] = la[:, :GLA_HEADS * GLA_DK]
    lab_ref[...] = la[:, GLA_HEADS * GLA_DK:]


def _gla_project(h, mods, g, w):
    tm = ROW_TILE
    row = lambda n, dt=None: pl.BlockSpec((tm, n), lambda i: (i, 0))
    hk, hv = GLA_HEADS * GLA_DK, GLA_HEADS * GLA_DV
    sds = jax.ShapeDtypeStruct
    return pl.pallas_call(
        _gla_proj_kernel,
        out_shape=(sds((N_ROWS, hk), BF16), sds((N_ROWS, hk), BF16), sds((N_ROWS, hv), BF16),
                   sds((hv, N_ROWS), BF16), sds((N_ROWS, hv), BF16),
                   sds((N_ROWS, hk), F32), sds((N_ROWS, hk), F32)),
        grid=(N_ROWS // tm,),
        in_specs=[row(D), _mod_spec(1, tm), _full((4, D)), _full((D, GLA_IN_P)), _full((hv, D)),
                  _full((LANE, 2 * hk)), _full((1, 2 * hk))],
        out_specs=(row(hk), row(hk), row(hv), pl.BlockSpec((hv, tm), lambda i: (0, i)), row(hv),
                   row(hk), row(hk)),
        compiler_params=_params(("parallel",)),
        name="gla_project",
    )(h, mods, g, w["w_in"], w["w_vt"], w["w_gate"], w["b_gate"])


GLA_LAT_BLOCKS = SEQ // GLA_BLOCK
GLA_CTX_BLOCKS = CTX // GLA_BLOCK
GLA_STRIPS = GLA_BLOCK // GLA_SUB


def _gla_block(q, k, v, vt, la, state, reverse):
    n = GLA_BLOCK
    r = lax.broadcasted_iota(jnp.int32, (n, n), 0)
    c = lax.broadcasted_iota(jnp.int32, (n, n), 1)
    keep = (c >= r) if reverse else (c <= r)
    tri = jnp.where(keep, 1.0, 0.0).astype(BF16)
    la_hi, la_lo = _split_bf16(la)
    cum = _dot(tri, la_hi) + _dot(tri, la_lo)
    total = cum[0:1, :] if reverse else cum[n - 1:n, :]
    k_state = (k * jnp.exp(total - cum)).astype(BF16)
    new_state = state * jnp.exp(total) + _dot(vt, k_state)
    if q is None:
        return None, new_state
    o = _dot_nt((q * jnp.exp(cum)).astype(BF16), state.astype(BF16))
    row = lax.broadcasted_iota(jnp.int32, (n, GLA_DK), 0)
    strips = []
    for i in range(GLA_STRIPS):
        lo, hi = i * GLA_SUB, (i + 1) * GLA_SUB
        if reverse:
            ref = cum[hi:hi + 1, :] if i < GLA_STRIPS - 1 else jnp.zeros((1, GLA_DK), F32)
            live = row >= lo
        else:
            ref = cum[lo - 1:lo, :] if i > 0 else jnp.zeros((1, GLA_DK), F32)
            live = row < hi
        q_loc = (q[lo:hi, :] * jnp.exp(cum[lo:hi, :] - ref)).astype(BF16)
        k_loc = jnp.where(live, k * jnp.exp(ref - cum), 0.0).astype(BF16)
        strips.append(_dot_nt(q_loc, k_loc))
    scores = jnp.where(keep, jnp.concatenate(strips, axis=0), 0.0).astype(BF16)
    return o + _dot(scores, v), new_state


def _gla_kernel(q_ref, kl_ref, kc_ref, vl_ref, vc_ref, vtl_ref, vtc_ref,
                lafl_ref, lafc_ref, labl_ref, labc_ref, o_ref, sf_ref, sb_ref):
    n = GLA_BLOCK
    sf_ref[...] = jnp.zeros_like(sf_ref)
    sb_ref[...] = jnp.zeros_like(sb_ref)

    for j in range(GLA_CTX_BLOCKS):
        fs = slice(j * n, (j + 1) * n)
        bs = slice((GLA_CTX_BLOCKS - 1 - j) * n, (GLA_CTX_BLOCKS - j) * n)
        _, sf = _gla_block(None, kc_ref[fs, :].astype(F32), None, vtc_ref[:, fs], lafc_ref[fs, :],
                           sf_ref[...], False)
        sf_ref[...] = sf
        _, sb = _gla_block(None, kc_ref[bs, :].astype(F32), None, vtc_ref[:, bs], labc_ref[bs, :],
                           sb_ref[...], True)
        sb_ref[...] = sb

    def step(j, accumulate):
        for reverse, la_ref, s_ref in ((False, lafl_ref, sf_ref), (True, labl_ref, sb_ref)):
            blk = (GLA_LAT_BLOCKS - 1 - j) if reverse else j
            off = pl.multiple_of(blk * n, n)
            rows = pl.ds(off, n)
            o, s_new = _gla_block(q_ref[rows, :].astype(F32), kl_ref[rows, :].astype(F32), vl_ref[rows, :],
                                  vtl_ref[:, rows], la_ref[rows, :], s_ref[...], reverse)
            s_ref[...] = s_new
            if accumulate:
                o_ref[rows, :] += o
            else:
                o_ref[rows, :] = o

    half = GLA_LAT_BLOCKS // 2

    def first(j, _):
        step(j, False)
        return 0

    def second(j, _):
        step(j, True)
        return 0

    lax.fori_loop(0, half, first, 0)
    lax.fori_loop(half, GLA_LAT_BLOCKS, second, 0)


def _gla_scan(q, k, v, vt, laf, lab):
    lat = lambda n: pl.BlockSpec((SEQ, n), lambda b, hd: (b, hd))
    ctx = lambda n: pl.BlockSpec((CTX, n), lambda b, hd: (N_LAT // CTX + b, hd))
    return pl.pallas_call(
        _gla_kernel,
        out_shape=jax.ShapeDtypeStruct((N_LAT, GLA_HEADS * GLA_DV), F32),
        grid=(BATCH, GLA_HEADS),
        in_specs=[lat(GLA_DK), lat(GLA_DK), ctx(GLA_DK), lat(GLA_DV), ctx(GLA_DV),
                  pl.BlockSpec((GLA_DV, SEQ), lambda b, hd: (hd, b)),
                  pl.BlockSpec((GLA_DV, CTX), lambda b, hd: (hd, N_LAT // CTX + b)),
                  lat(GLA_DK), ctx(GLA_DK), lat(GLA_DK), ctx(GLA_DK)],
        out_specs=lat(GLA_DV),
        scratch_shapes=[pltpu.VMEM((GLA_DV, GLA_DK), F32), pltpu.VMEM((GLA_DV, GLA_DK), F32)],
        compiler_params=_params(("parallel", "parallel")),
        name="gla_scan",
    )(q, k, k, v, v, vt, vt, laf, laf, lab, lab)


def _gla_out_kernel(o_ref, sg_ref, on_ref, w_ref, h_ref, m_ref, g_ref, wr_hi_ref, wr_lo_ref, br_ref,
                    h_out_ref, u_ref, route_ref, wts_ref):
    parts = []
    for hd in range(GLA_HEADS):
        sl = slice(hd * GLA_DV, (hd + 1) * GLA_DV)
        parts.append(_rms(o_ref[:, sl], on_ref[...]) * sg_ref[:, sl].astype(F32))
    y = _dot(jnp.concatenate(parts, axis=-1).astype(BF16), w_ref[...])
    h_new, u = _mixer_epilogue(y, h_ref[...], m_ref, g_ref)
    h_out_ref[...] = h_new
    u_ref[...] = u
    logits = _dot3(u, wr_hi_ref[...], wr_lo_ref[...]) + br_ref[...]
    lane = lax.broadcasted_iota(jnp.int32, logits.shape, 1)
    l1 = jnp.max(logits, axis=-1, keepdims=True)
    i1 = jnp.min(jnp.where(logits == l1, lane, LANE), axis=-1, keepdims=True)
    rest = jnp.where(lane == i1, -jnp.inf, logits)
    l2 = jnp.max(rest, axis=-1, keepdims=True)
    i2 = jnp.min(jnp.where(rest == l2, lane, LANE), axis=-1, keepdims=True)
    e2 = jnp.exp(l2 - l1)
    route_ref[...] = jnp.where(lane == 0, i1, jnp.where(lane == 1, i2, 0))
    wts_ref[...] = jnp.where(lane == 0, 1.0 / (1.0 + e2), jnp.where(lane == 1, e2 / (1.0 + e2), 0.0))


def _gla_output(o, sg, o_norm, w_out, h, mods, g, wr_hi, wr_lo, br):
    tm = ROW_TILE
    row = lambda n: pl.BlockSpec((tm, n), lambda i: (i, 0))
    hv = GLA_HEADS * GLA_DV
    return pl.pallas_call(
        _gla_out_kernel,
        out_shape=(jax.ShapeDtypeStruct((N_LAT, D), F32), jax.ShapeDtypeStruct((N_LAT, D), F32),
                   jax.ShapeDtypeStruct((N_LAT, LANE), jnp.int32), jax.ShapeDtypeStruct((N_LAT, LANE), F32)),
        grid=(N_LAT // tm,),
        in_specs=[row(hv), row(hv), _full((1, GLA_DV)), _full((hv, D)), row(D), _mod_spec(1, tm),
                  _full((4, D)), _full((D, LANE)), _full((D, LANE)), _full((1, LANE))],
        out_specs=(row(D), row(D), row(LANE), row(LANE)),
        compiler_params=_params(("parallel",)),
        name="gla_output",
    )(o, sg, o_norm, w_out, h, mods, g, wr_hi, wr_lo, br)


def _slots(w, n_heads, width):
    k = w.shape[0]
    w = w.reshape(k, n_heads, width)
    return jnp.pad(w, ((0, 0), (0, 0), (0, LANE - width))).reshape(k, n_heads * LANE)


def _attn_weights(w_in, q_norm, w_uq, kv_norm, w_ukv, qk_norm, w_out):
    c = 0
    cols = {}
    for name, n in (("q_lat", MLA_Q_RANK), ("kv_lat", MLA_KV_RANK), ("k_rope", MLA_ROPE),
                    ("q_b", GQA_HEADS * GQA_HEAD_DIM), ("k_b", GQA_KV_HEADS * GQA_HEAD_DIM),
                    ("v_b", GQA_KV_HEADS * GQA_HEAD_DIM)):
        cols[name] = w_in[:, c:c + n]
        c += n
    k_rope = jnp.pad(cols["k_rope"], ((0, 0), (MLA_NOPE, LANE - MLA_NOPE - MLA_ROPE)))
    w_in_p = jnp.concatenate([cols["q_lat"], cols["kv_lat"], k_rope,
                              _slots(cols["q_b"], GQA_HEADS, GQA_HEAD_DIM),
                              _slots(cols["k_b"], GQA_KV_HEADS, GQA_HEAD_DIM)], axis=1)
    ukv = w_ukv.reshape(MLA_KV_RANK, MLA_HEADS, MLA_NOPE + MLA_V)
    w_uk = _slots(ukv[:, :, :MLA_NOPE].reshape(MLA_KV_RANK, -1), MLA_HEADS, MLA_NOPE)
    w_uv = ukv[:, :, MLA_NOPE:].reshape(MLA_KV_RANK, MLA_HEADS * MLA_V)
    return {
        "w_in": w_in_p.astype(BF16),
        "q_norm": q_norm.reshape(1, -1),
        "w_uq": _slots(w_uq, MLA_HEADS, MLA_NOPE + MLA_ROPE).astype(BF16),
        "kv_norm": kv_norm.reshape(1, -1),
        "w_uk": w_uk.astype(BF16),
        "w_uvt": w_uv.T.astype(BF16),
        "w_vbt": cols["v_b"].T.astype(BF16),
        "qk_norm": jnp.pad(qk_norm, ((0, 0), (0, LANE - GQA_HEAD_DIM))),
        "w_out": w_out.astype(BF16),
    }


def _axial_tables(d):
    half, p = d // 2, d // 4
    t = jnp.arange(SEQ, dtype=jnp.int32)
    freqs = ROPE_THETA ** (-jnp.arange(0, half, 2, dtype=F32) / half)
    cos, sa, sb = [], [], []
    for pos in (t // GRID_W, t % GRID_W):
        ang = pos.astype(F32)[:, None] * freqs[None, :]
        c, s, z = jnp.cos(ang), jnp.sin(ang), jnp.zeros_like(ang)
        cos += [c, c]
        sa += [-s, z]
        sb += [z, s]
    return [jnp.concatenate(x, axis=-1) for x in (cos, sa, sb)]


def _rope_tables():
    def embed(tbl, left, fill):
        right = LANE - left - tbl.shape[1]
        tbl = jnp.pad(tbl, ((0, 0), (left, right)), constant_values=fill)
        return jnp.pad(tbl, ((0, ROW_TILE), (0, 0)), constant_values=fill)

    out = []
    for d, left in ((MLA_ROPE, MLA_NOPE), (GQA_HEAD_DIM, 0)):
        cos, sa, sb = _axial_tables(d)
        out += [embed(cos, left, 1.0), embed(sa, left, 0.0), embed(sb, left, 0.0)]
    return out


def _gla_weights(w_in, w_gate2, b_gate):
    hk, hv = GLA_HEADS * GLA_DK, GLA_HEADS * GLA_DV
    r = jnp.pad(w_in[:, 2 * hk + 2 * hv:], ((0, 0), (0, LANE - 2 * GLA_GATE_RANK)))
    w_in_p = jnp.concatenate([w_in[:, :2 * hk + 2 * hv], r], axis=1)
    w_gate = jnp.zeros((LANE, 2 * hk), F32)
    w_gate = w_gate.at[:GLA_GATE_RANK, :hk].set(w_gate2[0])
    w_gate = w_gate.at[GLA_GATE_RANK:2 * GLA_GATE_RANK, hk:].set(w_gate2[1])
    return {
        "w_in": w_in_p.astype(BF16),
        "w_vt": w_in[:, 2 * hk:2 * hk + hv].T.astype(BF16),
        "w_gate": w_gate.astype(BF16),
        "b_gate": b_gate.reshape(1, 2 * hk),
    }


def kernel(x, c, ctx, c_ctx, mod_w, mod_b, norm_g, attn_w_in, attn_q_norm, attn_w_uq, attn_kv_norm, attn_w_ukv,
           attn_qk_norm, attn_w_out, gla_w_in, gla_w_gate2, gla_b_gate, gla_o_norm, gla_w_out, ffn_w_gate,
           ffn_w_up, ffn_w_down, moe_w_router, moe_b_router, moe_w_gate, moe_w_up, moe_w_down):
    assert x.shape == (BATCH, SEQ, D) and ctx.shape == (BATCH, CTX, D)
    h = jnp.concatenate([x.reshape(N_LAT, D), ctx.reshape(N_CTX, D)], axis=0)
    cc = jnp.concatenate([c, c_ctx[None, :], jnp.zeros((MOD_ROWS - BATCH - 1, D), F32)], axis=0)
    mods = _mod_vectors(cc, mod_w, mod_b).reshape(mod_w.shape[0], MOD_ROWS, 6, D)

    aw = _attn_weights(attn_w_in[0], attn_q_norm[0], attn_w_uq[0], attn_kv_norm[0], attn_w_ukv[0],
                       attn_qk_norm[0], attn_w_out[0])
    q, k, vt = _attn_project(h, mods, norm_g[0], aw, _rope_tables())
    o = _attention(q, k, vt)
    h, u = _attn_output(o, aw["w_out"], h, mods, norm_g[0])
    h = _ffn(u, ffn_w_gate[0].astype(BF16), ffn_w_up[0].astype(BF16), ffn_w_down[0].astype(BF16), h, mods,
             norm_g[0])

    gw = _gla_weights(gla_w_in[0], gla_w_gate2[0], gla_b_gate[0])
    gq, gk, gv, gvt, sg, laf, lab = _gla_project(h, mods, norm_g[1], gw)
    go = _gla_scan(gq, gk, gv, gvt, laf, lab)
    wr = jnp.pad(moe_w_router[0], ((0, 0), (0, LANE - N_EXPERTS)))
    wr_hi = wr.astype(BF16)
    wr_lo = (wr - wr_hi.astype(F32)).astype(BF16)
    br = jnp.pad(moe_b_router[0], (0, LANE - N_EXPERTS), constant_values=-jnp.inf).reshape(1, LANE)
    h, u, route, wts = _gla_output(go, sg, gla_o_norm[0].reshape(1, GLA_DV), gla_w_out[0].astype(BF16), h,
                                   mods, norm_g[1], wr_hi, wr_lo, br)
    d1, d2, tile_expert, n_tiles = _routing(route)
    xs = _dispatch(d1, d2, u)
    ys = _experts(tile_expert, n_tiles, xs, moe_w_gate[0].astype(BF16), moe_w_up[0].astype(BF16),
                  moe_w_down[0].astype(BF16))
    h = _combine(d1, d2, wts, h, mods, norm_g[1], ys)
    return h.reshape(BATCH, SEQ, D)
```

```python
import jax
import jax.numpy as jnp
from jax import lax
from jax.experimental import pallas as pl
from jax.experimental.pallas import tpu as pltpu

F32 = jnp.float32
BF16 = jnp.bfloat16

D = 1024
BATCH = 4
SEQ = 4096
CTX = 256
GRID_W = 64
ROPE_THETA = 10000.0
EPS = 1e-6

N_LAT = BATCH * SEQ
N_CTX = BATCH * CTX
N_ROWS = N_LAT + N_CTX
MOD_ROWS = 8
CTX_MOD_ROW = BATCH

LANE = 128
ROW_TILE = 512
ATTN_PROJ_TILE = 256

MLA_HEADS = 8
MLA_Q_RANK = 384
MLA_KV_RANK = 256
MLA_NOPE = 64
MLA_ROPE = 32
MLA_V = 64
GQA_HEADS = 8
GQA_KV_HEADS = 2
GQA_GROUP = GQA_HEADS // GQA_KV_HEADS
GQA_HEAD_DIM = 64
N_Q_HEADS = MLA_HEADS + GQA_HEADS

GLA_HEADS = 4
GLA_DK = 128
GLA_DV = 256
GLA_GATE_RANK = 16
GLA_GATE_NORM = 16.0
GLA_BLOCK = 128
GLA_SUB = 32

MXU_WIDTH = 256
FFN_DIM = 2816
N_EXPERTS = 8
EXPERT_DIM = 3584
EXPERT_TILE = EXPERT_DIM // 2
MLP_CHUNK = 2 * MXU_WIDTH
MLP_ROW_TILE = 512

VMEM_LIMIT = 56 * 1024 * 1024


def _params(sem):
    return pltpu.CompilerParams(dimension_semantics=sem, vmem_limit_bytes=VMEM_LIMIT)


def _rms(x, g):
    return x * lax.rsqrt(jnp.mean(x * x, axis=-1, keepdims=True) + EPS) * g


def _silu(x):
    return x / (1.0 + jnp.exp(-x))


def _split_bf16(x):
    hi = x.astype(BF16)
    lo = (x - hi.astype(F32)).astype(BF16)
    return hi, lo


def _dot(a, b):
    return jnp.dot(a, b, preferred_element_type=F32)


def _dot_nt(a, b):
    return lax.dot_general(a, b, (((1,), (1,)), ((), ())), preferred_element_type=F32)


def _dot3(a, b_hi, b_lo):
    a_hi, a_lo = _split_bf16(a)
    return _dot(a_hi, b_hi) + (_dot(a_hi, b_lo) + _dot(a_lo, b_hi))


def _modulated(h, g_row, shift, scale):
    return _rms(h, g_row) * (1.0 + scale) + shift


def _rope(x, cos, sa, sb, p):
    return x * cos + pltpu.roll(x, LANE - p, 1) * sa + pltpu.roll(x, p, 1) * sb


def _mod_row(i, tile):
    r0 = i * tile
    return jnp.where(r0 < N_LAT, r0 // SEQ, CTX_MOD_ROW)


def _mod_spec(layer, tile):
    return pl.BlockSpec((None, None, 6, D), lambda i, *_: (layer, _mod_row(i, tile), 0, 0))


def _rope_block(i):
    r0 = i * ATTN_PROJ_TILE
    return jnp.where(r0 < N_LAT, (r0 % SEQ) // ATTN_PROJ_TILE, SEQ // ATTN_PROJ_TILE)


def _full(shape):
    return pl.BlockSpec(shape, lambda *_: (0,) * len(shape))


def _mod_kernel(c_ref, w_ref, b_ref, o_ref):
    w_hi, w_lo = _split_bf16(w_ref[...])
    o_ref[...] = _dot3(_silu(c_ref[...]), w_hi, w_lo) + b_ref[...]


def _mod_vectors(cc, mod_w, mod_b):
    depth, _, n = mod_w.shape
    tn = 1536
    return pl.pallas_call(
        _mod_kernel,
        out_shape=jax.ShapeDtypeStruct((depth, MOD_ROWS, n), F32),
        grid=(depth, n // tn),
        in_specs=[
            pl.BlockSpec((MOD_ROWS, D), lambda l, j: (0, 0)),
            pl.BlockSpec((None, D, tn), lambda l, j: (l, 0, j)),
            pl.BlockSpec((None, 1, tn), lambda l, j: (l, 0, j)),
        ],
        out_specs=pl.BlockSpec((None, MOD_ROWS, tn), lambda l, j: (l, 0, j)),
        compiler_params=_params(("parallel", "parallel")),
        name="mod_vectors",
    )(cc, mod_w, mod_b.reshape(depth, 1, n))


Q_LAT0, KV_LAT0, K_ROPE0 = 0, 384, 640
Q_B0 = 768
K_B0 = Q_B0 + GQA_HEADS * LANE
ATTN_IN_P = K_B0 + GQA_KV_HEADS * LANE
ATTN_V = 64
ATTN_VA = ATTN_V + 16
LOG2E = 1.4426950408889634


def _input_rows(x_ref, c_ref):
    return jnp.where(pl.program_id(0) < N_LAT // x_ref.shape[0], x_ref[...], c_ref[...])


def _input_specs(tm):
    n_lat = N_LAT // tm
    return [pl.BlockSpec((tm, D), lambda i: (jnp.minimum(i, n_lat - 1), 0)),
            pl.BlockSpec((tm, D), lambda i: (jnp.maximum(i - n_lat, 0), 0))]


def _attn_proj_kernel(x_ref, c_ref, m_ref, g_ref, w_in_ref, qn_ref, w_uq_ref, kvn_ref, w_uk_ref, w_uvt_ref,
                      w_vbt_ref, qkn_ref, ca_ref, saa_ref, sba_ref, cb_ref, sab_ref, sbb_ref,
                      q_ref, k_ref, vt_ref):
    u = _modulated(_input_rows(x_ref, c_ref), g_ref[0:1, :], m_ref[0:1, :], m_ref[1:2, :]).astype(BF16)
    z = _dot(u, w_in_ref[...])
    q_lat = _rms(z[:, Q_LAT0:Q_LAT0 + MLA_Q_RANK], qn_ref[...]).astype(BF16)
    kv_lat = _rms(z[:, KV_LAT0:KV_LAT0 + MLA_KV_RANK], kvn_ref[...]).astype(BF16)
    q_a = _dot(q_lat, w_uq_ref[...])
    k_a = _dot(kv_lat, w_uk_ref[...])
    ca, saa, sba = ca_ref[...], saa_ref[...], sba_ref[...]
    cb, sab, sbb = cb_ref[...], sab_ref[...], sbb_ref[...]
    pa, pb = MLA_ROPE // 4, GQA_HEAD_DIM // 4
    scale_a = (MLA_NOPE + MLA_ROPE) ** -0.5 * LOG2E
    scale_b = GQA_HEAD_DIM ** -0.5 * LOG2E
    k_rope = _rope(z[:, K_ROPE0:K_ROPE0 + LANE], ca, saa, sba, pa)
    for hd in range(MLA_HEADS):
        sl = slice(hd * LANE, (hd + 1) * LANE)
        q_ref[:, sl] = (_rope(q_a[:, sl], ca, saa, sba, pa) * scale_a).astype(BF16)
        k_ref[:, sl] = (k_a[:, sl] + k_rope).astype(BF16)

    def head_norm(x, gain):
        ms = jnp.sum(x * x, axis=-1, keepdims=True) * (1.0 / GQA_HEAD_DIM)
        return x * lax.rsqrt(ms + EPS) * gain

    for hd in range(GQA_HEADS):
        x = head_norm(z[:, Q_B0 + hd * LANE:Q_B0 + (hd + 1) * LANE], qkn_ref[0:1, :])
        sl = slice((MLA_HEADS + hd) * LANE, (MLA_HEADS + hd + 1) * LANE)
        q_ref[:, sl] = (_rope(x, cb, sab, sbb, pb) * scale_b).astype(BF16)
    for kv in range(GQA_KV_HEADS):
        x = head_norm(z[:, K_B0 + kv * LANE:K_B0 + (kv + 1) * LANE], qkn_ref[1:2, :])
        x = _rope(x, cb, sab, sbb, pb).astype(BF16)
        for hd in range(kv * GQA_GROUP, (kv + 1) * GQA_GROUP):
            k_ref[:, (MLA_HEADS + hd) * LANE:(MLA_HEADS + hd + 1) * LANE] = x

    vat = _dot_nt(w_uvt_ref[...], kv_lat).astype(BF16)
    vbt = _dot_nt(w_vbt_ref[...], u).astype(BF16)
    ones = jnp.ones((ATTN_VA - ATTN_V, vat.shape[1]), BF16)
    for hd in range(N_Q_HEADS):
        if hd < MLA_HEADS:
            v_t = vat[hd * ATTN_V:(hd + 1) * ATTN_V, :]
        else:
            kv = (hd - MLA_HEADS) // GQA_GROUP
            v_t = vbt[kv * ATTN_V:(kv + 1) * ATTN_V, :]
        vt_ref[hd * ATTN_VA:hd * ATTN_VA + ATTN_V, :] = v_t
        vt_ref[hd * ATTN_VA + ATTN_V:(hd + 1) * ATTN_VA, :] = ones


def _attn_project(x, ctx, mods, g, w, tables):
    tm = ATTN_PROJ_TILE
    row = lambda n: pl.BlockSpec((tm, n), lambda i: (i, 0))
    tab = pl.BlockSpec((tm, LANE), lambda i: (_rope_block(i), 0))
    return pl.pallas_call(
        _attn_proj_kernel,
        out_shape=(jax.ShapeDtypeStruct((N_ROWS, N_Q_HEADS * LANE), BF16),
                   jax.ShapeDtypeStruct((N_ROWS, N_Q_HEADS * LANE), BF16),
                   jax.ShapeDtypeStruct((N_Q_HEADS * ATTN_VA, N_ROWS), BF16)),
        grid=(N_ROWS // tm,),
        in_specs=[*_input_specs(tm), _mod_spec(0, tm), _full((4, D)), _full((D, ATTN_IN_P)),
                  _full((1, MLA_Q_RANK)), _full((MLA_Q_RANK, MLA_HEADS * LANE)),
                  _full((1, MLA_KV_RANK)), _full((MLA_KV_RANK, MLA_HEADS * LANE)),
                  _full((MLA_HEADS * ATTN_V, MLA_KV_RANK)), _full((GQA_KV_HEADS * ATTN_V, D)),
                  _full((2, LANE)), tab, tab, tab, tab, tab, tab],
        out_specs=(row(N_Q_HEADS * LANE), row(N_Q_HEADS * LANE),
                   pl.BlockSpec((N_Q_HEADS * ATTN_VA, tm), lambda i: (0, i))),
        compiler_params=_params(("parallel",)),
        name="attn_project",
    )(x, ctx, mods, g, w["w_in"], w["q_norm"], w["w_uq"], w["kv_norm"], w["w_uk"], w["w_uvt"], w["w_vbt"],
      w["qk_norm"], *tables)


ATT_TQ = 256
ATT_TK = 1024
ATT_LAT_TILES = SEQ // ATT_TQ
ATT_GROUP = 4


def _attn_kernel(q_ref, kc_ref, kl_ref, vtc_ref, vtl_ref, o_ref, sa_ref, sb_ref):
    qi = pl.program_id(2)
    heads = range(ATT_GROUP)
    n_lat = SEQ // ATT_TK

    def scores(s_ref, hd, k_ref, keys, n_keys):
        s = _dot_nt(k_ref[keys, hd * LANE:(hd + 1) * LANE], q_ref[:, hd * LANE:(hd + 1) * LANE])
        s_ref[hd, 0:n_keys, :] = s
        return jnp.max(s, axis=0, keepdims=True)

    def consume(s_ref, hd, vt_ref, keys, n_keys, cmax, carry):
        m, acc = carry
        m_new = jnp.maximum(m, cmax)
        p = jnp.exp2(s_ref[hd, 0:n_keys, :] - m_new).astype(BF16)
        acc = jnp.exp2(m - m_new) * acc + _dot(vt_ref[hd * ATTN_VA:(hd + 1) * ATTN_VA, keys], p)
        return m_new, acc

    def finish(carries):
        o_t = jnp.concatenate([acc[0:ATTN_V, :] / acc[ATTN_V:ATTN_V + 1, :] for _, acc in carries], axis=0)
        o_ref[...] = o_t.T.astype(BF16)

    init = (jnp.full((1, ATT_TQ), -jnp.inf, F32), jnp.zeros((ATTN_VA, ATT_TQ), F32))
    every = slice(None)

    def lat_keys(j):
        return pl.ds(pl.multiple_of(j * ATT_TK, ATT_TK), ATT_TK)

    def step(cur_ref, nxt_ref, j, nxt_k_ref, nxt_keys, nxt_n, cmax, carries):
        nxt, out = [], []
        for hd in heads:
            nxt.append(scores(nxt_ref, hd, nxt_k_ref, nxt_keys, nxt_n))
            out.append(consume(cur_ref, hd, vtl_ref, lat_keys(j), ATT_TK, cmax[hd], carries[hd]))
        return tuple(nxt), tuple(out)

    @pl.when(qi < ATT_LAT_TILES)
    def _():
        cmax0 = tuple(scores(sa_ref, hd, kl_ref, lat_keys(0), ATT_TK) for hd in heads)

        def body(i, state):
            state = step(sa_ref, sb_ref, 2 * i, kl_ref, lat_keys(2 * i + 1), ATT_TK, *state)
            return step(sb_ref, sa_ref, 2 * i + 1, kl_ref, lat_keys(2 * i + 2), ATT_TK, *state)

        state = lax.fori_loop(0, n_lat // 2 - 1, body, (cmax0, (init,) * ATT_GROUP))
        state = step(sa_ref, sb_ref, n_lat - 2, kl_ref, lat_keys(n_lat - 1), ATT_TK, *state)
        cmax_c, carries = step(sb_ref, sa_ref, n_lat - 1, kc_ref, every, CTX, *state)
        finish([consume(sa_ref, hd, vtc_ref, every, CTX, cmax_c[hd], carries[hd]) for hd in heads])

    @pl.when(qi == ATT_LAT_TILES)
    def _():
        cmax_c = [scores(sa_ref, hd, kc_ref, every, CTX) for hd in heads]
        finish([consume(sa_ref, hd, vtc_ref, every, CTX, cmax_c[hd], init) for hd in heads])


def _attention(q, k, vt):
    gw = ATT_GROUP * LANE
    gv = ATT_GROUP * ATTN_V
    gva = ATT_GROUP * ATTN_VA

    def q_row(b, qi):
        return jnp.where(qi < ATT_LAT_TILES, b * ATT_LAT_TILES + qi, N_LAT // ATT_TQ + b)

    return pl.pallas_call(
        _attn_kernel,
        out_shape=jax.ShapeDtypeStruct((N_ROWS, N_Q_HEADS * ATTN_V), BF16),
        grid=(BATCH, N_Q_HEADS // ATT_GROUP, ATT_LAT_TILES + 1),
        in_specs=[pl.BlockSpec((ATT_TQ, gw), lambda b, g, qi: (q_row(b, qi), g)),
                  pl.BlockSpec((CTX, gw), lambda b, g, qi: (N_LAT // CTX + b, g)),
                  pl.BlockSpec((SEQ, gw), lambda b, g, qi: (b, g)),
                  pl.BlockSpec((gva, CTX), lambda b, g, qi: (g, N_LAT // CTX + b)),
                  pl.BlockSpec((gva, SEQ), lambda b, g, qi: (g, b))],
        out_specs=pl.BlockSpec((ATT_TQ, gv), lambda b, g, qi: (q_row(b, qi), g)),
        scratch_shapes=[pltpu.VMEM((ATT_GROUP, ATT_TK, ATT_TQ), F32)] * 2,
        compiler_params=_params(("parallel", "parallel", "arbitrary")),
        name="attention",
    )(q, k, k, vt, vt)


def _mixer_epilogue(y, h, m_ref, g_ref):
    h_new = h + m_ref[2:3, :] * _rms(y, g_ref[1:2, :])
    u = _modulated(h_new, g_ref[2:3, :], m_ref[3:4, :], m_ref[4:5, :])
    return h_new, u


def _attn_out_kernel(o_ref, w_ref, x_ref, c_ref, m_ref, g_ref, h_out_ref, u_ref):
    y = _dot(o_ref[...], w_ref[...])
    h_new, u = _mixer_epilogue(y, _input_rows(x_ref, c_ref), m_ref, g_ref)
    h_out_ref[...] = h_new
    u_ref[...] = u.astype(BF16)


def _attn_output(o, w_out, x, ctx, mods, g):
    tm = ROW_TILE
    row = lambda n: pl.BlockSpec((tm, n), lambda i: (i, 0))
    return pl.pallas_call(
        _attn_out_kernel,
        out_shape=(jax.ShapeDtypeStruct((N_ROWS, D), F32), jax.ShapeDtypeStruct((N_ROWS, D), BF16)),
        grid=(N_ROWS // tm,),
        in_specs=[row(N_Q_HEADS * ATTN_V), _full((N_Q_HEADS * ATTN_V, D)), *_input_specs(tm), _mod_spec(0, tm),
                  _full((4, D))],
        out_specs=(row(D), row(D)),
        compiler_params=_params(("parallel",)),
        name="attn_output",
    )(o, w_out, x, ctx, mods, g)


def _swiglu(x, wg_ref, wu_ref, wd_ref, f_dim, chunk):
    y = None
    for c0 in range(0, f_dim, chunk):
        cols = slice(c0, min(c0 + chunk, f_dim))
        a = _silu(_dot(x, wg_ref[:, cols])) * _dot(x, wu_ref[:, cols])
        part = _dot(a.astype(BF16), wd_ref[cols, :])
        y = part if y is None else y + part
    return y


def _ffn_kernel(x_ref, wg_ref, wu_ref, wd_ref, h_ref, m_ref, g_ref, o_ref):
    y = _swiglu(x_ref[...], wg_ref, wu_ref, wd_ref, FFN_DIM, MLP_CHUNK)
    o_ref[...] = h_ref[...] + m_ref[5:6, :] * _rms(y, g_ref[3:4, :])


def _resident(shape):
    return pl.BlockSpec(shape, lambda *_: (0,) * len(shape), pipeline_mode=pl.Buffered(1))


def _ffn(x, wg, wu, wd, h, mods, g):
    tm = MLP_ROW_TILE
    row = lambda n: pl.BlockSpec((tm, n), lambda i: (i, 0))
    return pl.pallas_call(
        _ffn_kernel,
        out_shape=jax.ShapeDtypeStruct((N_ROWS, D), F32),
        grid=(N_ROWS // tm,),
        in_specs=[row(D), _resident((D, FFN_DIM)), _resident((D, FFN_DIM)), _resident((FFN_DIM, D)),
                  row(D), _mod_spec(0, tm), _full((4, D))],
        out_specs=row(D),
        compiler_params=_params(("parallel",)),
        name="ffn_mlp",
    )(x, wg, wu, wd, h, mods, g)


N_ASSIGN = 2 * N_LAT
N_SORTED = N_ASSIGN + N_EXPERTS * MLP_ROW_TILE
N_SORTED_TILES = N_SORTED // MLP_ROW_TILE
ROUTE_TILE = 256
DMA_UNROLL = 8


def _row_copy(src_ref, src_row, dst_ref, dst_row, sem):
    return pltpu.make_async_copy(src_ref.at[pl.ds(src_row, 1), :], dst_ref.at[pl.ds(dst_row, 1), :], sem)


def _dispatch_kernel(d1_ref, d2_ref, u_ref, zeros_ref, xs_ref, sem):
    del zeros_ref

    def start(t, _):
        _row_copy(u_ref, t, xs_ref, d1_ref[0, t], sem).start()
        _row_copy(u_ref, t, xs_ref, d2_ref[0, t], sem).start()
        return 0

    def wait(t, _):
        _row_copy(u_ref, t, xs_ref, d1_ref[0, t], sem).wait()
        _row_copy(u_ref, t, xs_ref, d2_ref[0, t], sem).wait()
        return 0

    lax.fori_loop(0, ROUTE_TILE, start, 0, unroll=DMA_UNROLL)
    lax.fori_loop(0, ROUTE_TILE, wait, 0, unroll=DMA_UNROLL)


def _route_spec():
    return pl.BlockSpec((None, 1, ROUTE_TILE), lambda i: (i, 0, 0), memory_space=pltpu.SMEM)


def _dispatch(d1, d2, u):
    tm = ROUTE_TILE
    return pl.pallas_call(
        _dispatch_kernel,
        out_shape=jax.ShapeDtypeStruct((N_SORTED, D), F32),
        grid=(N_LAT // tm,),
        in_specs=[_route_spec(), _route_spec(), pl.BlockSpec((tm, D), lambda i: (i, 0)),
                  pl.BlockSpec(memory_space=pl.ANY)],
        out_specs=pl.BlockSpec(memory_space=pl.ANY),
        scratch_shapes=[pltpu.SemaphoreType.DMA(())],
        input_output_aliases={3: 0},
        compiler_params=_params(("arbitrary",)),
        name="moe_dispatch",
    )(d1, d2, u, jnp.zeros((N_SORTED, D), F32))


def _experts_kernel(te_ref, nt_ref, x_ref, wg_ref, wu_ref, wd_ref, o_ref, xb_ref):
    del te_ref
    i, f = pl.program_id(0), pl.program_id(1)
    live = i < nt_ref[0]

    @pl.when(live & (f == 0))
    def _():
        xb_ref[...] = x_ref[...].astype(BF16)

    @pl.when(live)
    def _():
        y = _swiglu(xb_ref[...], wg_ref, wu_ref, wd_ref, EXPERT_TILE, MLP_CHUNK)

        @pl.when(f == 0)
        def _():
            o_ref[...] = y

        @pl.when(f > 0)
        def _():
            o_ref[...] += y

    @pl.when(jnp.logical_not(live) & (f == 0))
    def _():
        o_ref[...] = jnp.zeros_like(o_ref)


def _experts(tile_expert, n_tiles, xs, wg, wu, wd):
    tm = MLP_ROW_TILE
    nf = EXPERT_DIM // EXPERT_TILE

    def f_blk(i, f, nt):
        return jnp.where(i < nt[0], f, nf - 1)

    grid_spec = pltpu.PrefetchScalarGridSpec(
        num_scalar_prefetch=2,
        grid=(N_SORTED_TILES, nf),
        in_specs=[pl.BlockSpec((tm, D), lambda i, f, te, nt: (jnp.minimum(i, nt[0] - 1), 0)),
                  pl.BlockSpec((None, D, EXPERT_TILE), lambda i, f, te, nt: (te[i], 0, f_blk(i, f, nt))),
                  pl.BlockSpec((None, D, EXPERT_TILE), lambda i, f, te, nt: (te[i], 0, f_blk(i, f, nt))),
                  pl.BlockSpec((None, EXPERT_TILE, D), lambda i, f, te, nt: (te[i], f_blk(i, f, nt), 0))],
        out_specs=pl.BlockSpec((tm, D), lambda i, f, te, nt: (i, 0)),
        scratch_shapes=[pltpu.VMEM((tm, D), BF16)],
    )
    return pl.pallas_call(
        _experts_kernel,
        out_shape=jax.ShapeDtypeStruct((N_SORTED, D), F32),
        grid_spec=grid_spec,
        compiler_params=_params(("arbitrary", "arbitrary")),
        name="moe_experts",
    )(tile_expert, n_tiles, xs, wg, wu, wd)


def _combine_kernel(d1_ref, d2_ref, n1_ref, n2_ref, w_ref, h_ref, m_ref, g_ref, ys_ref, o_ref, buf_ref, sems):
    i, n = pl.program_id(0), pl.num_programs(0)
    slot = i % 2

    def gather(a_ref, b_ref, slot, start):
        def body(t, _):
            for choice, d_ref in enumerate((a_ref, b_ref)):
                copy = _row_copy(ys_ref, d_ref[0, t], buf_ref.at[slot, choice], t, sems.at[slot])
                copy.start() if start else copy.wait()
            return 0

        lax.fori_loop(0, ROUTE_TILE, body, 0, unroll=DMA_UNROLL)

    @pl.when(i == 0)
    def _():
        gather(d1_ref, d2_ref, 0, True)

    @pl.when(i + 1 < n)
    def _():
        gather(n1_ref, n2_ref, 1 - slot, True)

    gather(d1_ref, d2_ref, slot, False)
    w = w_ref[...]
    lane = lax.broadcasted_iota(jnp.int32, w.shape, 1)
    w1 = jnp.sum(jnp.where(lane == 0, w, 0.0), axis=-1, keepdims=True)
    w2 = jnp.sum(jnp.where(lane == 1, w, 0.0), axis=-1, keepdims=True)
    y = w1 * buf_ref[slot, 0] + w2 * buf_ref[slot, 1]
    o_ref[...] = h_ref[...] + m_ref[5:6, :] * _rms(y, g_ref[3:4, :])


def _combine(d1, d2, wts, h, mods, g, ys):
    tm = ROUTE_TILE
    n = N_LAT // tm
    row = lambda w: pl.BlockSpec((tm, w), lambda i: (i, 0))
    nxt = pl.BlockSpec((None, 1, tm), lambda i: (jnp.minimum(i + 1, n - 1), 0, 0), memory_space=pltpu.SMEM)
    return pl.pallas_call(
        _combine_kernel,
        out_shape=jax.ShapeDtypeStruct((N_LAT, D), F32),
        grid=(n,),
        in_specs=[_route_spec(), _route_spec(), nxt, nxt, row(LANE), row(D), _mod_spec(1, tm), _full((4, D)),
                  pl.BlockSpec(memory_space=pl.ANY)],
        out_specs=row(D),
        scratch_shapes=[pltpu.VMEM((2, 2, tm, D), F32), pltpu.SemaphoreType.DMA((2,))],
        compiler_params=_params(("arbitrary",)),
        name="moe_combine",
    )(d1, d2, d1, d2, wts, h, mods, g, ys)


def _routing(route):
    tm = MLP_ROW_TILE
    e1, e2 = route[:, 0], route[:, 1]
    experts = jnp.arange(N_EXPERTS, dtype=jnp.int32)
    hit = (e1[:, None] == experts).astype(jnp.int32) + (e2[:, None] == experts).astype(jnp.int32)
    pos = jnp.cumsum(hit, axis=0) - hit
    padded = (jnp.sum(hit, axis=0) + tm - 1) // tm * tm
    ends = jnp.cumsum(padded)
    starts = ends - padded
    d1 = starts[e1] + jnp.take_along_axis(pos, e1[:, None], axis=1)[:, 0]
    d2 = starts[e2] + jnp.take_along_axis(pos, e2[:, None], axis=1)[:, 0]
    n_tiles = ends[-1] // tm
    tile_start = jnp.minimum(jnp.arange(N_SORTED_TILES, dtype=jnp.int32), n_tiles - 1) * tm
    tile_expert = jnp.sum((tile_start[:, None] >= ends[None, :]).astype(jnp.int32), axis=1)
    shape = (N_LAT // ROUTE_TILE, 1, ROUTE_TILE)
    return (d1.astype(jnp.int32).reshape(shape), d2.astype(jnp.int32).reshape(shape),
            tile_expert.astype(jnp.int32), n_tiles.astype(jnp.int32).reshape(1))


GLA_Q0, GLA_K0, GLA_V0, GLA_G0, GLA_R0 = 0, 512, 1024, 2048, 3072
GLA_IN_P = GLA_R0 + LANE


def _gla_proj_kernel(h_ref, m_ref, g_ref, w_in_ref, w_vt_ref, q_ref, k_ref, v_ref, vt_ref, sg_ref, r_ref):
    u = _modulated(h_ref[...], g_ref[0:1, :], m_ref[0:1, :], m_ref[1:2, :]).astype(BF16)
    z = _dot(u, w_in_ref[...])
    q_ref[...] = (z[:, GLA_Q0:GLA_K0] * (GLA_DK ** -0.5)).astype(BF16)
    k_ref[...] = z[:, GLA_K0:GLA_V0].astype(BF16)
    v_ref[...] = z[:, GLA_V0:GLA_G0].astype(BF16)
    vt_ref[...] = _dot_nt(w_vt_ref[...], u).astype(BF16)
    sg_ref[...] = _silu(z[:, GLA_G0:GLA_R0]).astype(BF16)
    r_ref[...] = z[:, GLA_R0:].astype(BF16)


def _gla_project(h, mods, g, w):
    tm = ROW_TILE
    row = lambda n, dt=None: pl.BlockSpec((tm, n), lambda i: (i, 0))
    hk, hv = GLA_HEADS * GLA_DK, GLA_HEADS * GLA_DV
    sds = jax.ShapeDtypeStruct
    return pl.pallas_call(
        _gla_proj_kernel,
        out_shape=(sds((N_ROWS, hk), BF16), sds((N_ROWS, hk), BF16), sds((N_ROWS, hv), BF16),
                   sds((hv, N_ROWS), BF16), sds((N_ROWS, hv), BF16), sds((N_ROWS, LANE), BF16)),
        grid=(N_ROWS // tm,),
        in_specs=[row(D), _mod_spec(1, tm), _full((4, D)), _full((D, GLA_IN_P)), _full((hv, D))],
        out_specs=(row(hk), row(hk), row(hv), pl.BlockSpec((hv, tm), lambda i: (0, i)), row(hv), row(LANE)),
        compiler_params=_params(("parallel",)),
        name="gla_project",
    )(h, mods, g, w["w_in"], w["w_vt"])


GLA_LAT_BLOCKS = SEQ // GLA_BLOCK
GLA_CTX_BLOCKS = CTX // GLA_BLOCK
GLA_STRIPS = GLA_BLOCK // GLA_SUB


def _gla_blocks(chains):
    n = GLA_BLOCK
    r = lax.broadcasted_iota(jnp.int32, (n, n), 0)
    c = lax.broadcasted_iota(jnp.int32, (n, n), 1)
    row = lax.broadcasted_iota(jnp.int32, (n, GLA_DK), 0)
    keep = {rev: (c >= r) if rev else (c <= r) for rev in (False, True)}
    tri = {rev: jnp.where(keep[rev], 1.0, 0.0).astype(BF16) for rev in (False, True)}

    cums = []
    for ch in chains:
        la_hi, la_lo = _split_bf16(ch["la"])
        cums.append(_dot(tri[ch["reverse"]], la_hi) + _dot(tri[ch["reverse"]], la_lo))

    states, outs = [], []
    for ch, cum in zip(chains, cums):
        total = cum[0:1, :] if ch["reverse"] else cum[n - 1:n, :]
        k_state = (ch["k"] * jnp.exp(total - cum)).astype(BF16)
        states.append(ch["state"] * jnp.exp(total) + _dot(ch["vt"], k_state))
        if ch["q"] is None:
            outs.append(None)
        else:
            outs.append(_dot_nt((ch["q"] * jnp.exp(cum)).astype(BF16), ch["state"].astype(BF16)))

    strips = [[] for _ in chains]
    for i in range(GLA_STRIPS):
        lo, hi = i * GLA_SUB, (i + 1) * GLA_SUB
        for ci, (ch, cum) in enumerate(zip(chains, cums)):
            if ch["q"] is None:
                continue
            if ch["reverse"]:
                ref = cum[hi:hi + 1, :] if i < GLA_STRIPS - 1 else jnp.zeros((1, GLA_DK), F32)
                live = row >= lo
            else:
                ref = cum[lo - 1:lo, :] if i > 0 else jnp.zeros((1, GLA_DK), F32)
                live = row < hi
            q_loc = (ch["q"][lo:hi, :] * jnp.exp(cum[lo:hi, :] - ref)).astype(BF16)
            k_loc = jnp.where(live, ch["k"] * jnp.exp(ref - cum), 0.0).astype(BF16)
            strips[ci].append(_dot_nt(q_loc, k_loc))

    for ci, ch in enumerate(chains):
        if ch["q"] is not None:
            scores = jnp.where(keep[ch["reverse"]], jnp.concatenate(strips[ci], axis=0), 0.0).astype(BF16)
            outs[ci] = outs[ci] + _dot(scores, ch["v"])
    return list(zip(outs, states))


GLA_GROUP = 2


def _gla_kernel(q_ref, kl_ref, kc_ref, vl_ref, vtl_ref, vtc_ref, rl_ref, rc_ref,
                wgf_ref, wgb_ref, bgf_ref, bgb_ref, o_ref, s_ref):
    n = GLA_BLOCK
    s_ref[...] = jnp.zeros_like(s_ref)
    chains = [(hd, rev) for hd in range(GLA_GROUP) for rev in (False, True)]

    def log_decay(r_blk, hd, reverse):
        wg_ref, bg_ref = (wgb_ref, bgb_ref) if reverse else (wgf_ref, bgf_ref)
        ks = slice(hd * GLA_DK, (hd + 1) * GLA_DK)
        zg = _dot(r_blk, wg_ref[:, ks]) + bg_ref[:, ks]
        return (jnp.minimum(zg, 0.0) - jnp.log(1.0 + jnp.exp(-jnp.abs(zg)))) * (1.0 / GLA_GATE_NORM)

    def chain(hd, reverse, rows, k_ref, vt_ref, r_ref, with_output):
        ks = slice(hd * GLA_DK, (hd + 1) * GLA_DK)
        vs = slice(hd * GLA_DV, (hd + 1) * GLA_DV)
        return dict(reverse=reverse, k=k_ref[rows, ks].astype(F32), vt=vt_ref[vs, rows],
                    la=log_decay(r_ref[rows, :], hd, reverse), state=s_ref[int(reverse), hd],
                    q=q_ref[rows, ks].astype(F32) if with_output else None,
                    v=vl_ref[rows, vs] if with_output else None)

    for j in range(GLA_CTX_BLOCKS):
        work = []
        for hd, reverse in chains:
            blk = (GLA_CTX_BLOCKS - 1 - j) if reverse else j
            work.append(chain(hd, reverse, slice(blk * n, (blk + 1) * n), kc_ref, vtc_ref, rc_ref, False))
        for (hd, reverse), (_, s_new) in zip(chains, _gla_blocks(work)):
            s_ref[int(reverse), hd] = s_new

    def step(j, accumulate):
        work, where = [], []
        for hd, reverse in chains:
            blk = (GLA_LAT_BLOCKS - 1 - j) if reverse else j
            rows = pl.ds(pl.multiple_of(blk * n, n), n)
            work.append(chain(hd, reverse, rows, kl_ref, vtl_ref, rl_ref, True))
            where.append((rows, slice(hd * GLA_DV, (hd + 1) * GLA_DV)))
        for (hd, reverse), (rows, vs), (o, s_new) in zip(chains, where, _gla_blocks(work)):
            s_ref[int(reverse), hd] = s_new
            if accumulate:
                o_ref[rows, vs] += o
            else:
                o_ref[rows, vs] = o

    half = GLA_LAT_BLOCKS // 2

    def first(j, _):
        step(j, False)
        return 0

    def second(j, _):
        step(j, True)
        return 0

    lax.fori_loop(0, half, first, 0)
    lax.fori_loop(half, GLA_LAT_BLOCKS, second, 0)


def _gla_scan(q, k, v, vt, r, w_gate_f, w_gate_b, b_gate_f, b_gate_b):
    gk, gv = GLA_GROUP * GLA_DK, GLA_GROUP * GLA_DV
    lat = lambda n: pl.BlockSpec((SEQ, n), lambda b, g: (b, g))
    ctx = lambda n: pl.BlockSpec((CTX, n), lambda b, g: (N_LAT // CTX + b, g))
    grp = lambda rows: pl.BlockSpec((rows, gk), lambda b, g: (0, g))
    return pl.pallas_call(
        _gla_kernel,
        out_shape=jax.ShapeDtypeStruct((N_LAT, GLA_HEADS * GLA_DV), F32),
        grid=(BATCH, GLA_HEADS // GLA_GROUP),
        in_specs=[lat(gk), lat(gk), ctx(gk), lat(gv),
                  pl.BlockSpec((gv, SEQ), lambda b, g: (g, b)),
                  pl.BlockSpec((gv, CTX), lambda b, g: (g, N_LAT // CTX + b)),
                  pl.BlockSpec((SEQ, LANE), lambda b, g: (b, 0)),
                  pl.BlockSpec((CTX, LANE), lambda b, g: (N_LAT // CTX + b, 0)),
                  grp(LANE), grp(LANE), grp(1), grp(1)],
        out_specs=lat(gv),
        scratch_shapes=[pltpu.VMEM((2, GLA_GROUP, GLA_DV, GLA_DK), F32)],
        compiler_params=_params(("parallel", "parallel")),
        name="gla_scan",
    )(q, k, k, v, vt, vt, r, r, w_gate_f, w_gate_b, b_gate_f, b_gate_b)


def _gla_out_kernel(o_ref, sg_ref, on_ref, w_ref, h_ref, m_ref, g_ref, wr_hi_ref, wr_lo_ref, br_ref,
                    h_out_ref, u_ref, route_ref, wts_ref):
    parts = []
    for hd in range(GLA_HEADS):
        sl = slice(hd * GLA_DV, (hd + 1) * GLA_DV)
        parts.append(_rms(o_ref[:, sl], on_ref[...]) * sg_ref[:, sl].astype(F32))
    y = _dot(jnp.concatenate(parts, axis=-1).astype(BF16), w_ref[...])
    h_new, u = _mixer_epilogue(y, h_ref[...], m_ref, g_ref)
    h_out_ref[...] = h_new
    u_ref[...] = u
    logits = _dot3(u, wr_hi_ref[...], wr_lo_ref[...]) + br_ref[...]
    lane = lax.broadcasted_iota(jnp.int32, logits.shape, 1)
    l1 = jnp.max(logits, axis=-1, keepdims=True)
    i1 = jnp.min(jnp.where(logits == l1, lane, LANE), axis=-1, keepdims=True)
    rest = jnp.where(lane == i1, -jnp.inf, logits)
    l2 = jnp.max(rest, axis=-1, keepdims=True)
    i2 = jnp.min(jnp.where(rest == l2, lane, LANE), axis=-1, keepdims=True)
    e2 = jnp.exp(l2 - l1)
    route_ref[...] = jnp.where(lane == 0, i1, jnp.where(lane == 1, i2, 0))
    wts_ref[...] = jnp.where(lane == 0, 1.0 / (1.0 + e2), jnp.where(lane == 1, e2 / (1.0 + e2), 0.0))


def _gla_output(o, sg, o_norm, w_out, h, mods, g, wr_hi, wr_lo, br):
    tm = ROW_TILE
    row = lambda n: pl.BlockSpec((tm, n), lambda i: (i, 0))
    hv = GLA_HEADS * GLA_DV
    return pl.pallas_call(
        _gla_out_kernel,
        out_shape=(jax.ShapeDtypeStruct((N_LAT, D), F32), jax.ShapeDtypeStruct((N_LAT, D), F32),
                   jax.ShapeDtypeStruct((N_LAT, LANE), jnp.int32), jax.ShapeDtypeStruct((N_LAT, LANE), F32)),
        grid=(N_LAT // tm,),
        in_specs=[row(hv), row(hv), _full((1, GLA_DV)), _full((hv, D)), row(D), _mod_spec(1, tm),
                  _full((4, D)), _full((D, LANE)), _full((D, LANE)), _full((1, LANE))],
        out_specs=(row(D), row(D), row(LANE), row(LANE)),
        compiler_params=_params(("parallel",)),
        name="gla_output",
    )(o, sg, o_norm, w_out, h, mods, g, wr_hi, wr_lo, br)


def _slots(w, n_heads, width):
    k = w.shape[0]
    w = w.reshape(k, n_heads, width)
    return jnp.pad(w, ((0, 0), (0, 0), (0, LANE - width))).reshape(k, n_heads * LANE)


def _attn_weights(w_in, q_norm, w_uq, kv_norm, w_ukv, qk_norm, w_out):
    c = 0
    cols = {}
    for name, n in (("q_lat", MLA_Q_RANK), ("kv_lat", MLA_KV_RANK), ("k_rope", MLA_ROPE),
                    ("q_b", GQA_HEADS * GQA_HEAD_DIM), ("k_b", GQA_KV_HEADS * GQA_HEAD_DIM),
                    ("v_b", GQA_KV_HEADS * GQA_HEAD_DIM)):
        cols[name] = w_in[:, c:c + n]
        c += n
    k_rope = jnp.pad(cols["k_rope"], ((0, 0), (MLA_NOPE, LANE - MLA_NOPE - MLA_ROPE)))
    w_in_p = jnp.concatenate([cols["q_lat"], cols["kv_lat"], k_rope,
                              _slots(cols["q_b"], GQA_HEADS, GQA_HEAD_DIM),
                              _slots(cols["k_b"], GQA_KV_HEADS, GQA_HEAD_DIM)], axis=1)
    ukv = w_ukv.reshape(MLA_KV_RANK, MLA_HEADS, MLA_NOPE + MLA_V)
    w_uk = _slots(ukv[:, :, :MLA_NOPE].reshape(MLA_KV_RANK, -1), MLA_HEADS, MLA_NOPE)
    w_uv = ukv[:, :, MLA_NOPE:].reshape(MLA_KV_RANK, MLA_HEADS * MLA_V)
    return {
        "w_in": w_in_p.astype(BF16),
        "q_norm": q_norm.reshape(1, -1),
        "w_uq": _slots(w_uq, MLA_HEADS, MLA_NOPE + MLA_ROPE).astype(BF16),
        "kv_norm": kv_norm.reshape(1, -1),
        "w_uk": w_uk.astype(BF16),
        "w_uvt": w_uv.T.astype(BF16),
        "w_vbt": cols["v_b"].T.astype(BF16),
        "qk_norm": jnp.pad(qk_norm, ((0, 0), (0, LANE - GQA_HEAD_DIM))),
        "w_out": w_out.astype(BF16),
    }


def _axial_tables(d):
    half, p = d // 2, d // 4
    t = jnp.arange(SEQ, dtype=jnp.int32)
    freqs = ROPE_THETA ** (-jnp.arange(0, half, 2, dtype=F32) / half)
    cos, sa, sb = [], [], []
    for pos in (t // GRID_W, t % GRID_W):
        ang = pos.astype(F32)[:, None] * freqs[None, :]
        c, s, z = jnp.cos(ang), jnp.sin(ang), jnp.zeros_like(ang)
        cos += [c, c]
        sa += [-s, z]
        sb += [z, s]
    return [jnp.concatenate(x, axis=-1) for x in (cos, sa, sb)]


def _rope_tables():
    def embed(tbl, left, fill):
        right = LANE - left - tbl.shape[1]
        tbl = jnp.pad(tbl, ((0, 0), (left, right)), constant_values=fill)
        return jnp.pad(tbl, ((0, ATTN_PROJ_TILE), (0, 0)), constant_values=fill)

    out = []
    for d, left in ((MLA_ROPE, MLA_NOPE), (GQA_HEAD_DIM, 0)):
        cos, sa, sb = _axial_tables(d)
        out += [embed(cos, left, 1.0), embed(sa, left, 0.0), embed(sb, left, 0.0)]
    return out


def _gla_weights(w_in, w_gate2, b_gate):
    hk, hv = GLA_HEADS * GLA_DK, GLA_HEADS * GLA_DV
    r = jnp.pad(w_in[:, 2 * hk + 2 * hv:], ((0, 0), (0, LANE - 2 * GLA_GATE_RANK)))
    w_in_p = jnp.concatenate([w_in[:, :2 * hk + 2 * hv], r], axis=1)
    pad_f = ((0, LANE - GLA_GATE_RANK), (0, 0))
    pad_b = ((GLA_GATE_RANK, LANE - 2 * GLA_GATE_RANK), (0, 0))
    return {
        "w_in": w_in_p.astype(BF16),
        "w_vt": w_in[:, 2 * hk:2 * hk + hv].T.astype(BF16),
        "w_gate_f": jnp.pad(w_gate2[0], pad_f).astype(BF16),
        "w_gate_b": jnp.pad(w_gate2[1], pad_b).astype(BF16),
        "b_gate_f": b_gate[0].reshape(1, hk),
        "b_gate_b": b_gate[1].reshape(1, hk),
    }


def kernel(x, c, ctx, c_ctx, mod_w, mod_b, norm_g, attn_w_in, attn_q_norm, attn_w_uq, attn_kv_norm, attn_w_ukv,
           attn_qk_norm, attn_w_out, gla_w_in, gla_w_gate2, gla_b_gate, gla_o_norm, gla_w_out, ffn_w_gate,
           ffn_w_up, ffn_w_down, moe_w_router, moe_b_router, moe_w_gate, moe_w_up, moe_w_down):
    assert x.shape == (BATCH, SEQ, D) and ctx.shape == (BATCH, CTX, D)
    x, ctx = x.reshape(N_LAT, D), ctx.reshape(N_CTX, D)
    cc = jnp.concatenate([c, c_ctx[None, :], jnp.zeros((MOD_ROWS - BATCH - 1, D), F32)], axis=0)
    mods = _mod_vectors(cc, mod_w, mod_b).reshape(mod_w.shape[0], MOD_ROWS, 6, D)

    aw = _attn_weights(attn_w_in[0], attn_q_norm[0], attn_w_uq[0], attn_kv_norm[0], attn_w_ukv[0],
                       attn_qk_norm[0], attn_w_out[0])
    q, k, vt = _attn_project(x, ctx, mods, norm_g[0], aw, _rope_tables())
    o = _attention(q, k, vt)
    h, u = _attn_output(o, aw["w_out"], x, ctx, mods, norm_g[0])
    h = _ffn(u, ffn_w_gate[0].astype(BF16), ffn_w_up[0].astype(BF16), ffn_w_down[0].astype(BF16), h, mods,
             norm_g[0])

    gw = _gla_weights(gla_w_in[0], gla_w_gate2[0], gla_b_gate[0])
    gq, gk, gv, gvt, sg, gr = _gla_project(h, mods, norm_g[1], gw)
    go = _gla_scan(gq, gk, gv, gvt, gr, gw["w_gate_f"], gw["w_gate_b"], gw["b_gate_f"], gw["b_gate_b"])
    wr = jnp.pad(moe_w_router[0], ((0, 0), (0, LANE - N_EXPERTS)))
    wr_hi = wr.astype(BF16)
    wr_lo = (wr - wr_hi.astype(F32)).astype(BF16)
    br = jnp.pad(moe_b_router[0], (0, LANE - N_EXPERTS), constant_values=-jnp.inf).reshape(1, LANE)
    h, u, route, wts = _gla_output(go, sg, gla_o_norm[0].reshape(1, GLA_DV), gla_w_out[0].astype(BF16), h,
                                   mods, norm_g[1], wr_hi, wr_lo, br)
    d1, d2, tile_expert, n_tiles = _routing(route)
    xs = _dispatch(d1, d2, u)
    ys = _experts(tile_expert, n_tiles, xs, moe_w_gate[0].astype(BF16), moe_w_up[0].astype(BF16),
                  moe_w_down[0].astype(BF16))
    h = _combine(d1, d2, wts, h, mods, norm_g[1], ys)
    return h.reshape(BATCH, SEQ, D)
```

```python
import jax
import jax.numpy as jnp
from jax import lax
from jax.experimental import pallas as pl
from jax.experimental.pallas import tpu as pltpu

F32 = jnp.float32
BF16 = jnp.bfloat16

D = 1024
BATCH = 4
SEQ = 4096
CTX = 256
GRID_W = 64
ROPE_THETA = 10000.0
EPS = 1e-6

N_LAT = BATCH * SEQ
N_CTX = BATCH * CTX
N_ROWS = N_LAT + N_CTX
MOD_ROWS = 8
CTX_MOD_ROW = BATCH

LANE = 128
ROW_TILE = 512
ATTN_PROJ_TILE = 256

MLA_HEADS = 8
MLA_Q_RANK = 384
MLA_KV_RANK = 256
MLA_NOPE = 64
MLA_ROPE = 32
MLA_V = 64
GQA_HEADS = 8
GQA_KV_HEADS = 2
GQA_GROUP = GQA_HEADS // GQA_KV_HEADS
GQA_HEAD_DIM = 64
N_Q_HEADS = MLA_HEADS + GQA_HEADS

GLA_HEADS = 4
GLA_DK = 128
GLA_DV = 256
GLA_GATE_RANK = 16
GLA_GATE_NORM = 16.0
GLA_BLOCK = 128
GLA_SUB = 32

MXU_WIDTH = 256
FFN_DIM = 2816
N_EXPERTS = 8
EXPERT_DIM = 3584
EXPERT_TILE = EXPERT_DIM // 2
MLP_CHUNK = 2 * MXU_WIDTH
MLP_ROW_TILE = 512

VMEM_LIMIT = 56 * 1024 * 1024


def _params(sem):
    return pltpu.CompilerParams(dimension_semantics=sem, vmem_limit_bytes=VMEM_LIMIT)


def _rms(x, g):
    return x * lax.rsqrt(jnp.mean(x * x, axis=-1, keepdims=True) + EPS) * g


def _silu(x):
    return x / (1.0 + jnp.exp(-x))


def _split_bf16(x):
    hi = x.astype(BF16)
    lo = (x - hi.astype(F32)).astype(BF16)
    return hi, lo


def _dot(a, b):
    return jnp.dot(a, b, preferred_element_type=F32)


def _dot_nt(a, b):
    return lax.dot_general(a, b, (((1,), (1,)), ((), ())), preferred_element_type=F32)


def _dot3(a, b_hi, b_lo):
    a_hi, a_lo = _split_bf16(a)
    return _dot(a_hi, b_hi) + (_dot(a_hi, b_lo) + _dot(a_lo, b_hi))


def _modulated(h, g_row, shift, scale):
    return _rms(h, g_row) * (1.0 + scale) + shift


def _rope(x, cos, sa, sb, p):
    return x * cos + pltpu.roll(x, LANE - p, 1) * sa + pltpu.roll(x, p, 1) * sb


def _mod_row(i, tile):
    r0 = i * tile
    return jnp.where(r0 < N_LAT, r0 // SEQ, CTX_MOD_ROW)


def _mod_spec(layer, tile):
    return pl.BlockSpec((None, None, 6, D), lambda i, *_: (layer, _mod_row(i, tile), 0, 0))


def _rope_block(i):
    r0 = i * ATTN_PROJ_TILE
    return jnp.where(r0 < N_LAT, (r0 % SEQ) // ATTN_PROJ_TILE, SEQ // ATTN_PROJ_TILE)


def _full(shape):
    return pl.BlockSpec(shape, lambda *_: (0,) * len(shape))


def _mod_kernel(c_ref, w_ref, b_ref, o_ref):
    w_hi, w_lo = _split_bf16(w_ref[...])
    o_ref[...] = _dot3(_silu(c_ref[...]), w_hi, w_lo) + b_ref[...]


def _mod_vectors(cc, mod_w, mod_b):
    depth, _, n = mod_w.shape
    tn = 1536
    return pl.pallas_call(
        _mod_kernel,
        out_shape=jax.ShapeDtypeStruct((depth, MOD_ROWS, n), F32),
        grid=(depth, n // tn),
        in_specs=[
            pl.BlockSpec((MOD_ROWS, D), lambda l, j: (0, 0)),
            pl.BlockSpec((None, D, tn), lambda l, j: (l, 0, j)),
            pl.BlockSpec((None, 1, tn), lambda l, j: (l, 0, j)),
        ],
        out_specs=pl.BlockSpec((None, MOD_ROWS, tn), lambda l, j: (l, 0, j)),
        compiler_params=_params(("parallel", "parallel")),
        name="mod_vectors",
    )(cc, mod_w, mod_b.reshape(depth, 1, n))


Q_LAT0, KV_LAT0, K_ROPE0 = 0, 384, 640
Q_B0 = 768
K_B0 = Q_B0 + GQA_HEADS * LANE
ATTN_IN_P = K_B0 + GQA_KV_HEADS * LANE
ATTN_V = 64
ATTN_VA = ATTN_V + 16
LOG2E = 1.4426950408889634


def _input_rows(x_ref, c_ref):
    return jnp.where(pl.program_id(0) < N_LAT // x_ref.shape[0], x_ref[...], c_ref[...])


def _input_specs(tm):
    n_lat = N_LAT // tm
    return [pl.BlockSpec((tm, D), lambda i: (jnp.minimum(i, n_lat - 1), 0)),
            pl.BlockSpec((tm, D), lambda i: (jnp.maximum(i - n_lat, 0), 0))]


def _attn_proj_kernel(x_ref, c_ref, m_ref, g_ref, w_in_ref, qn_ref, w_uq_ref, kvn_ref, w_uk_ref, w_uvt_ref,
                      w_vbt_ref, qkn_ref, ca_ref, saa_ref, sba_ref, cb_ref, sab_ref, sbb_ref,
                      q_ref, k_ref, vt_ref):
    u = _modulated(_input_rows(x_ref, c_ref), g_ref[0:1, :], m_ref[0:1, :], m_ref[1:2, :]).astype(BF16)
    z = _dot(u, w_in_ref[...])
    q_lat = _rms(z[:, Q_LAT0:Q_LAT0 + MLA_Q_RANK], qn_ref[...]).astype(BF16)
    kv_lat = _rms(z[:, KV_LAT0:KV_LAT0 + MLA_KV_RANK], kvn_ref[...]).astype(BF16)
    q_a = _dot(q_lat, w_uq_ref[...])
    k_a = _dot(kv_lat, w_uk_ref[...])
    ca, saa, sba = ca_ref[...], saa_ref[...], sba_ref[...]
    cb, sab, sbb = cb_ref[...], sab_ref[...], sbb_ref[...]
    pa, pb = MLA_ROPE // 4, GQA_HEAD_DIM // 4
    scale_a = (MLA_NOPE + MLA_ROPE) ** -0.5 * LOG2E
    scale_b = GQA_HEAD_DIM ** -0.5 * LOG2E
    k_rope = _rope(z[:, K_ROPE0:K_ROPE0 + LANE], ca, saa, sba, pa)
    for hd in range(MLA_HEADS):
        sl = slice(hd * LANE, (hd + 1) * LANE)
        q_ref[:, sl] = (_rope(q_a[:, sl], ca, saa, sba, pa) * scale_a).astype(BF16)
        k_ref[:, sl] = (k_a[:, sl] + k_rope).astype(BF16)

    def head_norm(x, gain):
        ms = jnp.sum(x * x, axis=-1, keepdims=True) * (1.0 / GQA_HEAD_DIM)
        return x * lax.rsqrt(ms + EPS) * gain

    for hd in range(GQA_HEADS):
        x = head_norm(z[:, Q_B0 + hd * LANE:Q_B0 + (hd + 1) * LANE], qkn_ref[0:1, :])
        sl = slice((MLA_HEADS + hd) * LANE, (MLA_HEADS + hd + 1) * LANE)
        q_ref[:, sl] = (_rope(x, cb, sab, sbb, pb) * scale_b).astype(BF16)
    for kv in range(GQA_KV_HEADS):
        x = head_norm(z[:, K_B0 + kv * LANE:K_B0 + (kv + 1) * LANE], qkn_ref[1:2, :])
        x = _rope(x, cb, sab, sbb, pb).astype(BF16)
        for hd in range(kv * GQA_GROUP, (kv + 1) * GQA_GROUP):
            k_ref[:, (MLA_HEADS + hd) * LANE:(MLA_HEADS + hd + 1) * LANE] = x

    vat = _dot_nt(w_uvt_ref[...], kv_lat).astype(BF16)
    vbt = _dot_nt(w_vbt_ref[...], u).astype(BF16)
    ones = jnp.ones((ATTN_VA - ATTN_V, vat.shape[1]), BF16)
    for hd in range(N_Q_HEADS):
        if hd < MLA_HEADS:
            v_t = vat[hd * ATTN_V:(hd + 1) * ATTN_V, :]
        else:
            kv = (hd - MLA_HEADS) // GQA_GROUP
            v_t = vbt[kv * ATTN_V:(kv + 1) * ATTN_V, :]
        vt_ref[hd * ATTN_VA:hd * ATTN_VA + ATTN_V, :] = v_t
        vt_ref[hd * ATTN_VA + ATTN_V:(hd + 1) * ATTN_VA, :] = ones


def _attn_project(x, ctx, mods, g, w, tables):
    tm = ATTN_PROJ_TILE
    row = lambda n: pl.BlockSpec((tm, n), lambda i: (i, 0))
    tab = pl.BlockSpec((tm, LANE), lambda i: (_rope_block(i), 0))
    return pl.pallas_call(
        _attn_proj_kernel,
        out_shape=(jax.ShapeDtypeStruct((N_ROWS, N_Q_HEADS * LANE), BF16),
                   jax.ShapeDtypeStruct((N_ROWS, N_Q_HEADS * LANE), BF16),
                   jax.ShapeDtypeStruct((N_Q_HEADS * ATTN_VA, N_ROWS), BF16)),
        grid=(N_ROWS // tm,),
        in_specs=[*_input_specs(tm), _mod_spec(0, tm), _full((4, D)), _full((D, ATTN_IN_P)),
                  _full((1, MLA_Q_RANK)), _full((MLA_Q_RANK, MLA_HEADS * LANE)),
                  _full((1, MLA_KV_RANK)), _full((MLA_KV_RANK, MLA_HEADS * LANE)),
                  _full((MLA_HEADS * ATTN_V, MLA_KV_RANK)), _full((GQA_KV_HEADS * ATTN_V, D)),
                  _full((2, LANE)), tab, tab, tab, tab, tab, tab],
        out_specs=(row(N_Q_HEADS * LANE), row(N_Q_HEADS * LANE),
                   pl.BlockSpec((N_Q_HEADS * ATTN_VA, tm), lambda i: (0, i))),
        compiler_params=_params(("parallel",)),
        name="attn_project",
    )(x, ctx, mods, g, w["w_in"], w["q_norm"], w["w_uq"], w["kv_norm"], w["w_uk"], w["w_uvt"], w["w_vbt"],
      w["qk_norm"], *tables)


ATT_TQ = 256
ATT_TK = 1024
ATT_LAT_TILES = SEQ // ATT_TQ
ATT_GROUP = 4


def _attn_kernel(ql_ref, qc_ref, kc_ref, kl_ref, vtc_ref, vtl_ref, ol_ref, oc_ref, sa_ref, sb_ref):
    heads = range(ATT_GROUP)
    n_lat = SEQ // ATT_TK
    every = slice(None)
    init = (jnp.full((1, ATT_TQ), -jnp.inf, F32), jnp.zeros((ATTN_VA, ATT_TQ), F32))

    def lat_queries(t, hd):
        return ql_ref[pl.ds(pl.multiple_of(t * ATT_TQ, ATT_TQ), ATT_TQ), hd * LANE:(hd + 1) * LANE]

    def lat_keys(j):
        return slice(j * ATT_TK, (j + 1) * ATT_TK)

    def scores(s_ref, hd, q, k_ref, keys, n_keys):
        s = _dot_nt(k_ref[keys, hd * LANE:(hd + 1) * LANE], q)
        s_ref[hd, 0:n_keys, :] = s
        return jnp.max(s, axis=0, keepdims=True)

    def consume(s_ref, hd, vt_ref, keys, n_keys, cmax, carry):
        m, acc = carry
        m_new = jnp.maximum(m, cmax)
        p = jnp.exp2(s_ref[hd, 0:n_keys, :] - m_new).astype(BF16)
        acc = jnp.exp2(m - m_new) * acc + _dot(vt_ref[hd * ATTN_VA:(hd + 1) * ATTN_VA, keys], p)
        return m_new, acc

    def normalised(carries):
        o_t = jnp.concatenate([acc[0:ATTN_V, :] / acc[ATTN_V:ATTN_V + 1, :] for _, acc in carries], axis=0)
        return o_t.T.astype(BF16)

    def query_tile(t, t_next, first_ref, second_ref, cmax):
        bufs = (first_ref, second_ref)
        carries = (init,) * ATT_GROUP
        for c in range(n_lat + 1):
            cur, nxt = bufs[c % 2], bufs[(c + 1) % 2]
            cmax_next, out = [], []
            for hd in heads:
                if c + 1 < n_lat:
                    cmax_next.append(scores(nxt, hd, lat_queries(t, hd), kl_ref, lat_keys(c + 1), ATT_TK))
                elif c + 1 == n_lat:
                    cmax_next.append(scores(nxt, hd, lat_queries(t, hd), kc_ref, every, CTX))
                else:
                    cmax_next.append(scores(nxt, hd, lat_queries(t_next, hd), kl_ref, lat_keys(0), ATT_TK))
                if c < n_lat:
                    out.append(consume(cur, hd, vtl_ref, lat_keys(c), ATT_TK, cmax[hd], carries[hd]))
                else:
                    out.append(consume(cur, hd, vtc_ref, every, CTX, cmax[hd], carries[hd]))
            cmax, carries = cmax_next, out
        ol_ref[pl.ds(pl.multiple_of(t * ATT_TQ, ATT_TQ), ATT_TQ), :] = normalised(carries)
        return tuple(cmax)

    def tile_pair(i, cmax):
        cmax = query_tile(2 * i, 2 * i + 1, sa_ref, sb_ref, cmax)
        return query_tile(2 * i + 1, jnp.minimum(2 * i + 2, ATT_LAT_TILES - 1), sb_ref, sa_ref, cmax)

    cmax0 = tuple(scores(sa_ref, hd, lat_queries(0, hd), kl_ref, lat_keys(0), ATT_TK) for hd in heads)
    lax.fori_loop(0, ATT_LAT_TILES // 2, tile_pair, cmax0)

    cmax_c = [scores(sa_ref, hd, qc_ref[:, hd * LANE:(hd + 1) * LANE], kc_ref, every, CTX) for hd in heads]
    oc_ref[...] = normalised([consume(sa_ref, hd, vtc_ref, every, CTX, cmax_c[hd], init) for hd in heads])


def _attention(q, k, vt):
    assert (SEQ // ATT_TK) % 2 == 0 and ATT_LAT_TILES % 2 == 0 and CTX == ATT_TQ
    gw = ATT_GROUP * LANE
    gv = ATT_GROUP * ATTN_V
    gva = ATT_GROUP * ATTN_VA
    ctx_blk = N_LAT // CTX
    return pl.pallas_call(
        _attn_kernel,
        out_shape=(jax.ShapeDtypeStruct((N_LAT, N_Q_HEADS * ATTN_V), BF16),
                   jax.ShapeDtypeStruct((N_CTX, N_Q_HEADS * ATTN_V), BF16)),
        grid=(BATCH, N_Q_HEADS // ATT_GROUP),
        in_specs=[pl.BlockSpec((SEQ, gw), lambda b, g: (b, g)),
                  pl.BlockSpec((CTX, gw), lambda b, g: (ctx_blk + b, g)),
                  pl.BlockSpec((CTX, gw), lambda b, g: (ctx_blk + b, g)),
                  pl.BlockSpec((SEQ, gw), lambda b, g: (b, g)),
                  pl.BlockSpec((gva, CTX), lambda b, g: (g, ctx_blk + b)),
                  pl.BlockSpec((gva, SEQ), lambda b, g: (g, b))],
        out_specs=(pl.BlockSpec((SEQ, gv), lambda b, g: (b, g)), pl.BlockSpec((CTX, gv), lambda b, g: (b, g))),
        scratch_shapes=[pltpu.VMEM((ATT_GROUP, ATT_TK, ATT_TQ), F32)] * 2,
        compiler_params=_params(("parallel", "parallel")),
        name="attention",
    )(q, q, k, k, vt, vt)


def _mixer_epilogue(y, h, m_ref, g_ref):
    h_new = h + m_ref[2:3, :] * _rms(y, g_ref[1:2, :])
    u = _modulated(h_new, g_ref[2:3, :], m_ref[3:4, :], m_ref[4:5, :])
    return h_new, u


def _attn_out_kernel(ol_ref, oc_ref, w_ref, x_ref, c_ref, m_ref, g_ref, h_out_ref, u_ref):
    y = _dot(_input_rows(ol_ref, oc_ref), w_ref[...])
    h_new, u = _mixer_epilogue(y, _input_rows(x_ref, c_ref), m_ref, g_ref)
    h_out_ref[...] = h_new
    u_ref[...] = u.astype(BF16)


def _attn_output(o_lat, o_ctx, w_out, x, ctx, mods, g):
    assert N_Q_HEADS * ATTN_V == D
    tm = ROW_TILE
    row = lambda n: pl.BlockSpec((tm, n), lambda i: (i, 0))
    return pl.pallas_call(
        _attn_out_kernel,
        out_shape=(jax.ShapeDtypeStruct((N_ROWS, D), F32), jax.ShapeDtypeStruct((N_ROWS, D), BF16)),
        grid=(N_ROWS // tm,),
        in_specs=[*_input_specs(tm), _full((D, D)), *_input_specs(tm), _mod_spec(0, tm), _full((4, D))],
        out_specs=(row(D), row(D)),
        compiler_params=_params(("parallel",)),
        name="attn_output",
    )(o_lat, o_ctx, w_out, x, ctx, mods, g)


def _swiglu(x, wg_ref, wu_ref, wd_ref, f_dim, chunk):
    y = None
    for c0 in range(0, f_dim, chunk):
        cols = slice(c0, min(c0 + chunk, f_dim))
        a = _silu(_dot(x, wg_ref[:, cols])) * _dot(x, wu_ref[:, cols])
        part = _dot(a.astype(BF16), wd_ref[cols, :])
        y = part if y is None else y + part
    return y


def _ffn_kernel(x_ref, wg_ref, wu_ref, wd_ref, h_ref, m_ref, g_ref, o_ref):
    y = _swiglu(x_ref[...], wg_ref, wu_ref, wd_ref, FFN_DIM, MLP_CHUNK)
    o_ref[...] = h_ref[...] + m_ref[5:6, :] * _rms(y, g_ref[3:4, :])


def _resident(shape):
    return pl.BlockSpec(shape, lambda *_: (0,) * len(shape), pipeline_mode=pl.Buffered(1))


def _ffn(x, wg, wu, wd, h, mods, g):
    tm = MLP_ROW_TILE
    row = lambda n: pl.BlockSpec((tm, n), lambda i: (i, 0))
    return pl.pallas_call(
        _ffn_kernel,
        out_shape=jax.ShapeDtypeStruct((N_ROWS, D), F32),
        grid=(N_ROWS // tm,),
        in_specs=[row(D), _resident((D, FFN_DIM)), _resident((D, FFN_DIM)), _resident((FFN_DIM, D)),
                  row(D), _mod_spec(0, tm), _full((4, D))],
        out_specs=row(D),
        compiler_params=_params(("parallel",)),
        name="ffn_mlp",
    )(x, wg, wu, wd, h, mods, g)


N_ASSIGN = 2 * N_LAT
N_SORTED = N_ASSIGN + N_EXPERTS * MLP_ROW_TILE
N_SORTED_TILES = N_SORTED // MLP_ROW_TILE
ROUTE_TILE = 512
DMA_UNROLL = 8


def _row_copy(src_ref, src_row, dst_ref, dst_row, sem):
    return pltpu.make_async_copy(src_ref.at[pl.ds(src_row, 1), :], dst_ref.at[pl.ds(dst_row, 1), :], sem)


def _dispatch_kernel(d1_ref, d2_ref, u_ref, zeros_ref, xs_ref, sem):
    del zeros_ref

    def start(t, _):
        _row_copy(u_ref, t, xs_ref, d1_ref[0, t], sem).start()
        _row_copy(u_ref, t, xs_ref, d2_ref[0, t], sem).start()
        return 0

    def wait(t, _):
        _row_copy(u_ref, t, xs_ref, d1_ref[0, t], sem).wait()
        _row_copy(u_ref, t, xs_ref, d2_ref[0, t], sem).wait()
        return 0

    lax.fori_loop(0, ROUTE_TILE, start, 0, unroll=DMA_UNROLL)
    lax.fori_loop(0, ROUTE_TILE, wait, 0, unroll=DMA_UNROLL)


def _route_spec():
    return pl.BlockSpec((None, 1, ROUTE_TILE), lambda i: (i, 0, 0), memory_space=pltpu.SMEM)


def _dispatch(d1, d2, u):
    tm = ROUTE_TILE
    return pl.pallas_call(
        _dispatch_kernel,
        out_shape=jax.ShapeDtypeStruct((N_SORTED, D), F32),
        grid=(N_LAT // tm,),
        in_specs=[_route_spec(), _route_spec(), pl.BlockSpec((tm, D), lambda i: (i, 0)),
                  pl.BlockSpec(memory_space=pl.ANY)],
        out_specs=pl.BlockSpec(memory_space=pl.ANY),
        scratch_shapes=[pltpu.SemaphoreType.DMA(())],
        input_output_aliases={3: 0},
        compiler_params=_params(("arbitrary",)),
        name="moe_dispatch",
    )(d1, d2, u, jnp.zeros((N_SORTED, D), F32))


def _experts_kernel(te_ref, nt_ref, x_ref, wg_ref, wu_ref, wd_ref, o_ref, xb_ref):
    del te_ref
    i, f = pl.program_id(0), pl.program_id(1)
    live = i < nt_ref[0]

    @pl.when(live & (f == 0))
    def _():
        xb_ref[...] = x_ref[...].astype(BF16)

    @pl.when(live)
    def _():
        y = _swiglu(xb_ref[...], wg_ref, wu_ref, wd_ref, EXPERT_TILE, MLP_CHUNK)

        @pl.when(f == 0)
        def _():
            o_ref[...] = y

        @pl.when(f > 0)
        def _():
            o_ref[...] += y

    @pl.when(jnp.logical_not(live) & (f == 0))
    def _():
        o_ref[...] = jnp.zeros_like(o_ref)


def _experts(tile_expert, n_tiles, xs, wg, wu, wd):
    tm = MLP_ROW_TILE
    nf = EXPERT_DIM // EXPERT_TILE

    def f_blk(i, f, nt):
        return jnp.where(i < nt[0], f, nf - 1)

    grid_spec = pltpu.PrefetchScalarGridSpec(
        num_scalar_prefetch=2,
        grid=(N_SORTED_TILES, nf),
        in_specs=[pl.BlockSpec((tm, D), lambda i, f, te, nt: (jnp.minimum(i, nt[0] - 1), 0)),
                  pl.BlockSpec((None, D, EXPERT_TILE), lambda i, f, te, nt: (te[i], 0, f_blk(i, f, nt))),
                  pl.BlockSpec((None, D, EXPERT_TILE), lambda i, f, te, nt: (te[i], 0, f_blk(i, f, nt))),
                  pl.BlockSpec((None, EXPERT_TILE, D), lambda i, f, te, nt: (te[i], f_blk(i, f, nt), 0))],
        out_specs=pl.BlockSpec((tm, D), lambda i, f, te, nt: (i, 0)),
        scratch_shapes=[pltpu.VMEM((tm, D), BF16)],
    )
    return pl.pallas_call(
        _experts_kernel,
        out_shape=jax.ShapeDtypeStruct((N_SORTED, D), F32),
        grid_spec=grid_spec,
        compiler_params=_params(("arbitrary", "arbitrary")),
        name="moe_experts",
    )(tile_expert, n_tiles, xs, wg, wu, wd)


def _combine_kernel(d1_ref, d2_ref, n1_ref, n2_ref, w_ref, h_ref, m_ref, g_ref, ys_ref, o_ref, buf_ref, sems):
    i, n = pl.program_id(0), pl.num_programs(0)
    slot = i % 2

    def gather(a_ref, b_ref, slot, start):
        def body(t, _):
            for choice, d_ref in enumerate((a_ref, b_ref)):
                copy = _row_copy(ys_ref, d_ref[0, t], buf_ref.at[slot, choice], t, sems.at[slot])
                copy.start() if start else copy.wait()
            return 0

        lax.fori_loop(0, ROUTE_TILE, body, 0, unroll=DMA_UNROLL)

    @pl.when(i == 0)
    def _():
        gather(d1_ref, d2_ref, 0, True)

    @pl.when(i + 1 < n)
    def _():
        gather(n1_ref, n2_ref, 1 - slot, True)

    gather(d1_ref, d2_ref, slot, False)
    w = w_ref[...]
    lane = lax.broadcasted_iota(jnp.int32, w.shape, 1)
    w1 = jnp.sum(jnp.where(lane == 0, w, 0.0), axis=-1, keepdims=True)
    w2 = jnp.sum(jnp.where(lane == 1, w, 0.0), axis=-1, keepdims=True)
    y = w1 * buf_ref[slot, 0] + w2 * buf_ref[slot, 1]
    o_ref[...] = h_ref[...] + m_ref[5:6, :] * _rms(y, g_ref[3:4, :])


def _combine(d1, d2, wts, h, mods, g, ys):
    tm = ROUTE_TILE
    n = N_LAT // tm
    row = lambda w: pl.BlockSpec((tm, w), lambda i: (i, 0))
    nxt = pl.BlockSpec((None, 1, tm), lambda i: (jnp.minimum(i + 1, n - 1), 0, 0), memory_space=pltpu.SMEM)
    return pl.pallas_call(
        _combine_kernel,
        out_shape=jax.ShapeDtypeStruct((N_LAT, D), F32),
        grid=(n,),
        in_specs=[_route_spec(), _route_spec(), nxt, nxt, row(LANE), row(D), _mod_spec(1, tm), _full((4, D)),
                  pl.BlockSpec(memory_space=pl.ANY)],
        out_specs=row(D),
        scratch_shapes=[pltpu.VMEM((2, 2, tm, D), F32), pltpu.SemaphoreType.DMA((2,))],
        compiler_params=_params(("arbitrary",)),
        name="moe_combine",
    )(d1, d2, d1, d2, wts, h, mods, g, ys)


def _routing(route):
    tm = MLP_ROW_TILE
    e1, e2 = route[:, 0], route[:, 1]
    experts = jnp.arange(N_EXPERTS, dtype=jnp.int32)
    hit = (e1[:, None] == experts).astype(jnp.int32) + (e2[:, None] == experts).astype(jnp.int32)
    pos = jnp.cumsum(hit, axis=0) - hit
    padded = (jnp.sum(hit, axis=0) + tm - 1) // tm * tm
    ends = jnp.cumsum(padded)
    starts = ends - padded
    d1 = starts[e1] + jnp.take_along_axis(pos, e1[:, None], axis=1)[:, 0]
    d2 = starts[e2] + jnp.take_along_axis(pos, e2[:, None], axis=1)[:, 0]
    n_tiles = ends[-1] // tm
    tile_start = jnp.minimum(jnp.arange(N_SORTED_TILES, dtype=jnp.int32), n_tiles - 1) * tm
    tile_expert = jnp.sum((tile_start[:, None] >= ends[None, :]).astype(jnp.int32), axis=1)
    shape = (N_LAT // ROUTE_TILE, 1, ROUTE_TILE)
    return (d1.astype(jnp.int32).reshape(shape), d2.astype(jnp.int32).reshape(shape),
            tile_expert.astype(jnp.int32), n_tiles.astype(jnp.int32).reshape(1))


GLA_Q0, GLA_K0, GLA_V0, GLA_G0, GLA_R0 = 0, 512, 1024, 2048, 3072
GLA_IN_P = GLA_R0 + LANE


def _gla_proj_kernel(h_ref, m_ref, g_ref, w_in_ref, w_vt_ref, q_ref, k_ref, v_ref, vt_ref, sg_ref, r_ref):
    u = _modulated(h_ref[...], g_ref[0:1, :], m_ref[0:1, :], m_ref[1:2, :]).astype(BF16)
    z = _dot(u, w_in_ref[...])
    q_ref[...] = (z[:, GLA_Q0:GLA_K0] * (GLA_DK ** -0.5)).astype(BF16)
    k_ref[...] = z[:, GLA_K0:GLA_V0].astype(BF16)
    v_ref[...] = z[:, GLA_V0:GLA_G0].astype(BF16)
    vt_ref[...] = _dot_nt(w_vt_ref[...], u).astype(BF16)
    sg_ref[...] = _silu(z[:, GLA_G0:GLA_R0]).astype(BF16)
    r_ref[...] = z[:, GLA_R0:].astype(BF16)


def _gla_project(h, mods, g, w):
    tm = ROW_TILE
    row = lambda n, dt=None: pl.BlockSpec((tm, n), lambda i: (i, 0))
    hk, hv = GLA_HEADS * GLA_DK, GLA_HEADS * GLA_DV
    sds = jax.ShapeDtypeStruct
    return pl.pallas_call(
        _gla_proj_kernel,
        out_shape=(sds((N_ROWS, hk), BF16), sds((N_ROWS, hk), BF16), sds((N_ROWS, hv), BF16),
                   sds((hv, N_ROWS), BF16), sds((N_ROWS, hv), BF16), sds((N_ROWS, LANE), BF16)),
        grid=(N_ROWS // tm,),
        in_specs=[row(D), _mod_spec(1, tm), _full((4, D)), _full((D, GLA_IN_P)), _full((hv, D))],
        out_specs=(row(hk), row(hk), row(hv), pl.BlockSpec((hv, tm), lambda i: (0, i)), row(hv), row(LANE)),
        compiler_params=_params(("parallel",)),
        name="gla_project",
    )(h, mods, g, w["w_in"], w["w_vt"])


GLA_LAT_BLOCKS = SEQ // GLA_BLOCK
GLA_CTX_BLOCKS = CTX // GLA_BLOCK
GLA_STRIPS = GLA_BLOCK // GLA_SUB


def _gla_blocks(chains):
    n = GLA_BLOCK
    r = lax.broadcasted_iota(jnp.int32, (n, n), 0)
    c = lax.broadcasted_iota(jnp.int32, (n, n), 1)
    row = lax.broadcasted_iota(jnp.int32, (n, GLA_DK), 0)
    keep = {rev: (c >= r) if rev else (c <= r) for rev in (False, True)}
    tri = {rev: jnp.where(keep[rev], 1.0, 0.0).astype(BF16) for rev in (False, True)}

    cums = []
    for ch in chains:
        parts = _dot(tri[ch["reverse"]], jnp.concatenate(_split_bf16(ch["la"]), axis=1))
        cums.append(parts[:, :GLA_DK] + parts[:, GLA_DK:])

    states, outs = [], []
    for ch, cum in zip(chains, cums):
        total = cum[0:1, :] if ch["reverse"] else cum[n - 1:n, :]
        k_state = (ch["k"] * jnp.exp(total - cum)).astype(BF16)
        states.append(ch["state"] * jnp.exp(total) + _dot(ch["vt"], k_state))
        if ch["q"] is None:
            outs.append(None)
        else:
            outs.append(_dot_nt((ch["q"] * jnp.exp(cum)).astype(BF16), ch["state"].astype(BF16)))

    strips = [[] for _ in chains]
    for i in range(GLA_STRIPS):
        lo, hi = i * GLA_SUB, (i + 1) * GLA_SUB
        for ci, (ch, cum) in enumerate(zip(chains, cums)):
            if ch["q"] is None:
                continue
            if ch["reverse"]:
                ref = cum[hi:hi + 1, :] if i < GLA_STRIPS - 1 else jnp.zeros((1, GLA_DK), F32)
                live = row >= lo
            else:
                ref = cum[lo - 1:lo, :] if i > 0 else jnp.zeros((1, GLA_DK), F32)
                live = row < hi
            q_loc = (ch["q"][lo:hi, :] * jnp.exp(cum[lo:hi, :] - ref)).astype(BF16)
            k_loc = jnp.where(live, ch["k"] * jnp.exp(ref - cum), 0.0).astype(BF16)
            strips[ci].append(_dot_nt(q_loc, k_loc))

    for ci, ch in enumerate(chains):
        if ch["q"] is not None:
            scores = jnp.where(keep[ch["reverse"]], jnp.concatenate(strips[ci], axis=0), 0.0).astype(BF16)
            outs[ci] = outs[ci] + _dot(scores, ch["v"])
    return list(zip(outs, states))


GLA_GROUP = 2


def _gla_kernel(q_ref, kl_ref, kc_ref, vl_ref, vtl_ref, vtc_ref, rl_ref, rc_ref,
                wgf_ref, wgb_ref, bgf_ref, bgb_ref, o_ref, s_ref):
    n = GLA_BLOCK
    s_ref[...] = jnp.zeros_like(s_ref)
    chains = [(hd, rev) for hd in range(GLA_GROUP) for rev in (False, True)]

    def log_decays(r_blk, reverse):
        wg_ref, bg_ref = (wgb_ref, bgb_ref) if reverse else (wgf_ref, bgf_ref)
        zg = _dot(r_blk, wg_ref[...]) + bg_ref[...]
        return (jnp.minimum(zg, 0.0) - jnp.log(1.0 + jnp.exp(-jnp.abs(zg)))) * (1.0 / GLA_GATE_NORM)

    def chain(hd, reverse, rows, la, k_ref, vt_ref, with_output):
        ks = slice(hd * GLA_DK, (hd + 1) * GLA_DK)
        vs = slice(hd * GLA_DV, (hd + 1) * GLA_DV)
        return dict(reverse=reverse, k=k_ref[rows, ks].astype(F32), vt=vt_ref[vs, rows],
                    la=la[:, ks], state=s_ref[int(reverse), hd],
                    q=q_ref[rows, ks].astype(F32) if with_output else None,
                    v=vl_ref[rows, vs] if with_output else None)

    for j in range(GLA_CTX_BLOCKS):
        rows = {rev: slice(blk * n, (blk + 1) * n) for rev, blk in ((False, j), (True, GLA_CTX_BLOCKS - 1 - j))}
        la = {rev: log_decays(rc_ref[rows[rev], :], rev) for rev in (False, True)}
        work = [chain(hd, rev, rows[rev], la[rev], kc_ref, vtc_ref, False) for hd, rev in chains]
        for (hd, reverse), (_, s_new) in zip(chains, _gla_blocks(work)):
            s_ref[int(reverse), hd] = s_new

    def step(j, accumulate):
        rows = {rev: pl.ds(pl.multiple_of(blk * n, n), n) for rev, blk in ((False, j), (True, GLA_LAT_BLOCKS - 1 - j))}
        la = {rev: log_decays(rl_ref[rows[rev], :], rev) for rev in (False, True)}
        work = [chain(hd, rev, rows[rev], la[rev], kl_ref, vtl_ref, True) for hd, rev in chains]
        where = [(rows[rev], slice(hd * GLA_DV, (hd + 1) * GLA_DV)) for hd, rev in chains]
        for (hd, reverse), (rows, vs), (o, s_new) in zip(chains, where, _gla_blocks(work)):
            s_ref[int(reverse), hd] = s_new
            if accumulate:
                o_ref[rows, vs] += o
            else:
                o_ref[rows, vs] = o

    half = GLA_LAT_BLOCKS // 2

    def first(j, _):
        step(j, False)
        return 0

    def second(j, _):
        step(j, True)
        return 0

    lax.fori_loop(0, half, first, 0)
    lax.fori_loop(half, GLA_LAT_BLOCKS, second, 0)


def _gla_scan(q, k, v, vt, r, w_gate_f, w_gate_b, b_gate_f, b_gate_b):
    gk, gv = GLA_GROUP * GLA_DK, GLA_GROUP * GLA_DV
    lat = lambda n: pl.BlockSpec((SEQ, n), lambda b, g: (b, g))
    ctx = lambda n: pl.BlockSpec((CTX, n), lambda b, g: (N_LAT // CTX + b, g))
    grp = lambda rows: pl.BlockSpec((rows, gk), lambda b, g: (0, g))
    return pl.pallas_call(
        _gla_kernel,
        out_shape=jax.ShapeDtypeStruct((N_LAT, GLA_HEADS * GLA_DV), F32),
        grid=(BATCH, GLA_HEADS // GLA_GROUP),
        in_specs=[lat(gk), lat(gk), ctx(gk), lat(gv),
                  pl.BlockSpec((gv, SEQ), lambda b, g: (g, b)),
                  pl.BlockSpec((gv, CTX), lambda b, g: (g, N_LAT // CTX + b)),
                  pl.BlockSpec((SEQ, LANE), lambda b, g: (b, 0)),
                  pl.BlockSpec((CTX, LANE), lambda b, g: (N_LAT // CTX + b, 0)),
                  grp(LANE), grp(LANE), grp(1), grp(1)],
        out_specs=lat(gv),
        scratch_shapes=[pltpu.VMEM((2, GLA_GROUP, GLA_DV, GLA_DK), F32)],
        compiler_params=_params(("parallel", "parallel")),
        name="gla_scan",
    )(q, k, k, v, vt, vt, r, r, w_gate_f, w_gate_b, b_gate_f, b_gate_b)


def _gla_out_kernel(o_ref, sg_ref, on_ref, w_ref, h_ref, m_ref, g_ref, wr_hi_ref, wr_lo_ref, br_ref,
                    h_out_ref, u_ref, route_ref, wts_ref):
    parts = []
    for hd in range(GLA_HEADS):
        sl = slice(hd * GLA_DV, (hd + 1) * GLA_DV)
        parts.append(_rms(o_ref[:, sl], on_ref[...]) * sg_ref[:, sl].astype(F32))
    y = _dot(jnp.concatenate(parts, axis=-1).astype(BF16), w_ref[...])
    h_new, u = _mixer_epilogue(y, h_ref[...], m_ref, g_ref)
    h_out_ref[...] = h_new
    u_ref[...] = u
    logits = _dot3(u, wr_hi_ref[...], wr_lo_ref[...]) + br_ref[...]
    lane = lax.broadcasted_iota(jnp.int32, logits.shape, 1)
    l1 = jnp.max(logits, axis=-1, keepdims=True)
    i1 = jnp.min(jnp.where(logits == l1, lane, LANE), axis=-1, keepdims=True)
    rest = jnp.where(lane == i1, -jnp.inf, logits)
    l2 = jnp.max(rest, axis=-1, keepdims=True)
    i2 = jnp.min(jnp.where(rest == l2, lane, LANE), axis=-1, keepdims=True)
    e2 = jnp.exp(l2 - l1)
    route_ref[...] = jnp.where(lane == 0, i1, jnp.where(lane == 1, i2, 0))
    wts_ref[...] = jnp.where(lane == 0, 1.0 / (1.0 + e2), jnp.where(lane == 1, e2 / (1.0 + e2), 0.0))


def _gla_output(o, sg, o_norm, w_out, h, mods, g, wr_hi, wr_lo, br):
    tm = ROW_TILE
    row = lambda n: pl.BlockSpec((tm, n), lambda i: (i, 0))
    hv = GLA_HEADS * GLA_DV
    return pl.pallas_call(
        _gla_out_kernel,
        out_shape=(jax.ShapeDtypeStruct((N_LAT, D), F32), jax.ShapeDtypeStruct((N_LAT, D), F32),
                   jax.ShapeDtypeStruct((N_LAT, LANE), jnp.int32), jax.ShapeDtypeStruct((N_LAT, LANE), F32)),
        grid=(N_LAT // tm,),
        in_specs=[row(hv), row(hv), _full((1, GLA_DV)), _full((hv, D)), row(D), _mod_spec(1, tm),
                  _full((4, D)), _full((D, LANE)), _full((D, LANE)), _full((1, LANE))],
        out_specs=(row(D), row(D), row(LANE), row(LANE)),
        compiler_params=_params(("parallel",)),
        name="gla_output",
    )(o, sg, o_norm, w_out, h, mods, g, wr_hi, wr_lo, br)


def _slots(w, n_heads, width):
    k = w.shape[0]
    w = w.reshape(k, n_heads, width)
    return jnp.pad(w, ((0, 0), (0, 0), (0, LANE - width))).reshape(k, n_heads * LANE)


def _attn_weights(w_in, q_norm, w_uq, kv_norm, w_ukv, qk_norm, w_out):
    c = 0
    cols = {}
    for name, n in (("q_lat", MLA_Q_RANK), ("kv_lat", MLA_KV_RANK), ("k_rope", MLA_ROPE),
                    ("q_b", GQA_HEADS * GQA_HEAD_DIM), ("k_b", GQA_KV_HEADS * GQA_HEAD_DIM),
                    ("v_b", GQA_KV_HEADS * GQA_HEAD_DIM)):
        cols[name] = w_in[:, c:c + n]
        c += n
    k_rope = jnp.pad(cols["k_rope"], ((0, 0), (MLA_NOPE, LANE - MLA_NOPE - MLA_ROPE)))
    w_in_p = jnp.concatenate([cols["q_lat"], cols["kv_lat"], k_rope,
                              _slots(cols["q_b"], GQA_HEADS, GQA_HEAD_DIM),
                              _slots(cols["k_b"], GQA_KV_HEADS, GQA_HEAD_DIM)], axis=1)
    ukv = w_ukv.reshape(MLA_KV_RANK, MLA_HEADS, MLA_NOPE + MLA_V)
    w_uk = _slots(ukv[:, :, :MLA_NOPE].reshape(MLA_KV_RANK, -1), MLA_HEADS, MLA_NOPE)
    w_uv = ukv[:, :, MLA_NOPE:].reshape(MLA_KV_RANK, MLA_HEADS * MLA_V)
    return {
        "w_in": w_in_p.astype(BF16),
        "q_norm": q_norm.reshape(1, -1),
        "w_uq": _slots(w_uq, MLA_HEADS, MLA_NOPE + MLA_ROPE).astype(BF16),
        "kv_norm": kv_norm.reshape(1, -1),
        "w_uk": w_uk.astype(BF16),
        "w_uvt": w_uv.T.astype(BF16),
        "w_vbt": cols["v_b"].T.astype(BF16),
        "qk_norm": jnp.pad(qk_norm, ((0, 0), (0, LANE - GQA_HEAD_DIM))),
        "w_out": w_out.astype(BF16),
    }


def _axial_tables(d):
    half, p = d // 2, d // 4
    t = jnp.arange(SEQ, dtype=jnp.int32)
    freqs = ROPE_THETA ** (-jnp.arange(0, half, 2, dtype=F32) / half)
    cos, sa, sb = [], [], []
    for pos in (t // GRID_W, t % GRID_W):
        ang = pos.astype(F32)[:, None] * freqs[None, :]
        c, s, z = jnp.cos(ang), jnp.sin(ang), jnp.zeros_like(ang)
        cos += [c, c]
        sa += [-s, z]
        sb += [z, s]
    return [jnp.concatenate(x, axis=-1) for x in (cos, sa, sb)]


def _rope_tables():
    def embed(tbl, left, fill):
        right = LANE - left - tbl.shape[1]
        tbl = jnp.pad(tbl, ((0, 0), (left, right)), constant_values=fill)
        return jnp.pad(tbl, ((0, ATTN_PROJ_TILE), (0, 0)), constant_values=fill)

    out = []
    for d, left in ((MLA_ROPE, MLA_NOPE), (GQA_HEAD_DIM, 0)):
        cos, sa, sb = _axial_tables(d)
        out += [embed(cos, left, 1.0), embed(sa, left, 0.0), embed(sb, left, 0.0)]
    return out


def _gla_weights(w_in, w_gate2, b_gate):
    hk, hv = GLA_HEADS * GLA_DK, GLA_HEADS * GLA_DV
    r = jnp.pad(w_in[:, 2 * hk + 2 * hv:], ((0, 0), (0, LANE - 2 * GLA_GATE_RANK)))
    w_in_p = jnp.concatenate([w_in[:, :2 * hk + 2 * hv], r], axis=1)
    pad_f = ((0, LANE - GLA_GATE_RANK), (0, 0))
    pad_b = ((GLA_GATE_RANK, LANE - 2 * GLA_GATE_RANK), (0, 0))
    return {
        "w_in": w_in_p.astype(BF16),
        "w_vt": w_in[:, 2 * hk:2 * hk + hv].T.astype(BF16),
        "w_gate_f": jnp.pad(w_gate2[0], pad_f).astype(BF16),
        "w_gate_b": jnp.pad(w_gate2[1], pad_b).astype(BF16),
        "b_gate_f": b_gate[0].reshape(1, hk),
        "b_gate_b": b_gate[1].reshape(1, hk),
    }


def kernel(x, c, ctx, c_ctx, mod_w, mod_b, norm_g, attn_w_in, attn_q_norm, attn_w_uq, attn_kv_norm, attn_w_ukv,
           attn_qk_norm, attn_w_out, gla_w_in, gla_w_gate2, gla_b_gate, gla_o_norm, gla_w_out, ffn_w_gate,
           ffn_w_up, ffn_w_down, moe_w_router, moe_b_router, moe_w_gate, moe_w_up, moe_w_down):
    assert x.shape == (BATCH, SEQ, D) and ctx.shape == (BATCH, CTX, D)
    x, ctx = x.reshape(N_LAT, D), ctx.reshape(N_CTX, D)
    cc = jnp.concatenate([c, c_ctx[None, :], jnp.zeros((MOD_ROWS - BATCH - 1, D), F32)], axis=0)
    mods = _mod_vectors(cc, mod_w, mod_b).reshape(mod_w.shape[0], MOD_ROWS, 6, D)

    aw = _attn_weights(attn_w_in[0], attn_q_norm[0], attn_w_uq[0], attn_kv_norm[0], attn_w_ukv[0],
                       attn_qk_norm[0], attn_w_out[0])
    q, k, vt = _attn_project(x, ctx, mods, norm_g[0], aw, _rope_tables())
    o_lat, o_ctx = _attention(q, k, vt)
    h, u = _attn_output(o_lat, o_ctx, aw["w_out"], x, ctx, mods, norm_g[0])
    h = _ffn(u, ffn_w_gate[0].astype(BF16), ffn_w_up[0].astype(BF16), ffn_w_down[0].astype(BF16), h, mods,
             norm_g[0])

    gw = _gla_weights(gla_w_in[0], gla_w_gate2[0], gla_b_gate[0])
    gq, gk, gv, gvt, sg, gr = _gla_project(h, mods, norm_g[1], gw)
    go = _gla_scan(gq, gk, gv, gvt, gr, gw["w_gate_f"], gw["w_gate_b"], gw["b_gate_f"], gw["b_gate_b"])
    wr = jnp.pad(moe_w_router[0], ((0, 0), (0, LANE - N_EXPERTS)))
    wr_hi = wr.astype(BF16)
    wr_lo = (wr - wr_hi.astype(F32)).astype(BF16)
    br = jnp.pad(moe_b_router[0], (0, LANE - N_EXPERTS), constant_values=-jnp.inf).reshape(1, LANE)
    h, u, route, wts = _gla_output(go, sg, gla_o_norm[0].reshape(1, GLA_DV), gla_w_out[0].astype(BF16), h,
                                   mods, norm_g[1], wr_hi, wr_lo, br)
    d1, d2, tile_expert, n_tiles = _routing(route)
    xs = _dispatch(d1, d2, u)
    ys = _experts(tile_expert, n_tiles, xs, moe_w_gate[0].astype(BF16), moe_w_up[0].astype(BF16),
                  moe_w_down[0].astype(BF16))
    h = _combine(d1, d2, wts, h, mods, norm_g[1], ys)
    return h.reshape(BATCH, SEQ, D)
```

```python
import jax
import jax.numpy as jnp
from jax import lax
from jax.experimental import pallas as pl
from jax.experimental.pallas import tpu as pltpu

F32 = jnp.float32
BF16 = jnp.bfloat16

D = 1024
BATCH = 4
SEQ = 4096
CTX = 256
GRID_W = 64
ROPE_THETA = 10000.0
EPS = 1e-6

N_LAT = BATCH * SEQ
N_CTX = BATCH * CTX
N_ROWS = N_LAT + N_CTX
MOD_ROWS = 8
CTX_MOD_ROW = BATCH

LANE = 128
ROW_TILE = 512
ATTN_PROJ_TILE = 256

MLA_HEADS = 8
MLA_Q_RANK = 384
MLA_KV_RANK = 256
MLA_NOPE = 64
MLA_ROPE = 32
MLA_V = 64
GQA_HEADS = 8
GQA_KV_HEADS = 2
GQA_GROUP = GQA_HEADS // GQA_KV_HEADS
GQA_HEAD_DIM = 64
N_Q_HEADS = MLA_HEADS + GQA_HEADS

GLA_HEADS = 4
GLA_DK = 128
GLA_DV = 256
GLA_GATE_RANK = 16
GLA_GATE_NORM = 16.0
GLA_BLOCK = 128
GLA_SUB = 32

MXU_WIDTH = 256
FFN_DIM = 2816
N_EXPERTS = 8
EXPERT_DIM = 3584
EXPERT_TILE = EXPERT_DIM // 2
MLP_CHUNK = 2 * MXU_WIDTH
MLP_ROW_TILE = 512

VMEM_LIMIT = 56 * 1024 * 1024


def _params(sem):
    return pltpu.CompilerParams(dimension_semantics=sem, vmem_limit_bytes=VMEM_LIMIT)


def _rms(x, g):
    return x * lax.rsqrt(jnp.mean(x * x, axis=-1, keepdims=True) + EPS) * g


def _silu(x):
    return x / (1.0 + jnp.exp(-x))


def _split_bf16(x):
    hi = x.astype(BF16)
    lo = (x - hi.astype(F32)).astype(BF16)
    return hi, lo


def _dot(a, b):
    return jnp.dot(a, b, preferred_element_type=F32)


def _dot_nt(a, b):
    return lax.dot_general(a, b, (((1,), (1,)), ((), ())), preferred_element_type=F32)


def _dot3(a, b_hi, b_lo):
    a_hi, a_lo = _split_bf16(a)
    return _dot(a_hi, b_hi) + (_dot(a_hi, b_lo) + _dot(a_lo, b_hi))


def _modulated(h, g_row, shift, scale):
    return _rms(h, g_row) * (1.0 + scale) + shift


def _rope(x, cos, sa, sb, p):
    return x * cos + pltpu.roll(x, LANE - p, 1) * sa + pltpu.roll(x, p, 1) * sb


def _mod_row(i, tile):
    r0 = i * tile
    return jnp.where(r0 < N_LAT, r0 // SEQ, CTX_MOD_ROW)


def _mod_spec(layer, tile):
    return pl.BlockSpec((None, None, 6, D), lambda i, *_: (layer, _mod_row(i, tile), 0, 0))


def _rope_block(i):
    r0 = i * ATTN_PROJ_TILE
    return jnp.where(r0 < N_LAT, (r0 % SEQ) // ATTN_PROJ_TILE, SEQ // ATTN_PROJ_TILE)


def _full(shape):
    return pl.BlockSpec(shape, lambda *_: (0,) * len(shape))


def _mod_kernel(c_ref, w_ref, b_ref, o_ref):
    w_hi, w_lo = _split_bf16(w_ref[...])
    o_ref[...] = _dot3(_silu(c_ref[...]), w_hi, w_lo) + b_ref[...]


def _mod_vectors(cc, mod_w, mod_b):
    depth, _, n = mod_w.shape
    tn = 1536
    return pl.pallas_call(
        _mod_kernel,
        out_shape=jax.ShapeDtypeStruct((depth, MOD_ROWS, n), F32),
        grid=(depth, n // tn),
        in_specs=[
            pl.BlockSpec((MOD_ROWS, D), lambda l, j: (0, 0)),
            pl.BlockSpec((None, D, tn), lambda l, j: (l, 0, j)),
            pl.BlockSpec((None, 1, tn), lambda l, j: (l, 0, j)),
        ],
        out_specs=pl.BlockSpec((None, MOD_ROWS, tn), lambda l, j: (l, 0, j)),
        compiler_params=_params(("parallel", "parallel")),
        name="mod_vectors",
    )(cc, mod_w, mod_b.reshape(depth, 1, n))


Q_LAT0, KV_LAT0, K_ROPE0 = 0, 384, 640
Q_B0 = 768
K_B0 = Q_B0 + GQA_HEADS * LANE
ATTN_IN_P = K_B0 + GQA_KV_HEADS * LANE
ATTN_V = 64
ATTN_VA = ATTN_V + 16
LOG2E = 1.4426950408889634


def _input_rows(x_ref, c_ref):
    return jnp.where(pl.program_id(0) < N_LAT // x_ref.shape[0], x_ref[...], c_ref[...])


def _input_specs(tm):
    n_lat = N_LAT // tm
    return [pl.BlockSpec((tm, D), lambda i: (jnp.minimum(i, n_lat - 1), 0)),
            pl.BlockSpec((tm, D), lambda i: (jnp.maximum(i - n_lat, 0), 0))]


def _attn_proj_kernel(x_ref, c_ref, m_ref, g_ref, w_in_ref, qn_ref, w_uq_ref, kvn_ref, w_uk_ref, w_uvt_ref,
                      w_vbt_ref, qkn_ref, ca_ref, saa_ref, sba_ref, cb_ref, sab_ref, sbb_ref,
                      q_ref, k_ref, vt_ref):
    u = _modulated(_input_rows(x_ref, c_ref), g_ref[0:1, :], m_ref[0:1, :], m_ref[1:2, :]).astype(BF16)
    z = _dot(u, w_in_ref[...])
    q_lat = _rms(z[:, Q_LAT0:Q_LAT0 + MLA_Q_RANK], qn_ref[...]).astype(BF16)
    kv_lat = _rms(z[:, KV_LAT0:KV_LAT0 + MLA_KV_RANK], kvn_ref[...]).astype(BF16)
    q_a = _dot(q_lat, w_uq_ref[...])
    k_a = _dot(kv_lat, w_uk_ref[...])
    ca, saa, sba = ca_ref[...], saa_ref[...], sba_ref[...]
    cb, sab, sbb = cb_ref[...], sab_ref[...], sbb_ref[...]
    pa, pb = MLA_ROPE // 4, GQA_HEAD_DIM // 4
    scale_a = (MLA_NOPE + MLA_ROPE) ** -0.5 * LOG2E
    scale_b = GQA_HEAD_DIM ** -0.5 * LOG2E
    k_rope = _rope(z[:, K_ROPE0:K_ROPE0 + LANE], ca, saa, sba, pa)
    for hd in range(MLA_HEADS):
        sl = slice(hd * LANE, (hd + 1) * LANE)
        q_ref[:, sl] = (_rope(q_a[:, sl], ca, saa, sba, pa) * scale_a).astype(BF16)
        k_ref[:, sl] = (k_a[:, sl] + k_rope).astype(BF16)

    def head_norm(x, gain):
        ms = jnp.sum(x * x, axis=-1, keepdims=True) * (1.0 / GQA_HEAD_DIM)
        return x * lax.rsqrt(ms + EPS) * gain

    for hd in range(GQA_HEADS):
        x = head_norm(z[:, Q_B0 + hd * LANE:Q_B0 + (hd + 1) * LANE], qkn_ref[0:1, :])
        sl = slice((MLA_HEADS + hd) * LANE, (MLA_HEADS + hd + 1) * LANE)
        q_ref[:, sl] = (_rope(x, cb, sab, sbb, pb) * scale_b).astype(BF16)
    for kv in range(GQA_KV_HEADS):
        x = head_norm(z[:, K_B0 + kv * LANE:K_B0 + (kv + 1) * LANE], qkn_ref[1:2, :])
        x = _rope(x, cb, sab, sbb, pb).astype(BF16)
        for hd in range(kv * GQA_GROUP, (kv + 1) * GQA_GROUP):
            k_ref[:, (MLA_HEADS + hd) * LANE:(MLA_HEADS + hd + 1) * LANE] = x

    vat = _dot_nt(w_uvt_ref[...], kv_lat).astype(BF16)
    vbt = _dot_nt(w_vbt_ref[...], u).astype(BF16)
    ones = jnp.ones((ATTN_VA - ATTN_V, vat.shape[1]), BF16)
    for hd in range(N_Q_HEADS):
        if hd < MLA_HEADS:
            v_t = vat[hd * ATTN_V:(hd + 1) * ATTN_V, :]
        else:
            kv = (hd - MLA_HEADS) // GQA_GROUP
            v_t = vbt[kv * ATTN_V:(kv + 1) * ATTN_V, :]
        vt_ref[hd * ATTN_VA:hd * ATTN_VA + ATTN_V, :] = v_t
        vt_ref[hd * ATTN_VA + ATTN_V:(hd + 1) * ATTN_VA, :] = ones


def _attn_project(x, ctx, mods, g, w, tables):
    tm = ATTN_PROJ_TILE
    row = lambda n: pl.BlockSpec((tm, n), lambda i: (i, 0))
    tab = pl.BlockSpec((tm, LANE), lambda i: (_rope_block(i), 0))
    return pl.pallas_call(
        _attn_proj_kernel,
        out_shape=(jax.ShapeDtypeStruct((N_ROWS, N_Q_HEADS * LANE), BF16),
                   jax.ShapeDtypeStruct((N_ROWS, N_Q_HEADS * LANE), BF16),
                   jax.ShapeDtypeStruct((N_Q_HEADS * ATTN_VA, N_ROWS), BF16)),
        grid=(N_ROWS // tm,),
        in_specs=[*_input_specs(tm), _mod_spec(0, tm), _full((4, D)), _full((D, ATTN_IN_P)),
                  _full((1, MLA_Q_RANK)), _full((MLA_Q_RANK, MLA_HEADS * LANE)),
                  _full((1, MLA_KV_RANK)), _full((MLA_KV_RANK, MLA_HEADS * LANE)),
                  _full((MLA_HEADS * ATTN_V, MLA_KV_RANK)), _full((GQA_KV_HEADS * ATTN_V, D)),
                  _full((2, LANE)), tab, tab, tab, tab, tab, tab],
        out_specs=(row(N_Q_HEADS * LANE), row(N_Q_HEADS * LANE),
                   pl.BlockSpec((N_Q_HEADS * ATTN_VA, tm), lambda i: (0, i))),
        compiler_params=_params(("parallel",)),
        name="attn_project",
    )(x, ctx, mods, g, w["w_in"], w["q_norm"], w["w_uq"], w["kv_norm"], w["w_uk"], w["w_uvt"], w["w_vbt"],
      w["qk_norm"], *tables)


ATT_TQ = 256
ATT_TK = 1024
ATT_LAT_TILES = SEQ // ATT_TQ
ATT_GROUP = 4


def _attn_kernel(ql_ref, qc_ref, kc_ref, kl_ref, vtc_ref, vtl_ref, ol_ref, oc_ref, sa_ref, sb_ref):
    heads = range(ATT_GROUP)
    n_lat = SEQ // ATT_TK
    every = slice(None)
    init = (jnp.full((1, ATT_TQ), -jnp.inf, F32), jnp.zeros((ATTN_VA, ATT_TQ), F32))

    def lat_queries(t, hd):
        return ql_ref[pl.ds(pl.multiple_of(t * ATT_TQ, ATT_TQ), ATT_TQ), hd * LANE:(hd + 1) * LANE]

    def lat_keys(j):
        return slice(j * ATT_TK, (j + 1) * ATT_TK)

    def scores(s_ref, hd, q, k_ref, keys, n_keys):
        s = _dot_nt(k_ref[keys, hd * LANE:(hd + 1) * LANE], q)
        s_ref[hd, 0:n_keys, :] = s
        return jnp.max(s, axis=0, keepdims=True)

    def consume(s_ref, hd, vt_ref, keys, n_keys, cmax, carry):
        m, acc = carry
        m_new = jnp.maximum(m, cmax)
        p = jnp.exp2(s_ref[hd, 0:n_keys, :] - m_new).astype(BF16)
        acc = jnp.exp2(m - m_new) * acc + _dot(vt_ref[hd * ATTN_VA:(hd + 1) * ATTN_VA, keys], p)
        return m_new, acc

    def normalised(carries):
        o_t = jnp.concatenate([acc[0:ATTN_V, :] / acc[ATTN_V:ATTN_V + 1, :] for _, acc in carries], axis=0)
        return o_t.T.astype(BF16)

    def query_tile(t, t_next, first_ref, second_ref, cmax):
        bufs = (first_ref, second_ref)
        carries = (init,) * ATT_GROUP
        for c in range(n_lat + 1):
            cur, nxt = bufs[c % 2], bufs[(c + 1) % 2]
            cmax_next, out = [], []
            for hd in heads:
                if c + 1 < n_lat:
                    cmax_next.append(scores(nxt, hd, lat_queries(t, hd), kl_ref, lat_keys(c + 1), ATT_TK))
                elif c + 1 == n_lat:
                    cmax_next.append(scores(nxt, hd, lat_queries(t, hd), kc_ref, every, CTX))
                else:
                    cmax_next.append(scores(nxt, hd, lat_queries(t_next, hd), kl_ref, lat_keys(0), ATT_TK))
                if c < n_lat:
                    out.append(consume(cur, hd, vtl_ref, lat_keys(c), ATT_TK, cmax[hd], carries[hd]))
                else:
                    out.append(consume(cur, hd, vtc_ref, every, CTX, cmax[hd], carries[hd]))
            cmax, carries = cmax_next, out
        ol_ref[pl.ds(pl.multiple_of(t * ATT_TQ, ATT_TQ), ATT_TQ), :] = normalised(carries)
        return tuple(cmax)

    def tile_pair(i, cmax):
        cmax = query_tile(2 * i, 2 * i + 1, sa_ref, sb_ref, cmax)
        return query_tile(2 * i + 1, jnp.minimum(2 * i + 2, ATT_LAT_TILES - 1), sb_ref, sa_ref, cmax)

    cmax0 = tuple(scores(sa_ref, hd, lat_queries(0, hd), kl_ref, lat_keys(0), ATT_TK) for hd in heads)
    lax.fori_loop(0, ATT_LAT_TILES // 2, tile_pair, cmax0)

    cmax_c = [scores(sa_ref, hd, qc_ref[:, hd * LANE:(hd + 1) * LANE], kc_ref, every, CTX) for hd in heads]
    oc_ref[...] = normalised([consume(sa_ref, hd, vtc_ref, every, CTX, cmax_c[hd], init) for hd in heads])


def _attention(q, k, vt):
    assert (SEQ // ATT_TK) % 2 == 0 and ATT_LAT_TILES % 2 == 0 and CTX == ATT_TQ
    gw = ATT_GROUP * LANE
    gv = ATT_GROUP * ATTN_V
    gva = ATT_GROUP * ATTN_VA
    ctx_blk = N_LAT // CTX
    return pl.pallas_call(
        _attn_kernel,
        out_shape=(jax.ShapeDtypeStruct((N_LAT, N_Q_HEADS * ATTN_V), BF16),
                   jax.ShapeDtypeStruct((N_CTX, N_Q_HEADS * ATTN_V), BF16)),
        grid=(BATCH, N_Q_HEADS // ATT_GROUP),
        in_specs=[pl.BlockSpec((SEQ, gw), lambda b, g: (b, g)),
                  pl.BlockSpec((CTX, gw), lambda b, g: (ctx_blk + b, g)),
                  pl.BlockSpec((CTX, gw), lambda b, g: (ctx_blk + b, g)),
                  pl.BlockSpec((SEQ, gw), lambda b, g: (b, g)),
                  pl.BlockSpec((gva, CTX), lambda b, g: (g, ctx_blk + b)),
                  pl.BlockSpec((gva, SEQ), lambda b, g: (g, b))],
        out_specs=(pl.BlockSpec((SEQ, gv), lambda b, g: (b, g)), pl.BlockSpec((CTX, gv), lambda b, g: (b, g))),
        scratch_shapes=[pltpu.VMEM((ATT_GROUP, ATT_TK, ATT_TQ), F32)] * 2,
        compiler_params=_params(("parallel", "parallel")),
        name="attention",
    )(q, q, k, k, vt, vt)


def _mixer_epilogue(y, h, m_ref, g_ref):
    h_new = h + m_ref[2:3, :] * _rms(y, g_ref[1:2, :])
    u = _modulated(h_new, g_ref[2:3, :], m_ref[3:4, :], m_ref[4:5, :])
    return h_new, u


def _attn_out_kernel(ol_ref, oc_ref, w_ref, x_ref, c_ref, m_ref, g_ref, h_out_ref, u_ref):
    y = _dot(_input_rows(ol_ref, oc_ref), w_ref[...])
    h_new, u = _mixer_epilogue(y, _input_rows(x_ref, c_ref), m_ref, g_ref)
    h_out_ref[...] = h_new
    u_ref[...] = u.astype(BF16)


def _attn_output(o_lat, o_ctx, w_out, x, ctx, mods, g):
    assert N_Q_HEADS * ATTN_V == D
    tm = ROW_TILE
    row = lambda n: pl.BlockSpec((tm, n), lambda i: (i, 0))
    return pl.pallas_call(
        _attn_out_kernel,
        out_shape=(jax.ShapeDtypeStruct((N_ROWS, D), F32), jax.ShapeDtypeStruct((N_ROWS, D), BF16)),
        grid=(N_ROWS // tm,),
        in_specs=[*_input_specs(tm), _full((D, D)), *_input_specs(tm), _mod_spec(0, tm), _full((4, D))],
        out_specs=(row(D), row(D)),
        compiler_params=_params(("parallel",)),
        name="attn_output",
    )(o_lat, o_ctx, w_out, x, ctx, mods, g)


def _swiglu(x, wg_ref, wu_ref, wd_ref, f_dim, chunk):
    y = None
    for c0 in range(0, f_dim, chunk):
        cols = slice(c0, min(c0 + chunk, f_dim))
        a = _silu(_dot(x, wg_ref[:, cols])) * _dot(x, wu_ref[:, cols])
        part = _dot(a.astype(BF16), wd_ref[cols, :])
        y = part if y is None else y + part
    return y


def _ffn_kernel(x_ref, wg_ref, wu_ref, wd_ref, h_ref, m_ref, g_ref, o_ref):
    y = _swiglu(x_ref[...], wg_ref, wu_ref, wd_ref, FFN_DIM, MLP_CHUNK)
    o_ref[...] = h_ref[...] + m_ref[5:6, :] * _rms(y, g_ref[3:4, :])


def _resident(shape):
    return pl.BlockSpec(shape, lambda *_: (0,) * len(shape), pipeline_mode=pl.Buffered(1))


def _ffn(x, wg, wu, wd, h, mods, g):
    tm = MLP_ROW_TILE
    row = lambda n: pl.BlockSpec((tm, n), lambda i: (i, 0))
    return pl.pallas_call(
        _ffn_kernel,
        out_shape=jax.ShapeDtypeStruct((N_ROWS, D), F32),
        grid=(N_ROWS // tm,),
        in_specs=[row(D), _resident((D, FFN_DIM)), _resident((D, FFN_DIM)), _resident((FFN_DIM, D)),
                  row(D), _mod_spec(0, tm), _full((4, D))],
        out_specs=row(D),
        compiler_params=_params(("parallel",)),
        name="ffn_mlp",
    )(x, wg, wu, wd, h, mods, g)


N_ASSIGN = 2 * N_LAT
N_SORTED = N_ASSIGN + N_EXPERTS * MLP_ROW_TILE
N_SORTED_TILES = N_SORTED // MLP_ROW_TILE
ROUTE_TILE = 512
DMA_UNROLL = 8


def _row_copy(src_ref, src_row, dst_ref, dst_row, sem):
    return pltpu.make_async_copy(src_ref.at[pl.ds(src_row, 1), :], dst_ref.at[pl.ds(dst_row, 1), :], sem)


def _dispatch_kernel(ends_ref, d1_ref, d2_ref, u_ref, xs_ref, zero_ref, sem, zero_sem):
    tm = MLP_ROW_TILE

    def zero_tile(e):
        rows = pl.ds(pl.multiple_of(ends_ref[e] - tm, tm), tm)
        return pltpu.make_async_copy(zero_ref, xs_ref.at[rows, :], zero_sem)

    def group_size(e):
        return ends_ref[e] - (ends_ref[e - 1] if e > 0 else 0)

    def tail_tile(k):
        rows = pl.ds(pl.multiple_of(ends_ref[N_EXPERTS - 1] + k * tm, tm), tm)
        return pltpu.make_async_copy(zero_ref, xs_ref.at[rows, :], zero_sem)

    def has_tail(k):
        return ends_ref[N_EXPERTS - 1] + k * tm < N_SORTED

    @pl.when(pl.program_id(0) == 0)
    def _():
        zero_ref[...] = jnp.zeros_like(zero_ref)
        for e in range(N_EXPERTS):
            pl.when(group_size(e) > 0)(lambda e=e: zero_tile(e).start())
            pl.when(has_tail(e))(lambda e=e: tail_tile(e).start())
        for e in range(N_EXPERTS):
            pl.when(group_size(e) > 0)(lambda e=e: zero_tile(e).wait())
            pl.when(has_tail(e))(lambda e=e: tail_tile(e).wait())

    def start(t, _):
        _row_copy(u_ref, t, xs_ref, d1_ref[0, t], sem).start()
        _row_copy(u_ref, t, xs_ref, d2_ref[0, t], sem).start()
        return 0

    def wait(t, _):
        _row_copy(u_ref, t, xs_ref, d1_ref[0, t], sem).wait()
        _row_copy(u_ref, t, xs_ref, d2_ref[0, t], sem).wait()
        return 0

    lax.fori_loop(0, ROUTE_TILE, start, 0, unroll=DMA_UNROLL)
    lax.fori_loop(0, ROUTE_TILE, wait, 0, unroll=DMA_UNROLL)


def _route_spec():
    return pl.BlockSpec((None, 1, ROUTE_TILE), lambda i, *_: (i, 0, 0), memory_space=pltpu.SMEM)


def _dispatch(ends, d1, d2, u):
    tm = ROUTE_TILE
    grid_spec = pltpu.PrefetchScalarGridSpec(
        num_scalar_prefetch=1,
        grid=(N_LAT // tm,),
        in_specs=[_route_spec(), _route_spec(), pl.BlockSpec((tm, D), lambda i, ends: (i, 0))],
        out_specs=pl.BlockSpec(memory_space=pl.ANY),
        scratch_shapes=[pltpu.VMEM((MLP_ROW_TILE, D), F32), pltpu.SemaphoreType.DMA(()),
                        pltpu.SemaphoreType.DMA(())],
    )
    return pl.pallas_call(
        _dispatch_kernel,
        out_shape=jax.ShapeDtypeStruct((N_SORTED, D), F32),
        grid_spec=grid_spec,
        compiler_params=_params(("arbitrary",)),
        name="moe_dispatch",
    )(ends, d1, d2, u)


def _experts_kernel(te_ref, nt_ref, x_ref, wg_ref, wu_ref, wd_ref, o_ref, xb_ref):
    del te_ref
    i, f = pl.program_id(0), pl.program_id(1)
    live = i < nt_ref[0]

    @pl.when(live & (f == 0))
    def _():
        xb_ref[...] = x_ref[...].astype(BF16)

    @pl.when(live)
    def _():
        y = _swiglu(xb_ref[...], wg_ref, wu_ref, wd_ref, EXPERT_TILE, MLP_CHUNK)

        @pl.when(f == 0)
        def _():
            o_ref[...] = y

        @pl.when(f > 0)
        def _():
            o_ref[...] += y

    @pl.when(jnp.logical_not(live) & (f == 0))
    def _():
        o_ref[...] = jnp.zeros_like(o_ref)


def _experts(tile_expert, n_tiles, xs, wg, wu, wd):
    tm = MLP_ROW_TILE
    nf = EXPERT_DIM // EXPERT_TILE

    def f_blk(i, f, nt):
        return jnp.where(i < nt[0], f, nf - 1)

    grid_spec = pltpu.PrefetchScalarGridSpec(
        num_scalar_prefetch=2,
        grid=(N_SORTED_TILES, nf),
        in_specs=[pl.BlockSpec((tm, D), lambda i, f, te, nt: (jnp.minimum(i, nt[0] - 1), 0)),
                  pl.BlockSpec((None, D, EXPERT_TILE), lambda i, f, te, nt: (te[i], 0, f_blk(i, f, nt))),
                  pl.BlockSpec((None, D, EXPERT_TILE), lambda i, f, te, nt: (te[i], 0, f_blk(i, f, nt))),
                  pl.BlockSpec((None, EXPERT_TILE, D), lambda i, f, te, nt: (te[i], f_blk(i, f, nt), 0))],
        out_specs=pl.BlockSpec((tm, D), lambda i, f, te, nt: (i, 0)),
        scratch_shapes=[pltpu.VMEM((tm, D), BF16)],
    )
    return pl.pallas_call(
        _experts_kernel,
        out_shape=jax.ShapeDtypeStruct((N_SORTED, D), F32),
        grid_spec=grid_spec,
        compiler_params=_params(("arbitrary", "arbitrary")),
        name="moe_experts",
    )(tile_expert, n_tiles, xs, wg, wu, wd)


def _combine_kernel(d1_ref, d2_ref, n1_ref, n2_ref, w_ref, h_ref, m_ref, g_ref, ys_ref, o_ref, buf_ref, sems):
    i, n = pl.program_id(0), pl.num_programs(0)
    slot = i % 2

    def gather(a_ref, b_ref, slot, start):
        def body(t, _):
            for choice, d_ref in enumerate((a_ref, b_ref)):
                copy = _row_copy(ys_ref, d_ref[0, t], buf_ref.at[slot, choice], t, sems.at[slot])
                copy.start() if start else copy.wait()
            return 0

        lax.fori_loop(0, ROUTE_TILE, body, 0, unroll=DMA_UNROLL)

    @pl.when(i == 0)
    def _():
        gather(d1_ref, d2_ref, 0, True)

    @pl.when(i + 1 < n)
    def _():
        gather(n1_ref, n2_ref, 1 - slot, True)

    gather(d1_ref, d2_ref, slot, False)
    w = w_ref[...]
    lane = lax.broadcasted_iota(jnp.int32, w.shape, 1)
    w1 = jnp.sum(jnp.where(lane == 0, w, 0.0), axis=-1, keepdims=True)
    w2 = jnp.sum(jnp.where(lane == 1, w, 0.0), axis=-1, keepdims=True)
    y = w1 * buf_ref[slot, 0] + w2 * buf_ref[slot, 1]
    o_ref[...] = h_ref[...] + m_ref[5:6, :] * _rms(y, g_ref[3:4, :])


def _combine(d1, d2, wts, h, mods, g, ys):
    tm = ROUTE_TILE
    n = N_LAT // tm
    row = lambda w: pl.BlockSpec((tm, w), lambda i: (i, 0))
    nxt = pl.BlockSpec((None, 1, tm), lambda i: (jnp.minimum(i + 1, n - 1), 0, 0), memory_space=pltpu.SMEM)
    return pl.pallas_call(
        _combine_kernel,
        out_shape=jax.ShapeDtypeStruct((N_LAT, D), F32),
        grid=(n,),
        in_specs=[_route_spec(), _route_spec(), nxt, nxt, row(LANE), row(D), _mod_spec(1, tm), _full((4, D)),
                  pl.BlockSpec(memory_space=pl.ANY)],
        out_specs=row(D),
        scratch_shapes=[pltpu.VMEM((2, 2, tm, D), F32), pltpu.SemaphoreType.DMA((2,))],
        compiler_params=_params(("arbitrary",)),
        name="moe_combine",
    )(d1, d2, d1, d2, wts, h, mods, g, ys)


def _routing(route, counts):
    tm = MLP_ROW_TILE
    experts = jnp.arange(N_EXPERTS, dtype=jnp.int32)
    padded = (counts[0, :N_EXPERTS].astype(jnp.int32) + tm - 1) // tm * tm
    ends = jnp.cumsum(padded).astype(jnp.int32)
    starts = ends - padded

    def dest(choice, rank):
        return jnp.sum(jnp.where(choice[:, None] == experts, starts[None, :], 0), axis=1) + rank

    d1, d2 = dest(route[:, 0], route[:, 2]), dest(route[:, 1], route[:, 3])
    n_tiles = ends[-1] // tm
    tile_start = jnp.minimum(jnp.arange(N_SORTED_TILES, dtype=jnp.int32), n_tiles - 1) * tm
    tile_expert = jnp.sum((tile_start[:, None] >= ends[None, :]).astype(jnp.int32), axis=1)
    shape = (N_LAT // ROUTE_TILE, 1, ROUTE_TILE)
    return (ends, d1.astype(jnp.int32).reshape(shape), d2.astype(jnp.int32).reshape(shape),
            tile_expert.astype(jnp.int32), n_tiles.astype(jnp.int32).reshape(1))


GLA_Q0, GLA_K0, GLA_V0, GLA_G0, GLA_R0 = 0, 512, 1024, 2048, 3072
GLA_IN_P = GLA_R0 + LANE


def _gla_proj_kernel(h_ref, m_ref, g_ref, w_in_ref, w_vt_ref, q_ref, k_ref, v_ref, vt_ref, sg_ref, r_ref):
    u = _modulated(h_ref[...], g_ref[0:1, :], m_ref[0:1, :], m_ref[1:2, :]).astype(BF16)
    z = _dot(u, w_in_ref[...])
    q_ref[...] = (z[:, GLA_Q0:GLA_K0] * (GLA_DK ** -0.5)).astype(BF16)
    k_ref[...] = z[:, GLA_K0:GLA_V0].astype(BF16)
    v_ref[...] = z[:, GLA_V0:GLA_G0].astype(BF16)
    vt_ref[...] = _dot_nt(w_vt_ref[...], u).astype(BF16)
    sg_ref[...] = _silu(z[:, GLA_G0:GLA_R0]).astype(BF16)
    r_ref[...] = z[:, GLA_R0:].astype(BF16)


def _gla_project(h, mods, g, w):
    tm = ROW_TILE
    row = lambda n, dt=None: pl.BlockSpec((tm, n), lambda i: (i, 0))
    hk, hv = GLA_HEADS * GLA_DK, GLA_HEADS * GLA_DV
    sds = jax.ShapeDtypeStruct
    return pl.pallas_call(
        _gla_proj_kernel,
        out_shape=(sds((N_ROWS, hk), BF16), sds((N_ROWS, hk), BF16), sds((N_ROWS, hv), BF16),
                   sds((hv, N_ROWS), BF16), sds((N_ROWS, hv), BF16), sds((N_ROWS, LANE), BF16)),
        grid=(N_ROWS // tm,),
        in_specs=[row(D), _mod_spec(1, tm), _full((4, D)), _full((D, GLA_IN_P)), _full((hv, D))],
        out_specs=(row(hk), row(hk), row(hv), pl.BlockSpec((hv, tm), lambda i: (0, i)), row(hv), row(LANE)),
        compiler_params=_params(("parallel",)),
        name="gla_project",
    )(h, mods, g, w["w_in"], w["w_vt"])


GLA_LAT_BLOCKS = SEQ // GLA_BLOCK
GLA_CTX_BLOCKS = CTX // GLA_BLOCK
GLA_STRIPS = GLA_BLOCK // GLA_SUB


def _gla_blocks(chains):
    n = GLA_BLOCK
    r = lax.broadcasted_iota(jnp.int32, (n, n), 0)
    c = lax.broadcasted_iota(jnp.int32, (n, n), 1)
    row = lax.broadcasted_iota(jnp.int32, (n, GLA_DK), 0)
    keep = {rev: (c >= r) if rev else (c <= r) for rev in (False, True)}
    tri = {rev: jnp.where(keep[rev], 1.0, 0.0).astype(BF16) for rev in (False, True)}

    cums = []
    for ch in chains:
        parts = _dot(tri[ch["reverse"]], jnp.concatenate(_split_bf16(ch["la"]), axis=1))
        cums.append(parts[:, :GLA_DK] + parts[:, GLA_DK:])

    states, outs = [], []
    for ch, cum in zip(chains, cums):
        total = cum[0:1, :] if ch["reverse"] else cum[n - 1:n, :]
        k_state = (ch["k"] * jnp.exp(total - cum)).astype(BF16)
        states.append(ch["state"] * jnp.exp(total) + _dot(ch["vt"], k_state))
        if ch["q"] is None:
            outs.append(None)
        else:
            outs.append(_dot_nt((ch["q"] * jnp.exp(cum)).astype(BF16), ch["state"].astype(BF16)))

    strips = [[] for _ in chains]
    for i in range(GLA_STRIPS):
        lo, hi = i * GLA_SUB, (i + 1) * GLA_SUB
        for ci, (ch, cum) in enumerate(zip(chains, cums)):
            if ch["q"] is None:
                continue
            if ch["reverse"]:
                ref = cum[hi:hi + 1, :] if i < GLA_STRIPS - 1 else jnp.zeros((1, GLA_DK), F32)
                live = row >= lo
            else:
                ref = cum[lo - 1:lo, :] if i > 0 else jnp.zeros((1, GLA_DK), F32)
                live = row < hi
            q_loc = (ch["q"][lo:hi, :] * jnp.exp(cum[lo:hi, :] - ref)).astype(BF16)
            k_loc = jnp.where(live, ch["k"] * jnp.exp(ref - cum), 0.0).astype(BF16)
            strips[ci].append(_dot_nt(q_loc, k_loc))

    for ci, ch in enumerate(chains):
        if ch["q"] is not None:
            scores = jnp.where(keep[ch["reverse"]], jnp.concatenate(strips[ci], axis=0), 0.0).astype(BF16)
            outs[ci] = outs[ci] + _dot(scores, ch["v"])
    return list(zip(outs, states))


GLA_GROUP = 2


def _gla_kernel(q_ref, kl_ref, kc_ref, vl_ref, vtl_ref, vtc_ref, rl_ref, rc_ref,
                wgf_ref, wgb_ref, bgf_ref, bgb_ref, o_ref, s_ref):
    n = GLA_BLOCK
    s_ref[...] = jnp.zeros_like(s_ref)
    chains = [(hd, rev) for hd in range(GLA_GROUP) for rev in (False, True)]

    def log_decays(r_blk, reverse):
        wg_ref, bg_ref = (wgb_ref, bgb_ref) if reverse else (wgf_ref, bgf_ref)
        zg = _dot(r_blk, wg_ref[...]) + bg_ref[...]
        return (jnp.minimum(zg, 0.0) - jnp.log(1.0 + jnp.exp(-jnp.abs(zg)))) * (1.0 / GLA_GATE_NORM)

    def chain(hd, reverse, rows, la, k_ref, vt_ref, with_output):
        ks = slice(hd * GLA_DK, (hd + 1) * GLA_DK)
        vs = slice(hd * GLA_DV, (hd + 1) * GLA_DV)
        return dict(reverse=reverse, k=k_ref[rows, ks].astype(F32), vt=vt_ref[vs, rows],
                    la=la[:, ks], state=s_ref[int(reverse), hd],
                    q=q_ref[rows, ks].astype(F32) if with_output else None,
                    v=vl_ref[rows, vs] if with_output else None)

    for j in range(GLA_CTX_BLOCKS):
        rows = {rev: slice(blk * n, (blk + 1) * n) for rev, blk in ((False, j), (True, GLA_CTX_BLOCKS - 1 - j))}
        la = {rev: log_decays(rc_ref[rows[rev], :], rev) for rev in (False, True)}
        work = [chain(hd, rev, rows[rev], la[rev], kc_ref, vtc_ref, False) for hd, rev in chains]
        for (hd, reverse), (_, s_new) in zip(chains, _gla_blocks(work)):
            s_ref[int(reverse), hd] = s_new

    def step(j, accumulate):
        rows = {rev: pl.ds(pl.multiple_of(blk * n, n), n) for rev, blk in ((False, j), (True, GLA_LAT_BLOCKS - 1 - j))}
        la = {rev: log_decays(rl_ref[rows[rev], :], rev) for rev in (False, True)}
        work = [chain(hd, rev, rows[rev], la[rev], kl_ref, vtl_ref, True) for hd, rev in chains]
        where = [(rows[rev], slice(hd * GLA_DV, (hd + 1) * GLA_DV)) for hd, rev in chains]
        for (hd, reverse), (rows, vs), (o, s_new) in zip(chains, where, _gla_blocks(work)):
            s_ref[int(reverse), hd] = s_new
            if accumulate:
                o_ref[rows, vs] += o
            else:
                o_ref[rows, vs] = o

    half = GLA_LAT_BLOCKS // 2

    def first(j, _):
        step(j, False)
        return 0

    def second(j, _):
        step(j, True)
        return 0

    lax.fori_loop(0, half, first, 0)
    lax.fori_loop(half, GLA_LAT_BLOCKS, second, 0)


def _gla_scan(q, k, v, vt, r, w_gate_f, w_gate_b, b_gate_f, b_gate_b):
    gk, gv = GLA_GROUP * GLA_DK, GLA_GROUP * GLA_DV
    lat = lambda n: pl.BlockSpec((SEQ, n), lambda b, g: (b, g))
    ctx = lambda n: pl.BlockSpec((CTX, n), lambda b, g: (N_LAT // CTX + b, g))
    grp = lambda rows: pl.BlockSpec((rows, gk), lambda b, g: (0, g))
    return pl.pallas_call(
        _gla_kernel,
        out_shape=jax.ShapeDtypeStruct((N_LAT, GLA_HEADS * GLA_DV), F32),
        grid=(BATCH, GLA_HEADS // GLA_GROUP),
        in_specs=[lat(gk), lat(gk), ctx(gk), lat(gv),
                  pl.BlockSpec((gv, SEQ), lambda b, g: (g, b)),
                  pl.BlockSpec((gv, CTX), lambda b, g: (g, N_LAT // CTX + b)),
                  pl.BlockSpec((SEQ, LANE), lambda b, g: (b, 0)),
                  pl.BlockSpec((CTX, LANE), lambda b, g: (N_LAT // CTX + b, 0)),
                  grp(LANE), grp(LANE), grp(1), grp(1)],
        out_specs=lat(gv),
        scratch_shapes=[pltpu.VMEM((2, GLA_GROUP, GLA_DV, GLA_DK), F32)],
        compiler_params=_params(("parallel", "parallel")),
        name="gla_scan",
    )(q, k, k, v, vt, vt, r, r, w_gate_f, w_gate_b, b_gate_f, b_gate_b)


def _gla_out_kernel(o_ref, sg_ref, on_ref, w_ref, h_ref, m_ref, g_ref, wr_hi_ref, wr_lo_ref, br_ref,
                    h_out_ref, u_ref, route_ref, wts_ref, count_ref):
    parts = []
    for hd in range(GLA_HEADS):
        sl = slice(hd * GLA_DV, (hd + 1) * GLA_DV)
        parts.append(_rms(o_ref[:, sl], on_ref[...]) * sg_ref[:, sl].astype(F32))
    y = _dot(jnp.concatenate(parts, axis=-1).astype(BF16), w_ref[...])
    h_new, u = _mixer_epilogue(y, h_ref[...], m_ref, g_ref)
    h_out_ref[...] = h_new
    u_ref[...] = u
    logits = _dot3(u, wr_hi_ref[...], wr_lo_ref[...]) + br_ref[...]
    lane = lax.broadcasted_iota(jnp.int32, logits.shape, 1)
    l1 = jnp.max(logits, axis=-1, keepdims=True)
    i1 = jnp.min(jnp.where(logits == l1, lane, LANE), axis=-1, keepdims=True)
    rest = jnp.where(lane == i1, -jnp.inf, logits)
    l2 = jnp.max(rest, axis=-1, keepdims=True)
    i2 = jnp.min(jnp.where(rest == l2, lane, LANE), axis=-1, keepdims=True)
    e2 = jnp.exp(l2 - l1)
    wts_ref[...] = jnp.where(lane == 0, 1.0 / (1.0 + e2), jnp.where(lane == 1, e2 / (1.0 + e2), 0.0))

    @pl.when(pl.program_id(0) == 0)
    def _():
        count_ref[...] = jnp.zeros_like(count_ref)

    chosen = jnp.where((lane == i1) | (lane == i2), 1.0, 0.0)
    tm = chosen.shape[0]
    earlier = lax.broadcasted_iota(jnp.int32, (tm, tm), 1) < lax.broadcasted_iota(jnp.int32, (tm, tm), 0)
    rank = count_ref[0:1, :] + _dot(jnp.where(earlier, 1.0, 0.0).astype(BF16), chosen.astype(BF16))
    rank1 = jnp.sum(jnp.where(lane == i1, rank, 0.0), axis=-1, keepdims=True).astype(jnp.int32)
    rank2 = jnp.sum(jnp.where(lane == i2, rank, 0.0), axis=-1, keepdims=True).astype(jnp.int32)
    count_ref[...] = count_ref[...] + jnp.sum(chosen, axis=0, keepdims=True)
    route_ref[...] = jnp.where(lane == 0, i1, jnp.where(lane == 1, i2,
                               jnp.where(lane == 2, rank1, jnp.where(lane == 3, rank2, 0))))


def _gla_output(o, sg, o_norm, w_out, h, mods, g, wr_hi, wr_lo, br):
    tm = ROW_TILE
    row = lambda n: pl.BlockSpec((tm, n), lambda i: (i, 0))
    hv = GLA_HEADS * GLA_DV
    return pl.pallas_call(
        _gla_out_kernel,
        out_shape=(jax.ShapeDtypeStruct((N_LAT, D), F32), jax.ShapeDtypeStruct((N_LAT, D), F32),
                   jax.ShapeDtypeStruct((N_LAT, LANE), jnp.int32), jax.ShapeDtypeStruct((N_LAT, LANE), F32),
                   jax.ShapeDtypeStruct((8, LANE), F32)),
        grid=(N_LAT // tm,),
        in_specs=[row(hv), row(hv), _full((1, GLA_DV)), _full((hv, D)), row(D), _mod_spec(1, tm),
                  _full((4, D)), _full((D, LANE)), _full((D, LANE)), _full((1, LANE))],
        out_specs=(row(D), row(D), row(LANE), row(LANE), _full((8, LANE))),
        compiler_params=_params(("arbitrary",)),
        name="gla_output",
    )(o, sg, o_norm, w_out, h, mods, g, wr_hi, wr_lo, br)


def _slots(w, n_heads, width):
    k = w.shape[0]
    w = w.reshape(k, n_heads, width)
    return jnp.pad(w, ((0, 0), (0, 0), (0, LANE - width))).reshape(k, n_heads * LANE)


def _attn_weights(w_in, q_norm, w_uq, kv_norm, w_ukv, qk_norm, w_out):
    c = 0
    cols = {}
    for name, n in (("q_lat", MLA_Q_RANK), ("kv_lat", MLA_KV_RANK), ("k_rope", MLA_ROPE),
                    ("q_b", GQA_HEADS * GQA_HEAD_DIM), ("k_b", GQA_KV_HEADS * GQA_HEAD_DIM),
                    ("v_b", GQA_KV_HEADS * GQA_HEAD_DIM)):
        cols[name] = w_in[:, c:c + n]
        c += n
    k_rope = jnp.pad(cols["k_rope"], ((0, 0), (MLA_NOPE, LANE - MLA_NOPE - MLA_ROPE)))
    w_in_p = jnp.concatenate([cols["q_lat"], cols["kv_lat"], k_rope,
                              _slots(cols["q_b"], GQA_HEADS, GQA_HEAD_DIM),
                              _slots(cols["k_b"], GQA_KV_HEADS, GQA_HEAD_DIM)], axis=1)
    ukv = w_ukv.reshape(MLA_KV_RANK, MLA_HEADS, MLA_NOPE + MLA_V)
    w_uk = _slots(ukv[:, :, :MLA_NOPE].reshape(MLA_KV_RANK, -1), MLA_HEADS, MLA_NOPE)
    w_uv = ukv[:, :, MLA_NOPE:].reshape(MLA_KV_RANK, MLA_HEADS * MLA_V)
    return {
        "w_in": w_in_p.astype(BF16),
        "q_norm": q_norm.reshape(1, -1),
        "w_uq": _slots(w_uq, MLA_HEADS, MLA_NOPE + MLA_ROPE).astype(BF16),
        "kv_norm": kv_norm.reshape(1, -1),
        "w_uk": w_uk.astype(BF16),
        "w_uvt": w_uv.T.astype(BF16),
        "w_vbt": cols["v_b"].T.astype(BF16),
        "qk_norm": jnp.pad(qk_norm, ((0, 0), (0, LANE - GQA_HEAD_DIM))),
        "w_out": w_out.astype(BF16),
    }


def _axial_tables(d):
    half, p = d // 2, d // 4
    t = jnp.arange(SEQ, dtype=jnp.int32)
    freqs = ROPE_THETA ** (-jnp.arange(0, half, 2, dtype=F32) / half)
    cos, sa, sb = [], [], []
    for pos in (t // GRID_W, t % GRID_W):
        ang = pos.astype(F32)[:, None] * freqs[None, :]
        c, s, z = jnp.cos(ang), jnp.sin(ang), jnp.zeros_like(ang)
        cos += [c, c]
        sa += [-s, z]
        sb += [z, s]
    return [jnp.concatenate(x, axis=-1) for x in (cos, sa, sb)]


def _rope_tables():
    def embed(tbl, left, fill):
        right = LANE - left - tbl.shape[1]
        tbl = jnp.pad(tbl, ((0, 0), (left, right)), constant_values=fill)
        return jnp.pad(tbl, ((0, ATTN_PROJ_TILE), (0, 0)), constant_values=fill)

    out = []
    for d, left in ((MLA_ROPE, MLA_NOPE), (GQA_HEAD_DIM, 0)):
        cos, sa, sb = _axial_tables(d)
        out += [embed(cos, left, 1.0), embed(sa, left, 0.0), embed(sb, left, 0.0)]
    return out


def _gla_weights(w_in, w_gate2, b_gate):
    hk, hv = GLA_HEADS * GLA_DK, GLA_HEADS * GLA_DV
    r = jnp.pad(w_in[:, 2 * hk + 2 * hv:], ((0, 0), (0, LANE - 2 * GLA_GATE_RANK)))
    w_in_p = jnp.concatenate([w_in[:, :2 * hk + 2 * hv], r], axis=1)
    pad_f = ((0, LANE - GLA_GATE_RANK), (0, 0))
    pad_b = ((GLA_GATE_RANK, LANE - 2 * GLA_GATE_RANK), (0, 0))
    return {
        "w_in": w_in_p.astype(BF16),
        "w_vt": w_in[:, 2 * hk:2 * hk + hv].T.astype(BF16),
        "w_gate_f": jnp.pad(w_gate2[0], pad_f).astype(BF16),
        "w_gate_b": jnp.pad(w_gate2[1], pad_b).astype(BF16),
        "b_gate_f": b_gate[0].reshape(1, hk),
        "b_gate_b": b_gate[1].reshape(1, hk),
    }


def kernel(x, c, ctx, c_ctx, mod_w, mod_b, norm_g, attn_w_in, attn_q_norm, attn_w_uq, attn_kv_norm, attn_w_ukv,
           attn_qk_norm, attn_w_out, gla_w_in, gla_w_gate2, gla_b_gate, gla_o_norm, gla_w_out, ffn_w_gate,
           ffn_w_up, ffn_w_down, moe_w_router, moe_b_router, moe_w_gate, moe_w_up, moe_w_down):
    assert x.shape == (BATCH, SEQ, D) and ctx.shape == (BATCH, CTX, D)
    x, ctx = x.reshape(N_LAT, D), ctx.reshape(N_CTX, D)
    cc = jnp.concatenate([c, c_ctx[None, :], jnp.zeros((MOD_ROWS - BATCH - 1, D), F32)], axis=0)
    mods = _mod_vectors(cc, mod_w, mod_b).reshape(mod_w.shape[0], MOD_ROWS, 6, D)

    aw = _attn_weights(attn_w_in[0], attn_q_norm[0], attn_w_uq[0], attn_kv_norm[0], attn_w_ukv[0],
                       attn_qk_norm[0], attn_w_out[0])
    q, k, vt = _attn_project(x, ctx, mods, norm_g[0], aw, _rope_tables())
    o_lat, o_ctx = _attention(q, k, vt)
    h, u = _attn_output(o_lat, o_ctx, aw["w_out"], x, ctx, mods, norm_g[0])
    h = _ffn(u, ffn_w_gate[0].astype(BF16), ffn_w_up[0].astype(BF16), ffn_w_down[0].astype(BF16), h, mods,
             norm_g[0])

    gw = _gla_weights(gla_w_in[0], gla_w_gate2[0], gla_b_gate[0])
    gq, gk, gv, gvt, sg, gr = _gla_project(h, mods, norm_g[1], gw)
    go = _gla_scan(gq, gk, gv, gvt, gr, gw["w_gate_f"], gw["w_gate_b"], gw["b_gate_f"], gw["b_gate_b"])
    wr = jnp.pad(moe_w_router[0], ((0, 0), (0, LANE - N_EXPERTS)))
    wr_hi = wr.astype(BF16)
    wr_lo = (wr - wr_hi.astype(F32)).astype(BF16)
    br = jnp.pad(moe_b_router[0], (0, LANE - N_EXPERTS), constant_values=-jnp.inf).reshape(1, LANE)
    h, u, route, wts, counts = _gla_output(go, sg, gla_o_norm[0].reshape(1, GLA_DV), gla_w_out[0].astype(BF16),
                                           h, mods, norm_g[1], wr_hi, wr_lo, br)
    ends, d1, d2, tile_expert, n_tiles = _routing(route, counts)
    xs = _dispatch(ends, d1, d2, u)
    ys = _experts(tile_expert, n_tiles, xs, moe_w_gate[0].astype(BF16), moe_w_up[0].astype(BF16),
                  moe_w_down[0].astype(BF16))
    h = _combine(d1, d2, wts, h, mods, norm_g[1], ys)
    return h.reshape(BATCH, SEQ, D)
```

```python
import jax
import jax.numpy as jnp
import numpy as np
from jax import lax
from jax.experimental import pallas as pl
from jax.experimental.pallas import tpu as pltpu

F32 = jnp.float32
BF16 = jnp.bfloat16

D = 1024
BATCH = 4
SEQ = 4096
CTX = 256
GRID_W = 64
ROPE_THETA = 10000.0
EPS = 1e-6

N_LAT = BATCH * SEQ
N_CTX = BATCH * CTX
N_ROWS = N_LAT + N_CTX
MOD_ROWS = 8
CTX_MOD_ROW = BATCH

LANE = 128
ROW_TILE = 512
ATTN_PROJ_TILE = 256

MLA_HEADS = 8
MLA_Q_RANK = 384
MLA_KV_RANK = 256
MLA_NOPE = 64
MLA_ROPE = 32
MLA_V = 64
GQA_HEADS = 8
GQA_KV_HEADS = 2
GQA_GROUP = GQA_HEADS // GQA_KV_HEADS
GQA_HEAD_DIM = 64
N_Q_HEADS = MLA_HEADS + GQA_HEADS

GLA_HEADS = 4
GLA_DK = 128
GLA_DV = 256
GLA_GATE_RANK = 16
GLA_GATE_NORM = 16.0
GLA_BLOCK = 128
GLA_SUB = 32

MXU_WIDTH = 256
FFN_DIM = 2816
N_EXPERTS = 8
EXPERT_DIM = 3584
EXPERT_TILE = EXPERT_DIM // 2
MLP_CHUNK = 2 * MXU_WIDTH
MLP_ROW_TILE = 512

VMEM_LIMIT = 56 * 1024 * 1024


def _params(sem):
    return pltpu.CompilerParams(dimension_semantics=sem, vmem_limit_bytes=VMEM_LIMIT)


def _rms(x, g):
    return x * lax.rsqrt(jnp.mean(x * x, axis=-1, keepdims=True) + EPS) * g


def _silu(x):
    return x / (1.0 + jnp.exp(-x))


def _split_bf16(x):
    hi = x.astype(BF16)
    lo = (x - hi.astype(F32)).astype(BF16)
    return hi, lo


def _dot(a, b):
    return jnp.dot(a, b, preferred_element_type=F32)


def _dot_nt(a, b):
    return lax.dot_general(a, b, (((1,), (1,)), ((), ())), preferred_element_type=F32)


def _dot3(a, b_hi, b_lo):
    a_hi, a_lo = _split_bf16(a)
    return _dot(a_hi, b_hi) + (_dot(a_hi, b_lo) + _dot(a_lo, b_hi))


def _modulated(h, g_row, shift, scale):
    return _rms(h, g_row) * (1.0 + scale) + shift


def _rope(x, cos, sin):
    return x * cos + pltpu.roll(x, LANE // 2, 1) * sin


def _mod_row(i, tile):
    r0 = i * tile
    return jnp.where(r0 < N_LAT, r0 // SEQ, CTX_MOD_ROW)


def _mod_spec(layer, tile):
    return pl.BlockSpec((None, None, 6, D), lambda i, *_: (layer, _mod_row(i, tile), 0, 0))


def _rope_block(i):
    r0 = i * ATTN_PROJ_TILE
    return jnp.where(r0 < N_LAT, (r0 % SEQ) // ATTN_PROJ_TILE, SEQ // ATTN_PROJ_TILE)


def _full(shape):
    return pl.BlockSpec(shape, lambda *_: (0,) * len(shape))


def _mod_kernel(c_ref, w_ref, b_ref, o_ref):
    w_hi, w_lo = _split_bf16(w_ref[...])
    o_ref[...] = _dot3(_silu(c_ref[...]), w_hi, w_lo) + b_ref[...]


def _mod_vectors(cc, mod_w, mod_b):
    depth, _, n = mod_w.shape
    tn = 1536
    return pl.pallas_call(
        _mod_kernel,
        out_shape=jax.ShapeDtypeStruct((depth, MOD_ROWS, n), F32),
        grid=(depth, n // tn),
        in_specs=[
            pl.BlockSpec((MOD_ROWS, D), lambda l, j: (0, 0)),
            pl.BlockSpec((None, D, tn), lambda l, j: (l, 0, j)),
            pl.BlockSpec((None, 1, tn), lambda l, j: (l, 0, j)),
        ],
        out_specs=pl.BlockSpec((None, MOD_ROWS, tn), lambda l, j: (l, 0, j)),
        compiler_params=_params(("parallel", "parallel")),
        name="mod_vectors",
    )(cc, mod_w, mod_b.reshape(depth, 1, n))


Q_LAT0, KV_LAT0, K_ROPE0 = 0, 384, 640
Q_B0 = 768
K_B0 = Q_B0 + GQA_HEADS * LANE
ATTN_IN_P = K_B0 + GQA_KV_HEADS * LANE
ATTN_V = 64
ATTN_VA = ATTN_V + 16
LOG2E = 1.4426950408889634


def _input_rows(x_ref, c_ref):
    return jnp.where(pl.program_id(0) < N_LAT // x_ref.shape[0], x_ref[...], c_ref[...])


def _input_specs(tm):
    n_lat = N_LAT // tm
    return [pl.BlockSpec((tm, D), lambda i: (jnp.minimum(i, n_lat - 1), 0)),
            pl.BlockSpec((tm, D), lambda i: (jnp.maximum(i - n_lat, 0), 0))]


def _attn_proj_kernel(x_ref, c_ref, m_ref, g_ref, w_in_ref, qn_ref, w_uq_ref, kvn_ref, w_uk_ref, w_uvt_ref,
                      w_vbt_ref, qkn_ref, ca_ref, sa_ref, cb_ref, sb_ref, q_ref, k_ref, vt_ref):
    u = _modulated(_input_rows(x_ref, c_ref), g_ref[0:1, :], m_ref[0:1, :], m_ref[1:2, :]).astype(BF16)
    z = _dot(u, w_in_ref[...])
    q_lat = _rms(z[:, Q_LAT0:Q_LAT0 + MLA_Q_RANK], qn_ref[...]).astype(BF16)
    kv_lat = _rms(z[:, KV_LAT0:KV_LAT0 + MLA_KV_RANK], kvn_ref[...]).astype(BF16)
    q_a = _dot(q_lat, w_uq_ref[...])
    k_a = _dot(kv_lat, w_uk_ref[...])
    ca, sa, cb, sb = ca_ref[...], sa_ref[...], cb_ref[...], sb_ref[...]
    scale_a = (MLA_NOPE + MLA_ROPE) ** -0.5 * LOG2E
    scale_b = GQA_HEAD_DIM ** -0.5 * LOG2E
    k_rope = _rope(z[:, K_ROPE0:K_ROPE0 + LANE], ca, sa)
    for hd in range(MLA_HEADS):
        sl = slice(hd * LANE, (hd + 1) * LANE)
        q_ref[:, sl] = (_rope(q_a[:, sl], ca, sa) * scale_a).astype(BF16)
        k_ref[:, sl] = (k_a[:, sl] + k_rope).astype(BF16)

    def head_norm(x, gain):
        ms = jnp.sum(x * x, axis=-1, keepdims=True) * (1.0 / GQA_HEAD_DIM)
        return x * lax.rsqrt(ms + EPS) * gain

    for hd in range(GQA_HEADS):
        x = head_norm(z[:, Q_B0 + hd * LANE:Q_B0 + (hd + 1) * LANE], qkn_ref[0:1, :])
        sl = slice((MLA_HEADS + hd) * LANE, (MLA_HEADS + hd + 1) * LANE)
        q_ref[:, sl] = (_rope(x, cb, sb) * scale_b).astype(BF16)
    for kv in range(GQA_KV_HEADS):
        x = head_norm(z[:, K_B0 + kv * LANE:K_B0 + (kv + 1) * LANE], qkn_ref[1:2, :])
        x = _rope(x, cb, sb).astype(BF16)
        for hd in range(kv * GQA_GROUP, (kv + 1) * GQA_GROUP):
            k_ref[:, (MLA_HEADS + hd) * LANE:(MLA_HEADS + hd + 1) * LANE] = x

    vat = _dot_nt(w_uvt_ref[...], kv_lat).astype(BF16)
    vbt = _dot_nt(w_vbt_ref[...], u).astype(BF16)
    ones = jnp.ones((ATTN_VA - ATTN_V, vat.shape[1]), BF16)
    for hd in range(N_Q_HEADS):
        if hd < MLA_HEADS:
            v_t = vat[hd * ATTN_V:(hd + 1) * ATTN_V, :]
        else:
            kv = (hd - MLA_HEADS) // GQA_GROUP
            v_t = vbt[kv * ATTN_V:(kv + 1) * ATTN_V, :]
        vt_ref[hd * ATTN_VA:hd * ATTN_VA + ATTN_V, :] = v_t
        vt_ref[hd * ATTN_VA + ATTN_V:(hd + 1) * ATTN_VA, :] = ones


def _attn_project(x, ctx, mods, g, w, tables):
    tm = ATTN_PROJ_TILE
    row = lambda n: pl.BlockSpec((tm, n), lambda i: (i, 0))
    tab = pl.BlockSpec((tm, LANE), lambda i: (_rope_block(i), 0))
    return pl.pallas_call(
        _attn_proj_kernel,
        out_shape=(jax.ShapeDtypeStruct((N_ROWS, N_Q_HEADS * LANE), BF16),
                   jax.ShapeDtypeStruct((N_ROWS, N_Q_HEADS * LANE), BF16),
                   jax.ShapeDtypeStruct((N_Q_HEADS * ATTN_VA, N_ROWS), BF16)),
        grid=(N_ROWS // tm,),
        in_specs=[*_input_specs(tm), _mod_spec(0, tm), _full((4, D)), _full((D, ATTN_IN_P)),
                  _full((1, MLA_Q_RANK)), _full((MLA_Q_RANK, MLA_HEADS * LANE)),
                  _full((1, MLA_KV_RANK)), _full((MLA_KV_RANK, MLA_HEADS * LANE)),
                  _full((MLA_HEADS * ATTN_V, MLA_KV_RANK)), _full((GQA_KV_HEADS * ATTN_V, D)),
                  _full((2, LANE)), tab, tab, tab, tab],
        out_specs=(row(N_Q_HEADS * LANE), row(N_Q_HEADS * LANE),
                   pl.BlockSpec((N_Q_HEADS * ATTN_VA, tm), lambda i: (0, i))),
        compiler_params=_params(("parallel",)),
        name="attn_project",
    )(x, ctx, mods, g, w["w_in"], w["q_norm"], w["w_uq"], w["kv_norm"], w["w_uk"], w["w_uvt"], w["w_vbt"],
      w["qk_norm"], *tables)


ATT_TQ = 256
ATT_TK = 1024
ATT_LAT_TILES = SEQ // ATT_TQ
ATT_GROUP = 4


def _attn_kernel(ql_ref, qc_ref, kc_ref, kl_ref, vtc_ref, vtl_ref, ol_ref, oc_ref, sa_ref, sb_ref):
    heads = range(ATT_GROUP)
    n_lat = SEQ // ATT_TK
    every = slice(None)
    init = (jnp.full((1, ATT_TQ), -jnp.inf, F32), jnp.zeros((ATTN_VA, ATT_TQ), F32))

    def lat_queries(t, hd):
        return ql_ref[pl.ds(pl.multiple_of(t * ATT_TQ, ATT_TQ), ATT_TQ), hd * LANE:(hd + 1) * LANE]

    def lat_keys(j):
        return slice(j * ATT_TK, (j + 1) * ATT_TK)

    def scores(s_ref, hd, q, k_ref, keys, n_keys):
        s = _dot_nt(k_ref[keys, hd * LANE:(hd + 1) * LANE], q)
        s_ref[hd, 0:n_keys, :] = s
        return jnp.max(s, axis=0, keepdims=True)

    def consume(s_ref, hd, vt_ref, keys, n_keys, cmax, carry):
        m, acc = carry
        m_new = jnp.maximum(m, cmax)
        p = jnp.exp2(s_ref[hd, 0:n_keys, :] - m_new).astype(BF16)
        acc = jnp.exp2(m - m_new) * acc + _dot(vt_ref[hd * ATTN_VA:(hd + 1) * ATTN_VA, keys], p)
        return m_new, acc

    def normalised(carries):
        o_t = jnp.concatenate([acc[0:ATTN_V, :] / acc[ATTN_V:ATTN_V + 1, :] for _, acc in carries], axis=0)
        return o_t.T.astype(BF16)

    def query_tile(t, t_next, first_ref, second_ref, cmax):
        bufs = (first_ref, second_ref)
        carries = (init,) * ATT_GROUP
        for c in range(n_lat + 1):
            cur, nxt = bufs[c % 2], bufs[(c + 1) % 2]
            cmax_next, out = [], []
            for hd in heads:
                if c + 1 < n_lat:
                    cmax_next.append(scores(nxt, hd, lat_queries(t, hd), kl_ref, lat_keys(c + 1), ATT_TK))
                elif c + 1 == n_lat:
                    cmax_next.append(scores(nxt, hd, lat_queries(t, hd), kc_ref, every, CTX))
                else:
                    cmax_next.append(scores(nxt, hd, lat_queries(t_next, hd), kl_ref, lat_keys(0), ATT_TK))
                if c < n_lat:
                    out.append(consume(cur, hd, vtl_ref, lat_keys(c), ATT_TK, cmax[hd], carries[hd]))
                else:
                    out.append(consume(cur, hd, vtc_ref, every, CTX, cmax[hd], carries[hd]))
            cmax, carries = cmax_next, out
        ol_ref[pl.ds(pl.multiple_of(t * ATT_TQ, ATT_TQ), ATT_TQ), :] = normalised(carries)
        return tuple(cmax)

    def tile_pair(i, cmax):
        cmax = query_tile(2 * i, 2 * i + 1, sa_ref, sb_ref, cmax)
        return query_tile(2 * i + 1, jnp.minimum(2 * i + 2, ATT_LAT_TILES - 1), sb_ref, sa_ref, cmax)

    cmax0 = tuple(scores(sa_ref, hd, lat_queries(0, hd), kl_ref, lat_keys(0), ATT_TK) for hd in heads)
    lax.fori_loop(0, ATT_LAT_TILES // 2, tile_pair, cmax0)

    cmax_c = [scores(sa_ref, hd, qc_ref[:, hd * LANE:(hd + 1) * LANE], kc_ref, every, CTX) for hd in heads]
    oc_ref[...] = normalised([consume(sa_ref, hd, vtc_ref, every, CTX, cmax_c[hd], init) for hd in heads])


def _attention(q, k, vt):
    assert (SEQ // ATT_TK) % 2 == 0 and ATT_LAT_TILES % 2 == 0 and CTX == ATT_TQ
    gw = ATT_GROUP * LANE
    gv = ATT_GROUP * ATTN_V
    gva = ATT_GROUP * ATTN_VA
    ctx_blk = N_LAT // CTX
    return pl.pallas_call(
        _attn_kernel,
        out_shape=(jax.ShapeDtypeStruct((N_LAT, N_Q_HEADS * ATTN_V), BF16),
                   jax.ShapeDtypeStruct((N_CTX, N_Q_HEADS * ATTN_V), BF16)),
        grid=(BATCH, N_Q_HEADS // ATT_GROUP),
        in_specs=[pl.BlockSpec((SEQ, gw), lambda b, g: (b, g)),
                  pl.BlockSpec((CTX, gw), lambda b, g: (ctx_blk + b, g)),
                  pl.BlockSpec((CTX, gw), lambda b, g: (ctx_blk + b, g)),
                  pl.BlockSpec((SEQ, gw), lambda b, g: (b, g)),
                  pl.BlockSpec((gva, CTX), lambda b, g: (g, ctx_blk + b)),
                  pl.BlockSpec((gva, SEQ), lambda b, g: (g, b))],
        out_specs=(pl.BlockSpec((SEQ, gv), lambda b, g: (b, g)), pl.BlockSpec((CTX, gv), lambda b, g: (b, g))),
        scratch_shapes=[pltpu.VMEM((ATT_GROUP, ATT_TK, ATT_TQ), F32)] * 2,
        compiler_params=_params(("parallel", "parallel")),
        name="attention",
    )(q, q, k, k, vt, vt)


def _mixer_epilogue(y, h, m_ref, g_ref):
    h_new = h + m_ref[2:3, :] * _rms(y, g_ref[1:2, :])
    u = _modulated(h_new, g_ref[2:3, :], m_ref[3:4, :], m_ref[4:5, :])
    return h_new, u


def _attn_out_kernel(ol_ref, oc_ref, w_ref, x_ref, c_ref, m_ref, g_ref, h_out_ref, u_ref):
    y = _dot(_input_rows(ol_ref, oc_ref), w_ref[...])
    h_new, u = _mixer_epilogue(y, _input_rows(x_ref, c_ref), m_ref, g_ref)
    h_out_ref[...] = h_new
    u_ref[...] = u.astype(BF16)


def _attn_output(o_lat, o_ctx, w_out, x, ctx, mods, g):
    assert N_Q_HEADS * ATTN_V == D
    tm = ROW_TILE
    row = lambda n: pl.BlockSpec((tm, n), lambda i: (i, 0))
    return pl.pallas_call(
        _attn_out_kernel,
        out_shape=(jax.ShapeDtypeStruct((N_ROWS, D), F32), jax.ShapeDtypeStruct((N_ROWS, D), BF16)),
        grid=(N_ROWS // tm,),
        in_specs=[*_input_specs(tm), _full((D, D)), *_input_specs(tm), _mod_spec(0, tm), _full((4, D))],
        out_specs=(row(D), row(D)),
        compiler_params=_params(("parallel",)),
        name="attn_output",
    )(o_lat, o_ctx, w_out, x, ctx, mods, g)


def _swiglu(x, wg_ref, wu_ref, wd_ref, f_dim, chunk):
    y = None
    for c0 in range(0, f_dim, chunk):
        cols = slice(c0, min(c0 + chunk, f_dim))
        a = _silu(_dot(x, wg_ref[:, cols])) * _dot(x, wu_ref[:, cols])
        part = _dot(a.astype(BF16), wd_ref[cols, :])
        y = part if y is None else y + part
    return y


def _ffn_kernel(x_ref, wg_ref, wu_ref, wd_ref, h_ref, m_ref, g_ref, o_ref):
    y = _swiglu(x_ref[...], wg_ref, wu_ref, wd_ref, FFN_DIM, MLP_CHUNK)
    o_ref[...] = h_ref[...] + m_ref[5:6, :] * _rms(y, g_ref[3:4, :])


def _resident(shape):
    return pl.BlockSpec(shape, lambda *_: (0,) * len(shape), pipeline_mode=pl.Buffered(1))


def _ffn(x, wg, wu, wd, h, mods, g):
    tm = MLP_ROW_TILE
    row = lambda n: pl.BlockSpec((tm, n), lambda i: (i, 0))
    return pl.pallas_call(
        _ffn_kernel,
        out_shape=jax.ShapeDtypeStruct((N_ROWS, D), F32),
        grid=(N_ROWS // tm,),
        in_specs=[row(D), _resident((D, FFN_DIM)), _resident((D, FFN_DIM)), _resident((FFN_DIM, D)),
                  row(D), _mod_spec(0, tm), _full((4, D))],
        out_specs=row(D),
        compiler_params=_params(("parallel",)),
        name="ffn_mlp",
    )(x, wg, wu, wd, h, mods, g)


N_ASSIGN = 2 * N_LAT
N_SORTED = N_ASSIGN + N_EXPERTS * MLP_ROW_TILE
N_SORTED_TILES = N_SORTED // MLP_ROW_TILE
ROUTE_TILE = 512
DMA_UNROLL = 8


def _row_copy(src_ref, src_row, dst_ref, dst_row, sem):
    return pltpu.make_async_copy(src_ref.at[pl.ds(src_row, 1), :], dst_ref.at[pl.ds(dst_row, 1), :], sem)


def _dispatch_kernel(ends_ref, d1_ref, d2_ref, u_ref, xs_ref, zero_ref, sem, zero_sem):
    tm = MLP_ROW_TILE

    def zero_tile(e):
        rows = pl.ds(pl.multiple_of(ends_ref[e] - tm, tm), tm)
        return pltpu.make_async_copy(zero_ref, xs_ref.at[rows, :], zero_sem)

    def group_size(e):
        return ends_ref[e] - (ends_ref[e - 1] if e > 0 else 0)

    def tail_tile(k):
        rows = pl.ds(pl.multiple_of(ends_ref[N_EXPERTS - 1] + k * tm, tm), tm)
        return pltpu.make_async_copy(zero_ref, xs_ref.at[rows, :], zero_sem)

    def has_tail(k):
        return ends_ref[N_EXPERTS - 1] + k * tm < N_SORTED

    @pl.when(pl.program_id(0) == 0)
    def _():
        zero_ref[...] = jnp.zeros_like(zero_ref)
        for e in range(N_EXPERTS):
            pl.when(group_size(e) > 0)(lambda e=e: zero_tile(e).start())
            pl.when(has_tail(e))(lambda e=e: tail_tile(e).start())
        for e in range(N_EXPERTS):
            pl.when(group_size(e) > 0)(lambda e=e: zero_tile(e).wait())
            pl.when(has_tail(e))(lambda e=e: tail_tile(e).wait())

    def start(t, _):
        _row_copy(u_ref, t, xs_ref, d1_ref[0, t], sem).start()
        _row_copy(u_ref, t, xs_ref, d2_ref[0, t], sem).start()
        return 0

    def wait(t, _):
        _row_copy(u_ref, t, xs_ref, d1_ref[0, t], sem).wait()
        _row_copy(u_ref, t, xs_ref, d2_ref[0, t], sem).wait()
        return 0

    lax.fori_loop(0, ROUTE_TILE, start, 0, unroll=DMA_UNROLL)
    lax.fori_loop(0, ROUTE_TILE, wait, 0, unroll=DMA_UNROLL)


def _route_spec():
    return pl.BlockSpec((None, 1, ROUTE_TILE), lambda i, *_: (i, 0, 0), memory_space=pltpu.SMEM)


def _dispatch(ends, d1, d2, u):
    tm = ROUTE_TILE
    grid_spec = pltpu.PrefetchScalarGridSpec(
        num_scalar_prefetch=1,
        grid=(N_LAT // tm,),
        in_specs=[_route_spec(), _route_spec(), pl.BlockSpec((tm, D), lambda i, ends: (i, 0))],
        out_specs=pl.BlockSpec(memory_space=pl.ANY),
        scratch_shapes=[pltpu.VMEM((MLP_ROW_TILE, D), F32), pltpu.SemaphoreType.DMA(()),
                        pltpu.SemaphoreType.DMA(())],
    )
    return pl.pallas_call(
        _dispatch_kernel,
        out_shape=jax.ShapeDtypeStruct((N_SORTED, D), F32),
        grid_spec=grid_spec,
        compiler_params=_params(("arbitrary",)),
        name="moe_dispatch",
    )(ends, d1, d2, u)


def _experts_kernel(te_ref, nt_ref, x_ref, wg_ref, wu_ref, wd_ref, o_ref, xb_ref):
    del te_ref
    i, f = pl.program_id(0), pl.program_id(1)
    live = i < nt_ref[0]

    @pl.when(live & (f == 0))
    def _():
        xb_ref[...] = x_ref[...].astype(BF16)

    @pl.when(live)
    def _():
        y = _swiglu(xb_ref[...], wg_ref, wu_ref, wd_ref, EXPERT_TILE, MLP_CHUNK)

        @pl.when(f == 0)
        def _():
            o_ref[...] = y

        @pl.when(f > 0)
        def _():
            o_ref[...] += y

    @pl.when(jnp.logical_not(live) & (f == 0))
    def _():
        o_ref[...] = jnp.zeros_like(o_ref)


def _experts(tile_expert, n_tiles, xs, wg, wu, wd):
    tm = MLP_ROW_TILE
    nf = EXPERT_DIM // EXPERT_TILE

    def f_blk(i, f, nt):
        return jnp.where(i < nt[0], f, nf - 1)

    grid_spec = pltpu.PrefetchScalarGridSpec(
        num_scalar_prefetch=2,
        grid=(N_SORTED_TILES, nf),
        in_specs=[pl.BlockSpec((tm, D), lambda i, f, te, nt: (jnp.minimum(i, nt[0] - 1), 0)),
                  pl.BlockSpec((None, D, EXPERT_TILE), lambda i, f, te, nt: (te[i], 0, f_blk(i, f, nt))),
                  pl.BlockSpec((None, D, EXPERT_TILE), lambda i, f, te, nt: (te[i], 0, f_blk(i, f, nt))),
                  pl.BlockSpec((None, EXPERT_TILE, D), lambda i, f, te, nt: (te[i], f_blk(i, f, nt), 0))],
        out_specs=pl.BlockSpec((tm, D), lambda i, f, te, nt: (i, 0)),
        scratch_shapes=[pltpu.VMEM((tm, D), BF16)],
    )
    return pl.pallas_call(
        _experts_kernel,
        out_shape=jax.ShapeDtypeStruct((N_SORTED, D), F32),
        grid_spec=grid_spec,
        compiler_params=_params(("arbitrary", "arbitrary")),
        name="moe_experts",
    )(tile_expert, n_tiles, xs, wg, wu, wd)


def _combine_kernel(d1_ref, d2_ref, n1_ref, n2_ref, w_ref, h_ref, m_ref, g_ref, ys_ref, o_ref, buf_ref, sems):
    i, n = pl.program_id(0), pl.num_programs(0)
    slot = i % 2

    def gather(a_ref, b_ref, slot, start):
        def body(t, _):
            for choice, d_ref in enumerate((a_ref, b_ref)):
                copy = _row_copy(ys_ref, d_ref[0, t], buf_ref.at[slot, choice], t, sems.at[slot])
                copy.start() if start else copy.wait()
            return 0

        lax.fori_loop(0, ROUTE_TILE, body, 0, unroll=DMA_UNROLL)

    @pl.when(i == 0)
    def _():
        gather(d1_ref, d2_ref, 0, True)

    @pl.when(i + 1 < n)
    def _():
        gather(n1_ref, n2_ref, 1 - slot, True)

    gather(d1_ref, d2_ref, slot, False)
    w = w_ref[...]
    lane = lax.broadcasted_iota(jnp.int32, w.shape, 1)
    w1 = jnp.sum(jnp.where(lane == 0, w, 0.0), axis=-1, keepdims=True)
    w2 = jnp.sum(jnp.where(lane == 1, w, 0.0), axis=-1, keepdims=True)
    y = w1 * buf_ref[slot, 0] + w2 * buf_ref[slot, 1]
    o_ref[...] = h_ref[...] + m_ref[5:6, :] * _rms(y, g_ref[3:4, :])


def _combine(d1, d2, wts, h, mods, g, ys):
    tm = ROUTE_TILE
    n = N_LAT // tm
    row = lambda w: pl.BlockSpec((tm, w), lambda i: (i, 0))
    nxt = pl.BlockSpec((None, 1, tm), lambda i: (jnp.minimum(i + 1, n - 1), 0, 0), memory_space=pltpu.SMEM)
    return pl.pallas_call(
        _combine_kernel,
        out_shape=jax.ShapeDtypeStruct((N_LAT, D), F32),
        grid=(n,),
        in_specs=[_route_spec(), _route_spec(), nxt, nxt, row(LANE), row(D), _mod_spec(1, tm), _full((4, D)),
                  pl.BlockSpec(memory_space=pl.ANY)],
        out_specs=row(D),
        scratch_shapes=[pltpu.VMEM((2, 2, tm, D), F32), pltpu.SemaphoreType.DMA((2,))],
        compiler_params=_params(("arbitrary",)),
        name="moe_combine",
    )(d1, d2, d1, d2, wts, h, mods, g, ys)


def _routing(route, counts):
    tm = MLP_ROW_TILE
    experts = jnp.arange(N_EXPERTS, dtype=jnp.int32)
    padded = (counts[0, :N_EXPERTS].astype(jnp.int32) + tm - 1) // tm * tm
    ends = jnp.cumsum(padded).astype(jnp.int32)
    starts = ends - padded

    def dest(choice, rank):
        return jnp.sum(jnp.where(choice[:, None] == experts, starts[None, :], 0), axis=1) + rank

    d1, d2 = dest(route[:, 0], route[:, 2]), dest(route[:, 1], route[:, 3])
    n_tiles = ends[-1] // tm
    tile_start = jnp.minimum(jnp.arange(N_SORTED_TILES, dtype=jnp.int32), n_tiles - 1) * tm
    tile_expert = jnp.sum((tile_start[:, None] >= ends[None, :]).astype(jnp.int32), axis=1)
    shape = (N_LAT // ROUTE_TILE, 1, ROUTE_TILE)
    return (ends, d1.astype(jnp.int32).reshape(shape), d2.astype(jnp.int32).reshape(shape),
            tile_expert.astype(jnp.int32), n_tiles.astype(jnp.int32).reshape(1))


GLA_Q0, GLA_K0, GLA_V0, GLA_G0, GLA_R0 = 0, 512, 1024, 2048, 3072
GLA_IN_P = GLA_R0 + LANE


def _gla_proj_kernel(h_ref, m_ref, g_ref, w_in_ref, w_vt_ref, q_ref, k_ref, v_ref, vt_ref, sg_ref, r_ref):
    u = _modulated(h_ref[...], g_ref[0:1, :], m_ref[0:1, :], m_ref[1:2, :]).astype(BF16)
    z = _dot(u, w_in_ref[...])
    q_ref[...] = (z[:, GLA_Q0:GLA_K0] * (GLA_DK ** -0.5)).astype(BF16)
    k_ref[...] = z[:, GLA_K0:GLA_V0].astype(BF16)
    v_ref[...] = z[:, GLA_V0:GLA_G0].astype(BF16)
    vt_ref[...] = _dot_nt(w_vt_ref[...], u).astype(BF16)
    sg_ref[...] = _silu(z[:, GLA_G0:GLA_R0]).astype(BF16)
    r_ref[...] = z[:, GLA_R0:].astype(BF16)


def _gla_project(h, mods, g, w):
    tm = ROW_TILE
    row = lambda n, dt=None: pl.BlockSpec((tm, n), lambda i: (i, 0))
    hk, hv = GLA_HEADS * GLA_DK, GLA_HEADS * GLA_DV
    sds = jax.ShapeDtypeStruct
    return pl.pallas_call(
        _gla_proj_kernel,
        out_shape=(sds((N_ROWS, hk), BF16), sds((N_ROWS, hk), BF16), sds((N_ROWS, hv), BF16),
                   sds((hv, N_ROWS), BF16), sds((N_ROWS, hv), BF16), sds((N_ROWS, LANE), BF16)),
        grid=(N_ROWS // tm,),
        in_specs=[row(D), _mod_spec(1, tm), _full((4, D)), _full((D, GLA_IN_P)), _full((hv, D))],
        out_specs=(row(hk), row(hk), row(hv), pl.BlockSpec((hv, tm), lambda i: (0, i)), row(hv), row(LANE)),
        compiler_params=_params(("parallel",)),
        name="gla_project",
    )(h, mods, g, w["w_in"], w["w_vt"])


GLA_LAT_BLOCKS = SEQ // GLA_BLOCK
GLA_CTX_BLOCKS = CTX // GLA_BLOCK
GLA_STRIPS = GLA_BLOCK // GLA_SUB


def _gla_blocks(chains):
    n = GLA_BLOCK
    r = lax.broadcasted_iota(jnp.int32, (n, n), 0)
    c = lax.broadcasted_iota(jnp.int32, (n, n), 1)
    row = lax.broadcasted_iota(jnp.int32, (n, GLA_DK), 0)
    keep = {rev: (c >= r) if rev else (c <= r) for rev in (False, True)}
    tri = {rev: jnp.where(keep[rev], 1.0, 0.0).astype(BF16) for rev in (False, True)}

    cums = []
    for ch in chains:
        parts = _dot(tri[ch["reverse"]], jnp.concatenate(_split_bf16(ch["la"]), axis=1))
        cums.append(parts[:, :GLA_DK] + parts[:, GLA_DK:])

    states, outs = [], []
    for ch, cum in zip(chains, cums):
        total = cum[0:1, :] if ch["reverse"] else cum[n - 1:n, :]
        k_state = (ch["k"] * jnp.exp(total - cum)).astype(BF16)
        states.append(ch["state"] * jnp.exp(total) + _dot(ch["vt"], k_state))
        if ch["q"] is None:
            outs.append(None)
        else:
            outs.append(_dot_nt((ch["q"] * jnp.exp(cum)).astype(BF16), ch["state"].astype(BF16)))

    strips = [[] for _ in chains]
    for i in range(GLA_STRIPS):
        lo, hi = i * GLA_SUB, (i + 1) * GLA_SUB
        for ci, (ch, cum) in enumerate(zip(chains, cums)):
            if ch["q"] is None:
                continue
            if ch["reverse"]:
                ref = cum[hi:hi + 1, :] if i < GLA_STRIPS - 1 else jnp.zeros((1, GLA_DK), F32)
                live = row >= lo
            else:
                ref = cum[lo - 1:lo, :] if i > 0 else jnp.zeros((1, GLA_DK), F32)
                live = row < hi
            q_loc = (ch["q"][lo:hi, :] * jnp.exp(cum[lo:hi, :] - ref)).astype(BF16)
            k_loc = jnp.where(live, ch["k"] * jnp.exp(ref - cum), 0.0).astype(BF16)
            strips[ci].append(_dot_nt(q_loc, k_loc))

    for ci, ch in enumerate(chains):
        if ch["q"] is not None:
            scores = jnp.where(keep[ch["reverse"]], jnp.concatenate(strips[ci], axis=0), 0.0).astype(BF16)
            outs[ci] = outs[ci] + _dot(scores, ch["v"])
    return list(zip(outs, states))


GLA_GROUP = 2


def _gla_kernel(q_ref, kl_ref, kc_ref, vl_ref, vtl_ref, vtc_ref, rl_ref, rc_ref,
                wgf_ref, wgb_ref, bgf_ref, bgb_ref, o_ref, s_ref):
    n = GLA_BLOCK
    s_ref[...] = jnp.zeros_like(s_ref)
    chains = [(hd, rev) for hd in range(GLA_GROUP) for rev in (False, True)]

    def log_decays(r_blk, reverse):
        wg_ref, bg_ref = (wgb_ref, bgb_ref) if reverse else (wgf_ref, bgf_ref)
        zg = _dot(r_blk, wg_ref[...]) + bg_ref[...]
        return (jnp.minimum(zg, 0.0) - jnp.log(1.0 + jnp.exp(-jnp.abs(zg)))) * (1.0 / GLA_GATE_NORM)

    def chain(hd, reverse, rows, la, k_ref, vt_ref, with_output):
        ks = slice(hd * GLA_DK, (hd + 1) * GLA_DK)
        vs = slice(hd * GLA_DV, (hd + 1) * GLA_DV)
        return dict(reverse=reverse, k=k_ref[rows, ks].astype(F32), vt=vt_ref[vs, rows],
                    la=la[:, ks], state=s_ref[int(reverse), hd],
                    q=q_ref[rows, ks].astype(F32) if with_output else None,
                    v=vl_ref[rows, vs] if with_output else None)

    for j in range(GLA_CTX_BLOCKS):
        rows = {rev: slice(blk * n, (blk + 1) * n) for rev, blk in ((False, j), (True, GLA_CTX_BLOCKS - 1 - j))}
        la = {rev: log_decays(rc_ref[rows[rev], :], rev) for rev in (False, True)}
        work = [chain(hd, rev, rows[rev], la[rev], kc_ref, vtc_ref, False) for hd, rev in chains]
        for (hd, reverse), (_, s_new) in zip(chains, _gla_blocks(work)):
            s_ref[int(reverse), hd] = s_new

    def step(j, accumulate):
        rows = {rev: pl.ds(pl.multiple_of(blk * n, n), n) for rev, blk in ((False, j), (True, GLA_LAT_BLOCKS - 1 - j))}
        la = {rev: log_decays(rl_ref[rows[rev], :], rev) for rev in (False, True)}
        work = [chain(hd, rev, rows[rev], la[rev], kl_ref, vtl_ref, True) for hd, rev in chains]
        where = [(rows[rev], slice(hd * GLA_DV, (hd + 1) * GLA_DV)) for hd, rev in chains]
        for (hd, reverse), (rows, vs), (o, s_new) in zip(chains, where, _gla_blocks(work)):
            s_ref[int(reverse), hd] = s_new
            if accumulate:
                o_ref[rows, vs] += o
            else:
                o_ref[rows, vs] = o

    half = GLA_LAT_BLOCKS // 2

    def first(j, _):
        step(j, False)
        return 0

    def second(j, _):
        step(j, True)
        return 0

    lax.fori_loop(0, half, first, 0)
    lax.fori_loop(half, GLA_LAT_BLOCKS, second, 0)


def _gla_scan(q, k, v, vt, r, w_gate_f, w_gate_b, b_gate_f, b_gate_b):
    gk, gv = GLA_GROUP * GLA_DK, GLA_GROUP * GLA_DV
    lat = lambda n: pl.BlockSpec((SEQ, n), lambda b, g: (b, g))
    ctx = lambda n: pl.BlockSpec((CTX, n), lambda b, g: (N_LAT // CTX + b, g))
    grp = lambda rows: pl.BlockSpec((rows, gk), lambda b, g: (0, g))
    return pl.pallas_call(
        _gla_kernel,
        out_shape=jax.ShapeDtypeStruct((N_LAT, GLA_HEADS * GLA_DV), F32),
        grid=(BATCH, GLA_HEADS // GLA_GROUP),
        in_specs=[lat(gk), lat(gk), ctx(gk), lat(gv),
                  pl.BlockSpec((gv, SEQ), lambda b, g: (g, b)),
                  pl.BlockSpec((gv, CTX), lambda b, g: (g, N_LAT // CTX + b)),
                  pl.BlockSpec((SEQ, LANE), lambda b, g: (b, 0)),
                  pl.BlockSpec((CTX, LANE), lambda b, g: (N_LAT // CTX + b, 0)),
                  grp(LANE), grp(LANE), grp(1), grp(1)],
        out_specs=lat(gv),
        scratch_shapes=[pltpu.VMEM((2, GLA_GROUP, GLA_DV, GLA_DK), F32)],
        compiler_params=_params(("parallel", "parallel")),
        name="gla_scan",
    )(q, k, k, v, vt, vt, r, r, w_gate_f, w_gate_b, b_gate_f, b_gate_b)


def _gla_out_kernel(o_ref, sg_ref, on_ref, w_ref, h_ref, m_ref, g_ref, wr_hi_ref, wr_lo_ref, br_ref,
                    h_out_ref, u_ref, route_ref, wts_ref, count_ref):
    parts = []
    for hd in range(GLA_HEADS):
        sl = slice(hd * GLA_DV, (hd + 1) * GLA_DV)
        parts.append(_rms(o_ref[:, sl], on_ref[...]) * sg_ref[:, sl].astype(F32))
    y = _dot(jnp.concatenate(parts, axis=-1).astype(BF16), w_ref[...])
    h_new, u = _mixer_epilogue(y, h_ref[...], m_ref, g_ref)
    h_out_ref[...] = h_new
    u_ref[...] = u
    logits = _dot3(u, wr_hi_ref[...], wr_lo_ref[...]) + br_ref[...]
    lane = lax.broadcasted_iota(jnp.int32, logits.shape, 1)
    l1 = jnp.max(logits, axis=-1, keepdims=True)
    i1 = jnp.min(jnp.where(logits == l1, lane, LANE), axis=-1, keepdims=True)
    rest = jnp.where(lane == i1, -jnp.inf, logits)
    l2 = jnp.max(rest, axis=-1, keepdims=True)
    i2 = jnp.min(jnp.where(rest == l2, lane, LANE), axis=-1, keepdims=True)
    e2 = jnp.exp(l2 - l1)
    wts_ref[...] = jnp.where(lane == 0, 1.0 / (1.0 + e2), jnp.where(lane == 1, e2 / (1.0 + e2), 0.0))

    @pl.when(pl.program_id(0) == 0)
    def _():
        count_ref[...] = jnp.zeros_like(count_ref)

    chosen = jnp.where((lane == i1) | (lane == i2), 1.0, 0.0)
    tm = chosen.shape[0]
    earlier = lax.broadcasted_iota(jnp.int32, (tm, tm), 1) < lax.broadcasted_iota(jnp.int32, (tm, tm), 0)
    rank = count_ref[0:1, :] + _dot(jnp.where(earlier, 1.0, 0.0).astype(BF16), chosen.astype(BF16))
    rank1 = jnp.sum(jnp.where(lane == i1, rank, 0.0), axis=-1, keepdims=True).astype(jnp.int32)
    rank2 = jnp.sum(jnp.where(lane == i2, rank, 0.0), axis=-1, keepdims=True).astype(jnp.int32)
    count_ref[...] = count_ref[...] + jnp.sum(chosen, axis=0, keepdims=True)
    route_ref[...] = jnp.where(lane == 0, i1, jnp.where(lane == 1, i2,
                               jnp.where(lane == 2, rank1, jnp.where(lane == 3, rank2, 0))))


def _gla_output(o, sg, o_norm, w_out, h, mods, g, wr_hi, wr_lo, br):
    tm = ROW_TILE
    row = lambda n: pl.BlockSpec((tm, n), lambda i: (i, 0))
    hv = GLA_HEADS * GLA_DV
    return pl.pallas_call(
        _gla_out_kernel,
        out_shape=(jax.ShapeDtypeStruct((N_LAT, D), F32), jax.ShapeDtypeStruct((N_LAT, D), F32),
                   jax.ShapeDtypeStruct((N_LAT, LANE), jnp.int32), jax.ShapeDtypeStruct((N_LAT, LANE), F32),
                   jax.ShapeDtypeStruct((8, LANE), F32)),
        grid=(N_LAT // tm,),
        in_specs=[row(hv), row(hv), _full((1, GLA_DV)), _full((hv, D)), row(D), _mod_spec(1, tm),
                  _full((4, D)), _full((D, LANE)), _full((D, LANE)), _full((1, LANE))],
        out_specs=(row(D), row(D), row(LANE), row(LANE), _full((8, LANE))),
        compiler_params=_params(("arbitrary",)),
        name="gla_output",
    )(o, sg, o_norm, w_out, h, mods, g, wr_hi, wr_lo, br)


def _rotary_lanes(d):
    half, p = d // 2, d // 4
    lanes = np.empty(d, np.int64)
    for axis in range(2):
        lanes[axis * half:axis * half + p] = axis * p + np.arange(p)
        lanes[axis * half + p:(axis + 1) * half] = LANE // 2 + axis * p + np.arange(p)
    return lanes


def _slot_sources(n_plain, n_rotary):
    src = np.full(LANE, -1, np.int64)
    src[_rotary_lanes(n_rotary)] = n_plain + np.arange(n_rotary)
    free = np.flatnonzero(src < 0)[:n_plain]
    src[free] = np.arange(n_plain)
    return src


def _to_slots(w, n_heads, src):
    k = w.shape[0]
    w = jnp.pad(w.reshape(k, n_heads, -1), ((0, 0), (0, 0), (0, 1)))
    return w[:, :, np.where(src < 0, w.shape[2] - 1, src)].reshape(k, n_heads * LANE)


def _attn_weights(w_in, q_norm, w_uq, kv_norm, w_ukv, qk_norm, w_out):
    c = 0
    cols = {}
    for name, n in (("q_lat", MLA_Q_RANK), ("kv_lat", MLA_KV_RANK), ("k_rope", MLA_ROPE),
                    ("q_b", GQA_HEADS * GQA_HEAD_DIM), ("k_b", GQA_KV_HEADS * GQA_HEAD_DIM),
                    ("v_b", GQA_KV_HEADS * GQA_HEAD_DIM)):
        cols[name] = w_in[:, c:c + n]
        c += n
    src_a = _slot_sources(MLA_NOPE, MLA_ROPE)
    src_a_nope = np.where(src_a < MLA_NOPE, src_a, -1)
    src_a_rope = np.where(src_a >= MLA_NOPE, src_a - MLA_NOPE, -1)
    src_b = _slot_sources(0, GQA_HEAD_DIM)
    w_in_p = jnp.concatenate([cols["q_lat"], cols["kv_lat"], _to_slots(cols["k_rope"], 1, src_a_rope),
                              _to_slots(cols["q_b"], GQA_HEADS, src_b),
                              _to_slots(cols["k_b"], GQA_KV_HEADS, src_b)], axis=1)
    ukv = w_ukv.reshape(MLA_KV_RANK, MLA_HEADS, MLA_NOPE + MLA_V)
    w_uk = _to_slots(ukv[:, :, :MLA_NOPE].reshape(MLA_KV_RANK, -1), MLA_HEADS, src_a_nope)
    w_uv = ukv[:, :, MLA_NOPE:].reshape(MLA_KV_RANK, MLA_HEADS * MLA_V)
    return {
        "w_in": w_in_p.astype(BF16),
        "q_norm": q_norm.reshape(1, -1),
        "w_uq": _to_slots(w_uq, MLA_HEADS, src_a).astype(BF16),
        "kv_norm": kv_norm.reshape(1, -1),
        "w_uk": w_uk.astype(BF16),
        "w_uvt": w_uv.T.astype(BF16),
        "w_vbt": cols["v_b"].T.astype(BF16),
        "qk_norm": _to_slots(qk_norm, 1, src_b),
        "w_out": w_out.astype(BF16),
    }


def _rope_tables():
    t = jnp.arange(SEQ, dtype=jnp.int32)
    out = []
    for d in (MLA_ROPE, GQA_HEAD_DIM):
        half, p = d // 2, d // 4
        freqs = ROPE_THETA ** (-jnp.arange(0, half, 2, dtype=F32) / half)
        ang = jnp.concatenate([pos.astype(F32)[:, None] * freqs[None, :] for pos in (t // GRID_W, t % GRID_W)], axis=1)
        rest = LANE // 2 - 2 * p
        cos = jnp.pad(jnp.cos(ang), ((0, 0), (0, rest)), constant_values=1.0)
        sin = jnp.pad(jnp.sin(ang), ((0, 0), (0, rest)))
        for tbl, fill in ((jnp.concatenate([cos, cos], axis=1), 1.0), (jnp.concatenate([-sin, sin], axis=1), 0.0)):
            out.append(jnp.pad(tbl, ((0, ATTN_PROJ_TILE), (0, 0)), constant_values=fill))
    return out


def _gla_weights(w_in, w_gate2, b_gate):
    hk, hv = GLA_HEADS * GLA_DK, GLA_HEADS * GLA_DV
    r = jnp.pad(w_in[:, 2 * hk + 2 * hv:], ((0, 0), (0, LANE - 2 * GLA_GATE_RANK)))
    w_in_p = jnp.concatenate([w_in[:, :2 * hk + 2 * hv], r], axis=1)
    pad_f = ((0, LANE - GLA_GATE_RANK), (0, 0))
    pad_b = ((GLA_GATE_RANK, LANE - 2 * GLA_GATE_RANK), (0, 0))
    return {
        "w_in": w_in_p.astype(BF16),
        "w_vt": w_in[:, 2 * hk:2 * hk + hv].T.astype(BF16),
        "w_gate_f": jnp.pad(w_gate2[0], pad_f).astype(BF16),
        "w_gate_b": jnp.pad(w_gate2[1], pad_b).astype(BF16),
        "b_gate_f": b_gate[0].reshape(1, hk),
        "b_gate_b": b_gate[1].reshape(1, hk),
    }


def kernel(x, c, ctx, c_ctx, mod_w, mod_b, norm_g, attn_w_in, attn_q_norm, attn_w_uq, attn_kv_norm, attn_w_ukv,
           attn_qk_norm, attn_w_out, gla_w_in, gla_w_gate2, gla_b_gate, gla_o_norm, gla_w_out, ffn_w_gate,
           ffn_w_up, ffn_w_down, moe_w_router, moe_b_router, moe_w_gate, moe_w_up, moe_w_down):
    assert x.shape == (BATCH, SEQ, D) and ctx.shape == (BATCH, CTX, D)
    x, ctx = x.reshape(N_LAT, D), ctx.reshape(N_CTX, D)
    cc = jnp.concatenate([c, c_ctx[None, :], jnp.zeros((MOD_ROWS - BATCH - 1, D), F32)], axis=0)
    mods = _mod_vectors(cc, mod_w, mod_b).reshape(mod_w.shape[0], MOD_ROWS, 6, D)

    aw = _attn_weights(attn_w_in[0], attn_q_norm[0], attn_w_uq[0], attn_kv_norm[0], attn_w_ukv[0],
                       attn_qk_norm[0], attn_w_out[0])
    q, k, vt = _attn_project(x, ctx, mods, norm_g[0], aw, _rope_tables())
    o_lat, o_ctx = _attention(q, k, vt)
    h, u = _attn_output(o_lat, o_ctx, aw["w_out"], x, ctx, mods, norm_g[0])
    h = _ffn(u, ffn_w_gate[0].astype(BF16), ffn_w_up[0].astype(BF16), ffn_w_down[0].astype(BF16), h, mods,
             norm_g[0])

    gw = _gla_weights(gla_w_in[0], gla_w_gate2[0], gla_b_gate[0])
    gq, gk, gv, gvt, sg, gr = _gla_project(h, mods, norm_g[1], gw)
    go = _gla_scan(gq, gk, gv, gvt, gr, gw["w_gate_f"], gw["w_gate_b"], gw["b_gate_f"], gw["b_gate_b"])
    wr = jnp.pad(moe_w_router[0], ((0, 0), (0, LANE - N_EXPERTS)))
    wr_hi = wr.astype(BF16)
    wr_lo = (wr - wr_hi.astype(F32)).astype(BF16)
    br = jnp.pad(moe_b_router[0], (0, LANE - N_EXPERTS), constant_values=-jnp.inf).reshape(1, LANE)
    h, u, route, wts, counts = _gla_output(go, sg, gla_o_norm[0].reshape(1, GLA_DV), gla_w_out[0].astype(BF16),
                                           h, mods, norm_g[1], wr_hi, wr_lo, br)
    ends, d1, d2, tile_expert, n_tiles = _routing(route, counts)
    xs = _dispatch(ends, d1, d2, u)
    ys = _experts(tile_expert, n_tiles, xs, moe_w_gate[0].astype(BF16), moe_w_up[0].astype(BF16),
                  moe_w_down[0].astype(BF16))
    h = _combine(d1, d2, wts, h, mods, norm_g[1], ys)
    return h.reshape(BATCH, SEQ, D)
```

```python
import jax
import jax.numpy as jnp
import numpy as np
from jax import lax
from jax.experimental import pallas as pl
from jax.experimental.pallas import tpu as pltpu

F32 = jnp.float32
BF16 = jnp.bfloat16

D = 1024
BATCH = 4
SEQ = 4096
CTX = 256
GRID_W = 64
ROPE_THETA = 10000.0
EPS = 1e-6

N_LAT = BATCH * SEQ
N_CTX = BATCH * CTX
N_ROWS = N_LAT + N_CTX
MOD_ROWS = 8
CTX_MOD_ROW = BATCH

LANE = 128
ROW_TILE = 512
ATTN_PROJ_TILE = 256

MLA_HEADS = 8
MLA_Q_RANK = 384
MLA_KV_RANK = 256
MLA_NOPE = 64
MLA_ROPE = 32
MLA_V = 64
GQA_HEADS = 8
GQA_KV_HEADS = 2
GQA_GROUP = GQA_HEADS // GQA_KV_HEADS
GQA_HEAD_DIM = 64
N_Q_HEADS = MLA_HEADS + GQA_HEADS

GLA_HEADS = 4
GLA_DK = 128
GLA_DV = 256
GLA_GATE_RANK = 16
GLA_GATE_NORM = 16.0
GLA_BLOCK = 128
GLA_SUB = 32

MXU_WIDTH = 256
FFN_DIM = 2816
N_EXPERTS = 8
EXPERT_DIM = 3584
EXPERT_TILE = EXPERT_DIM // 2
MLP_CHUNK = 2 * MXU_WIDTH
MLP_ROW_TILE = 512

VMEM_LIMIT = 56 * 1024 * 1024


def _params(sem):
    return pltpu.CompilerParams(dimension_semantics=sem, vmem_limit_bytes=VMEM_LIMIT)


def _rms(x, g):
    return x * lax.rsqrt(jnp.mean(x * x, axis=-1, keepdims=True) + EPS) * g


def _silu(x):
    return x / (1.0 + jnp.exp(-x))


def _split_bf16(x):
    hi = x.astype(BF16)
    lo = (x - hi.astype(F32)).astype(BF16)
    return hi, lo


def _dot(a, b):
    return jnp.dot(a, b, preferred_element_type=F32)


def _dot_nt(a, b):
    return lax.dot_general(a, b, (((1,), (1,)), ((), ())), preferred_element_type=F32)


def _dot3(a, b_hi, b_lo):
    a_hi, a_lo = _split_bf16(a)
    return _dot(a_hi, b_hi) + (_dot(a_hi, b_lo) + _dot(a_lo, b_hi))


def _modulated(h, g_row, shift, scale):
    return _rms(h, g_row) * (1.0 + scale) + shift


def _rope(x, cos, sin):
    return x * cos + pltpu.roll(x, LANE // 2, 1) * sin


def _mod_row(i, tile):
    r0 = i * tile
    return jnp.where(r0 < N_LAT, r0 // SEQ, CTX_MOD_ROW)


def _mod_spec(layer, tile):
    return pl.BlockSpec((None, None, 6, D), lambda i, *_: (layer, _mod_row(i, tile), 0, 0))


def _rope_block(i):
    r0 = i * ATTN_PROJ_TILE
    return jnp.where(r0 < N_LAT, (r0 % SEQ) // ATTN_PROJ_TILE, SEQ // ATTN_PROJ_TILE)


def _full(shape):
    return pl.BlockSpec(shape, lambda *_: (0,) * len(shape))


def _mod_kernel(c_ref, w_ref, b_ref, o_ref):
    w_hi, w_lo = _split_bf16(w_ref[...])
    o_ref[...] = _dot3(_silu(c_ref[...]), w_hi, w_lo) + b_ref[...]


def _mod_vectors(cc, mod_w, mod_b):
    depth, _, n = mod_w.shape
    tn = 1536
    return pl.pallas_call(
        _mod_kernel,
        out_shape=jax.ShapeDtypeStruct((depth, MOD_ROWS, n), F32),
        grid=(depth, n // tn),
        in_specs=[
            pl.BlockSpec((MOD_ROWS, D), lambda l, j: (0, 0)),
            pl.BlockSpec((None, D, tn), lambda l, j: (l, 0, j)),
            pl.BlockSpec((None, 1, tn), lambda l, j: (l, 0, j)),
        ],
        out_specs=pl.BlockSpec((None, MOD_ROWS, tn), lambda l, j: (l, 0, j)),
        compiler_params=_params(("parallel", "parallel")),
        name="mod_vectors",
    )(cc, mod_w, mod_b.reshape(depth, 1, n))


Q_LAT0, KV_LAT0, K_ROPE0 = 0, 384, 640
Q_B0 = 768
K_B0 = Q_B0 + GQA_HEADS * LANE
ATTN_IN_P = K_B0 + GQA_KV_HEADS * LANE
ATTN_V = 64
ATTN_VA = ATTN_V + 16
LOG2E = 1.4426950408889634


def _input_rows(x_ref, c_ref):
    return jnp.where(pl.program_id(0) < N_LAT // x_ref.shape[0], x_ref[...], c_ref[...])


def _input_specs(tm):
    n_lat = N_LAT // tm
    return [pl.BlockSpec((tm, D), lambda i: (jnp.minimum(i, n_lat - 1), 0)),
            pl.BlockSpec((tm, D), lambda i: (jnp.maximum(i - n_lat, 0), 0))]


def _attn_proj_kernel(x_ref, c_ref, m_ref, g_ref, w_in_ref, qn_ref, w_uq_ref, kvn_ref, w_uk_ref, w_uvt_ref,
                      w_vbt_ref, qkn_ref, ca_ref, sa_ref, cb_ref, sb_ref, q_ref, k_ref, vt_ref):
    u = _modulated(_input_rows(x_ref, c_ref), g_ref[0:1, :], m_ref[0:1, :], m_ref[1:2, :]).astype(BF16)
    z = _dot(u, w_in_ref[...])
    q_lat = _rms(z[:, Q_LAT0:Q_LAT0 + MLA_Q_RANK], qn_ref[...]).astype(BF16)
    kv_lat = _rms(z[:, KV_LAT0:KV_LAT0 + MLA_KV_RANK], kvn_ref[...]).astype(BF16)
    q_a = _dot(q_lat, w_uq_ref[...])
    k_a = _dot(kv_lat, w_uk_ref[...])
    ca, sa, cb, sb = ca_ref[...], sa_ref[...], cb_ref[...], sb_ref[...]
    scale_a = (MLA_NOPE + MLA_ROPE) ** -0.5 * LOG2E
    scale_b = GQA_HEAD_DIM ** -0.5 * LOG2E
    k_rope = _rope(z[:, K_ROPE0:K_ROPE0 + LANE], ca, sa)
    for hd in range(MLA_HEADS):
        sl = slice(hd * LANE, (hd + 1) * LANE)
        q_ref[:, sl] = (_rope(q_a[:, sl], ca, sa) * scale_a).astype(BF16)
        k_ref[:, sl] = (k_a[:, sl] + k_rope).astype(BF16)

    def head_norm(x, gain):
        ms = jnp.sum(x * x, axis=-1, keepdims=True) * (1.0 / GQA_HEAD_DIM)
        return x * lax.rsqrt(ms + EPS) * gain

    for hd in range(GQA_HEADS):
        x = head_norm(z[:, Q_B0 + hd * LANE:Q_B0 + (hd + 1) * LANE], qkn_ref[0:1, :])
        sl = slice((MLA_HEADS + hd) * LANE, (MLA_HEADS + hd + 1) * LANE)
        q_ref[:, sl] = (_rope(x, cb, sb) * scale_b).astype(BF16)
    for kv in range(GQA_KV_HEADS):
        x = head_norm(z[:, K_B0 + kv * LANE:K_B0 + (kv + 1) * LANE], qkn_ref[1:2, :])
        x = _rope(x, cb, sb).astype(BF16)
        for hd in range(kv * GQA_GROUP, (kv + 1) * GQA_GROUP):
            k_ref[:, (MLA_HEADS + hd) * LANE:(MLA_HEADS + hd + 1) * LANE] = x

    vat = _dot_nt(w_uvt_ref[...], kv_lat).astype(BF16)
    vbt = _dot_nt(w_vbt_ref[...], u).astype(BF16)
    ones = jnp.ones((ATTN_VA - ATTN_V, vat.shape[1]), BF16)
    for hd in range(N_Q_HEADS):
        if hd < MLA_HEADS:
            v_t = vat[hd * ATTN_V:(hd + 1) * ATTN_V, :]
        else:
            kv = (hd - MLA_HEADS) // GQA_GROUP
            v_t = vbt[kv * ATTN_V:(kv + 1) * ATTN_V, :]
        vt_ref[hd * ATTN_VA:hd * ATTN_VA + ATTN_V, :] = v_t
        vt_ref[hd * ATTN_VA + ATTN_V:(hd + 1) * ATTN_VA, :] = ones


def _attn_project(x, ctx, mods, g, w, tables):
    tm = ATTN_PROJ_TILE
    row = lambda n: pl.BlockSpec((tm, n), lambda i: (i, 0))
    tab = pl.BlockSpec((tm, LANE), lambda i: (_rope_block(i), 0))
    return pl.pallas_call(
        _attn_proj_kernel,
        out_shape=(jax.ShapeDtypeStruct((N_ROWS, N_Q_HEADS * LANE), BF16),
                   jax.ShapeDtypeStruct((N_ROWS, N_Q_HEADS * LANE), BF16),
                   jax.ShapeDtypeStruct((N_Q_HEADS * ATTN_VA, N_ROWS), BF16)),
        grid=(N_ROWS // tm,),
        in_specs=[*_input_specs(tm), _mod_spec(0, tm), _full((4, D)), _full((D, ATTN_IN_P)),
                  _full((1, MLA_Q_RANK)), _full((MLA_Q_RANK, MLA_HEADS * LANE)),
                  _full((1, MLA_KV_RANK)), _full((MLA_KV_RANK, MLA_HEADS * LANE)),
                  _full((MLA_HEADS * ATTN_V, MLA_KV_RANK)), _full((GQA_KV_HEADS * ATTN_V, D)),
                  _full((2, LANE)), tab, tab, tab, tab],
        out_specs=(row(N_Q_HEADS * LANE), row(N_Q_HEADS * LANE),
                   pl.BlockSpec((N_Q_HEADS * ATTN_VA, tm), lambda i: (0, i))),
        compiler_params=_params(("parallel",)),
        name="attn_project",
    )(x, ctx, mods, g, w["w_in"], w["q_norm"], w["w_uq"], w["kv_norm"], w["w_uk"], w["w_uvt"], w["w_vbt"],
      w["qk_norm"], *tables)


ATT_TQ = 256
ATT_TK = 1024
ATT_LAT_TILES = SEQ // ATT_TQ
ATT_GROUP = 4


def _attn_kernel(ql_ref, qc_ref, kc_ref, kl_ref, vtc_ref, vtl_ref, ol_ref, oc_ref, sa_ref, sb_ref):
    heads = range(ATT_GROUP)
    n_lat = SEQ // ATT_TK
    every = slice(None)
    init = (jnp.full((1, ATT_TQ), -jnp.inf, F32), jnp.zeros((ATTN_VA, ATT_TQ), F32))

    def lat_queries(t, hd):
        return ql_ref[pl.ds(pl.multiple_of(t * ATT_TQ, ATT_TQ), ATT_TQ), hd * LANE:(hd + 1) * LANE]

    def lat_keys(j):
        return slice(j * ATT_TK, (j + 1) * ATT_TK)

    def scores(s_ref, hd, q, k_ref, keys, n_keys):
        s = _dot_nt(k_ref[keys, hd * LANE:(hd + 1) * LANE], q)
        s_ref[hd, 0:n_keys, :] = s
        return jnp.max(s, axis=0, keepdims=True)

    def consume(s_ref, hd, vt_ref, keys, n_keys, cmax, carry):
        m, acc = carry
        m_new = jnp.maximum(m, cmax)
        p = jnp.exp2(s_ref[hd, 0:n_keys, :] - m_new).astype(BF16)
        acc = jnp.exp2(m - m_new) * acc + _dot(vt_ref[hd * ATTN_VA:(hd + 1) * ATTN_VA, keys], p)
        return m_new, acc

    def normalised(carries):
        o_t = jnp.concatenate([acc[0:ATTN_V, :] / acc[ATTN_V:ATTN_V + 1, :] for _, acc in carries], axis=0)
        return o_t.T.astype(BF16)

    def query_tile(t, t_next, first_ref, second_ref, cmax):
        bufs = (first_ref, second_ref)
        carries = (init,) * ATT_GROUP
        for c in range(n_lat + 1):
            cur, nxt = bufs[c % 2], bufs[(c + 1) % 2]
            cmax_next, out = [], []
            for hd in heads:
                if c + 1 < n_lat:
                    cmax_next.append(scores(nxt, hd, lat_queries(t, hd), kl_ref, lat_keys(c + 1), ATT_TK))
                elif c + 1 == n_lat:
                    cmax_next.append(scores(nxt, hd, lat_queries(t, hd), kc_ref, every, CTX))
                else:
                    cmax_next.append(scores(nxt, hd, lat_queries(t_next, hd), kl_ref, lat_keys(0), ATT_TK))
                if c < n_lat:
                    out.append(consume(cur, hd, vtl_ref, lat_keys(c), ATT_TK, cmax[hd], carries[hd]))
                else:
                    out.append(consume(cur, hd, vtc_ref, every, CTX, cmax[hd], carries[hd]))
            cmax, carries = cmax_next, out
        ol_ref[pl.ds(pl.multiple_of(t * ATT_TQ, ATT_TQ), ATT_TQ), :] = normalised(carries)
        return tuple(cmax)

    def tile_pair(i, cmax):
        cmax = query_tile(2 * i, 2 * i + 1, sa_ref, sb_ref, cmax)
        return query_tile(2 * i + 1, jnp.minimum(2 * i + 2, ATT_LAT_TILES - 1), sb_ref, sa_ref, cmax)

    cmax0 = tuple(scores(sa_ref, hd, lat_queries(0, hd), kl_ref, lat_keys(0), ATT_TK) for hd in heads)
    lax.fori_loop(0, ATT_LAT_TILES // 2, tile_pair, cmax0)

    cmax_c = [scores(sa_ref, hd, qc_ref[:, hd * LANE:(hd + 1) * LANE], kc_ref, every, CTX) for hd in heads]
    oc_ref[...] = normalised([consume(sa_ref, hd, vtc_ref, every, CTX, cmax_c[hd], init) for hd in heads])


def _attention(q, k, vt):
    assert (SEQ // ATT_TK) % 2 == 0 and ATT_LAT_TILES % 2 == 0 and CTX == ATT_TQ
    gw = ATT_GROUP * LANE
    gv = ATT_GROUP * ATTN_V
    gva = ATT_GROUP * ATTN_VA
    ctx_blk = N_LAT // CTX
    return pl.pallas_call(
        _attn_kernel,
        out_shape=(jax.ShapeDtypeStruct((N_LAT, N_Q_HEADS * ATTN_V), BF16),
                   jax.ShapeDtypeStruct((N_CTX, N_Q_HEADS * ATTN_V), BF16)),
        grid=(BATCH, N_Q_HEADS // ATT_GROUP),
        in_specs=[pl.BlockSpec((SEQ, gw), lambda b, g: (b, g)),
                  pl.BlockSpec((CTX, gw), lambda b, g: (ctx_blk + b, g)),
                  pl.BlockSpec((CTX, gw), lambda b, g: (ctx_blk + b, g)),
                  pl.BlockSpec((SEQ, gw), lambda b, g: (b, g)),
                  pl.BlockSpec((gva, CTX), lambda b, g: (g, ctx_blk + b)),
                  pl.BlockSpec((gva, SEQ), lambda b, g: (g, b))],
        out_specs=(pl.BlockSpec((SEQ, gv), lambda b, g: (b, g)), pl.BlockSpec((CTX, gv), lambda b, g: (b, g))),
        scratch_shapes=[pltpu.VMEM((ATT_GROUP, ATT_TK, ATT_TQ), F32)] * 2,
        compiler_params=_params(("parallel", "parallel")),
        name="attention",
    )(q, q, k, k, vt, vt)


def _mixer_epilogue(y, h, m_ref, g_ref):
    h_new = h + m_ref[2:3, :] * _rms(y, g_ref[1:2, :])
    u = _modulated(h_new, g_ref[2:3, :], m_ref[3:4, :], m_ref[4:5, :])
    return h_new, u


def _attn_out_kernel(ol_ref, oc_ref, w_ref, x_ref, c_ref, m_ref, g_ref, h_out_ref, u_ref):
    y = _dot(_input_rows(ol_ref, oc_ref), w_ref[...])
    h_new, u = _mixer_epilogue(y, _input_rows(x_ref, c_ref), m_ref, g_ref)
    h_out_ref[...] = h_new
    u_ref[...] = u.astype(BF16)


def _attn_output(o_lat, o_ctx, w_out, x, ctx, mods, g):
    assert N_Q_HEADS * ATTN_V == D
    tm = ROW_TILE
    row = lambda n: pl.BlockSpec((tm, n), lambda i: (i, 0))
    return pl.pallas_call(
        _attn_out_kernel,
        out_shape=(jax.ShapeDtypeStruct((N_ROWS, D), F32), jax.ShapeDtypeStruct((N_ROWS, D), BF16)),
        grid=(N_ROWS // tm,),
        in_specs=[*_input_specs(tm), _full((D, D)), *_input_specs(tm), _mod_spec(0, tm), _full((4, D))],
        out_specs=(row(D), row(D)),
        compiler_params=_params(("parallel",)),
        name="attn_output",
    )(o_lat, o_ctx, w_out, x, ctx, mods, g)


def _swiglu(x, wg_ref, wu_ref, wd_ref, f_dim, chunk):
    y = None
    for c0 in range(0, f_dim, chunk):
        cols = slice(c0, min(c0 + chunk, f_dim))
        a = _silu(_dot(x, wg_ref[:, cols])) * _dot(x, wu_ref[:, cols])
        part = _dot(a.astype(BF16), wd_ref[cols, :])
        y = part if y is None else y + part
    return y


def _ffn_kernel(x_ref, wg_ref, wu_ref, wd_ref, h_ref, m_ref, g_ref, o_ref):
    y = _swiglu(x_ref[...], wg_ref, wu_ref, wd_ref, FFN_DIM, MLP_CHUNK)
    o_ref[...] = h_ref[...] + m_ref[5:6, :] * _rms(y, g_ref[3:4, :])


def _resident(shape):
    return pl.BlockSpec(shape, lambda *_: (0,) * len(shape), pipeline_mode=pl.Buffered(1))


def _ffn(x, wg, wu, wd, h, mods, g):
    tm = MLP_ROW_TILE
    row = lambda n: pl.BlockSpec((tm, n), lambda i: (i, 0))
    return pl.pallas_call(
        _ffn_kernel,
        out_shape=jax.ShapeDtypeStruct((N_ROWS, D), F32),
        grid=(N_ROWS // tm,),
        in_specs=[row(D), _resident((D, FFN_DIM)), _resident((D, FFN_DIM)), _resident((FFN_DIM, D)),
                  row(D), _mod_spec(0, tm), _full((4, D))],
        out_specs=row(D),
        compiler_params=_params(("parallel",)),
        name="ffn_mlp",
    )(x, wg, wu, wd, h, mods, g)


N_ASSIGN = 2 * N_LAT
N_SORTED = N_ASSIGN + N_EXPERTS * MLP_ROW_TILE
N_SORTED_TILES = N_SORTED // MLP_ROW_TILE
ROUTE_TILE = 512
DMA_UNROLL = 8
ROUTE_ROWS = 8


def _row_copy(src_ref, src_row, dst_ref, dst_row, sem):
    return pltpu.make_async_copy(src_ref.at[pl.ds(src_row, 1), :], dst_ref.at[pl.ds(dst_row, 1), :], sem)


def _dispatch_kernel(ends_ref, d1_ref, d2_ref, u_ref, xs_ref, zero_ref, sem, zero_sem):
    tm = MLP_ROW_TILE

    def zero_tile(e):
        rows = pl.ds(pl.multiple_of(ends_ref[e] - tm, tm), tm)
        return pltpu.make_async_copy(zero_ref, xs_ref.at[rows, :], zero_sem)

    def group_size(e):
        return ends_ref[e] - (ends_ref[e - 1] if e > 0 else 0)

    def tail_tile(k):
        rows = pl.ds(pl.multiple_of(ends_ref[N_EXPERTS - 1] + k * tm, tm), tm)
        return pltpu.make_async_copy(zero_ref, xs_ref.at[rows, :], zero_sem)

    def has_tail(k):
        return ends_ref[N_EXPERTS - 1] + k * tm < N_SORTED

    @pl.when(pl.program_id(0) == 0)
    def _():
        zero_ref[...] = jnp.zeros_like(zero_ref)
        for e in range(N_EXPERTS):
            pl.when(group_size(e) > 0)(lambda e=e: zero_tile(e).start())
            pl.when(has_tail(e))(lambda e=e: tail_tile(e).start())
        for e in range(N_EXPERTS):
            pl.when(group_size(e) > 0)(lambda e=e: zero_tile(e).wait())
            pl.when(has_tail(e))(lambda e=e: tail_tile(e).wait())

    def start(t, _):
        _row_copy(u_ref, t, xs_ref, d1_ref[0, t], sem).start()
        _row_copy(u_ref, t, xs_ref, d2_ref[0, t], sem).start()
        return 0

    def wait(t, _):
        _row_copy(u_ref, t, xs_ref, d1_ref[0, t], sem).wait()
        _row_copy(u_ref, t, xs_ref, d2_ref[0, t], sem).wait()
        return 0

    lax.fori_loop(0, ROUTE_TILE, start, 0, unroll=DMA_UNROLL)
    lax.fori_loop(0, ROUTE_TILE, wait, 0, unroll=DMA_UNROLL)


def _route_spec():
    return pl.BlockSpec((None, 1, ROUTE_TILE), lambda i, *_: (i, 0, 0), memory_space=pltpu.SMEM)


def _dispatch(ends, d1, d2, u):
    tm = ROUTE_TILE
    grid_spec = pltpu.PrefetchScalarGridSpec(
        num_scalar_prefetch=1,
        grid=(N_LAT // tm,),
        in_specs=[_route_spec(), _route_spec(), pl.BlockSpec((tm, D), lambda i, ends: (i, 0))],
        out_specs=pl.BlockSpec(memory_space=pl.ANY),
        scratch_shapes=[pltpu.VMEM((MLP_ROW_TILE, D), F32), pltpu.SemaphoreType.DMA(()),
                        pltpu.SemaphoreType.DMA(())],
    )
    return pl.pallas_call(
        _dispatch_kernel,
        out_shape=jax.ShapeDtypeStruct((N_SORTED, D), F32),
        grid_spec=grid_spec,
        compiler_params=_params(("arbitrary",)),
        name="moe_dispatch",
    )(ends, d1, d2, u)


def _experts_kernel(te_ref, nt_ref, x_ref, wg_ref, wu_ref, wd_ref, o_ref, xb_ref):
    del te_ref
    i, f = pl.program_id(0), pl.program_id(1)
    live = i < nt_ref[0]

    @pl.when(live & (f == 0))
    def _():
        xb_ref[...] = x_ref[...].astype(BF16)

    @pl.when(live)
    def _():
        y = _swiglu(xb_ref[...], wg_ref, wu_ref, wd_ref, EXPERT_TILE, MLP_CHUNK)

        @pl.when(f == 0)
        def _():
            o_ref[...] = y

        @pl.when(f > 0)
        def _():
            o_ref[...] += y

    @pl.when(jnp.logical_not(live) & (f == 0))
    def _():
        o_ref[...] = jnp.zeros_like(o_ref)


def _experts(tile_expert, n_tiles, xs, wg, wu, wd):
    tm = MLP_ROW_TILE
    nf = EXPERT_DIM // EXPERT_TILE

    def f_blk(i, f, nt):
        return jnp.where(i < nt[0], f, nf - 1)

    grid_spec = pltpu.PrefetchScalarGridSpec(
        num_scalar_prefetch=2,
        grid=(N_SORTED_TILES, nf),
        in_specs=[pl.BlockSpec((tm, D), lambda i, f, te, nt: (jnp.minimum(i, nt[0] - 1), 0)),
                  pl.BlockSpec((None, D, EXPERT_TILE), lambda i, f, te, nt: (te[i], 0, f_blk(i, f, nt))),
                  pl.BlockSpec((None, D, EXPERT_TILE), lambda i, f, te, nt: (te[i], 0, f_blk(i, f, nt))),
                  pl.BlockSpec((None, EXPERT_TILE, D), lambda i, f, te, nt: (te[i], f_blk(i, f, nt), 0))],
        out_specs=pl.BlockSpec((tm, D), lambda i, f, te, nt: (i, 0)),
        scratch_shapes=[pltpu.VMEM((tm, D), BF16)],
    )
    return pl.pallas_call(
        _experts_kernel,
        out_shape=jax.ShapeDtypeStruct((N_SORTED, D), F32),
        grid_spec=grid_spec,
        compiler_params=_params(("arbitrary", "arbitrary")),
        name="moe_experts",
    )(tile_expert, n_tiles, xs, wg, wu, wd)


def _combine_kernel(d1_ref, d2_ref, n1_ref, n2_ref, w_ref, h_ref, m_ref, g_ref, ys_ref, o_ref, buf_ref, sems):
    i, n = pl.program_id(0), pl.num_programs(0)
    slot = i % 2

    def gather(a_ref, b_ref, slot, start):
        def body(t, _):
            for choice, d_ref in enumerate((a_ref, b_ref)):
                copy = _row_copy(ys_ref, d_ref[0, t], buf_ref.at[slot, choice], t, sems.at[slot])
                copy.start() if start else copy.wait()
            return 0

        lax.fori_loop(0, ROUTE_TILE, body, 0, unroll=DMA_UNROLL)

    @pl.when(i == 0)
    def _():
        gather(d1_ref, d2_ref, 0, True)

    @pl.when(i + 1 < n)
    def _():
        gather(n1_ref, n2_ref, 1 - slot, True)

    gather(d1_ref, d2_ref, slot, False)
    w = w_ref[...]
    lane = lax.broadcasted_iota(jnp.int32, w.shape, 1)
    w1 = jnp.sum(jnp.where(lane == 0, w, 0.0), axis=-1, keepdims=True)
    w2 = jnp.sum(jnp.where(lane == 1, w, 0.0), axis=-1, keepdims=True)
    y = w1 * buf_ref[slot, 0] + w2 * buf_ref[slot, 1]
    o_ref[...] = h_ref[...] + m_ref[5:6, :] * _rms(y, g_ref[3:4, :])


def _combine(d1, d2, wts, h, mods, g, ys):
    tm = ROUTE_TILE
    n = N_LAT // tm
    row = lambda w: pl.BlockSpec((tm, w), lambda i: (i, 0))
    nxt = pl.BlockSpec((None, 1, tm), lambda i: (jnp.minimum(i + 1, n - 1), 0, 0), memory_space=pltpu.SMEM)
    return pl.pallas_call(
        _combine_kernel,
        out_shape=jax.ShapeDtypeStruct((N_LAT, D), F32),
        grid=(n,),
        in_specs=[_route_spec(), _route_spec(), nxt, nxt, row(LANE), row(D), _mod_spec(1, tm), _full((4, D)),
                  pl.BlockSpec(memory_space=pl.ANY)],
        out_specs=row(D),
        scratch_shapes=[pltpu.VMEM((2, 2, tm, D), F32), pltpu.SemaphoreType.DMA((2,))],
        compiler_params=_params(("arbitrary",)),
        name="moe_combine",
    )(d1, d2, d1, d2, wts, h, mods, g, ys)


def _routing(route, counts):
    tm = MLP_ROW_TILE
    experts = jnp.arange(N_EXPERTS, dtype=jnp.int32)
    padded = (counts[0, :N_EXPERTS].astype(jnp.int32) + tm - 1) // tm * tm
    ends = jnp.cumsum(padded).astype(jnp.int32)
    starts = ends - padded

    def dest(choice, rank):
        return jnp.sum(jnp.where(choice[:, None] == experts, starts[None, :], 0), axis=1) + rank

    d1, d2 = dest(route[0], route[2]), dest(route[1], route[3])
    n_tiles = ends[-1] // tm
    tile_start = jnp.minimum(jnp.arange(N_SORTED_TILES, dtype=jnp.int32), n_tiles - 1) * tm
    tile_expert = jnp.sum((tile_start[:, None] >= ends[None, :]).astype(jnp.int32), axis=1)
    shape = (N_LAT // ROUTE_TILE, 1, ROUTE_TILE)
    return (ends, d1.astype(jnp.int32).reshape(shape), d2.astype(jnp.int32).reshape(shape),
            tile_expert.astype(jnp.int32), n_tiles.astype(jnp.int32).reshape(1))


GLA_Q0, GLA_K0, GLA_V0, GLA_G0, GLA_R0 = 0, 512, 1024, 2048, 3072
GLA_IN_P = GLA_R0 + LANE


def _gla_proj_kernel(h_ref, m_ref, g_ref, w_in_ref, w_vt_ref, q_ref, k_ref, v_ref, vt_ref, sg_ref, r_ref):
    u = _modulated(h_ref[...], g_ref[0:1, :], m_ref[0:1, :], m_ref[1:2, :]).astype(BF16)
    z = _dot(u, w_in_ref[...])
    q_ref[...] = (z[:, GLA_Q0:GLA_K0] * (GLA_DK ** -0.5)).astype(BF16)
    k_ref[...] = z[:, GLA_K0:GLA_V0].astype(BF16)
    v_ref[...] = z[:, GLA_V0:GLA_G0].astype(BF16)
    vt_ref[...] = _dot_nt(w_vt_ref[...], u).astype(BF16)
    sg_ref[...] = _silu(z[:, GLA_G0:GLA_R0]).astype(BF16)
    r_ref[...] = z[:, GLA_R0:].astype(BF16)


def _gla_project(h, mods, g, w):
    tm = ROW_TILE
    row = lambda n, dt=None: pl.BlockSpec((tm, n), lambda i: (i, 0))
    hk, hv = GLA_HEADS * GLA_DK, GLA_HEADS * GLA_DV
    sds = jax.ShapeDtypeStruct
    return pl.pallas_call(
        _gla_proj_kernel,
        out_shape=(sds((N_ROWS, hk), BF16), sds((N_ROWS, hk), BF16), sds((N_ROWS, hv), BF16),
                   sds((hv, N_ROWS), BF16), sds((N_ROWS, hv), BF16), sds((N_ROWS, LANE), BF16)),
        grid=(N_ROWS // tm,),
        in_specs=[row(D), _mod_spec(1, tm), _full((4, D)), _full((D, GLA_IN_P)), _full((hv, D))],
        out_specs=(row(hk), row(hk), row(hv), pl.BlockSpec((hv, tm), lambda i: (0, i)), row(hv), row(LANE)),
        compiler_params=_params(("parallel",)),
        name="gla_project",
    )(h, mods, g, w["w_in"], w["w_vt"])


GLA_LAT_BLOCKS = SEQ // GLA_BLOCK
GLA_CTX_BLOCKS = CTX // GLA_BLOCK
GLA_STRIPS = GLA_BLOCK // GLA_SUB


def _gla_blocks(chains):
    n = GLA_BLOCK
    r = lax.broadcasted_iota(jnp.int32, (n, n), 0)
    c = lax.broadcasted_iota(jnp.int32, (n, n), 1)
    row = lax.broadcasted_iota(jnp.int32, (n, GLA_DK), 0)
    keep = {rev: (c >= r) if rev else (c <= r) for rev in (False, True)}
    tri = {rev: jnp.where(keep[rev], 1.0, 0.0).astype(BF16) for rev in (False, True)}

    cums = []
    for ch in chains:
        parts = _dot(tri[ch["reverse"]], jnp.concatenate(_split_bf16(ch["la"]), axis=1))
        cums.append(parts[:, :GLA_DK] + parts[:, GLA_DK:])

    states, outs = [], []
    for ch, cum in zip(chains, cums):
        total = cum[0:1, :] if ch["reverse"] else cum[n - 1:n, :]
        k_state = (ch["k"] * jnp.exp(total - cum)).astype(BF16)
        states.append(ch["state"] * jnp.exp(total) + _dot(ch["vt"], k_state))
        if ch["q"] is None:
            outs.append(None)
        else:
            outs.append(_dot_nt((ch["q"] * jnp.exp(cum)).astype(BF16), ch["state"].astype(BF16)))

    strips = [[] for _ in chains]
    for i in range(GLA_STRIPS):
        lo, hi = i * GLA_SUB, (i + 1) * GLA_SUB
        for ci, (ch, cum) in enumerate(zip(chains, cums)):
            if ch["q"] is None:
                continue
            if ch["reverse"]:
                ref = cum[hi:hi + 1, :] if i < GLA_STRIPS - 1 else jnp.zeros((1, GLA_DK), F32)
                live = row >= lo
            else:
                ref = cum[lo - 1:lo, :] if i > 0 else jnp.zeros((1, GLA_DK), F32)
                live = row < hi
            q_loc = (ch["q"][lo:hi, :] * jnp.exp(cum[lo:hi, :] - ref)).astype(BF16)
            k_loc = jnp.where(live, ch["k"] * jnp.exp(ref - cum), 0.0).astype(BF16)
            strips[ci].append(_dot_nt(q_loc, k_loc))

    for ci, ch in enumerate(chains):
        if ch["q"] is not None:
            scores = jnp.where(keep[ch["reverse"]], jnp.concatenate(strips[ci], axis=0), 0.0).astype(BF16)
            outs[ci] = outs[ci] + _dot(scores, ch["v"])
    return list(zip(outs, states))


GLA_GROUP = 2


def _gla_kernel(q_ref, kl_ref, kc_ref, vl_ref, vtl_ref, vtc_ref, rl_ref, rc_ref,
                wgf_ref, wgb_ref, bgf_ref, bgb_ref, o_ref, s_ref):
    n = GLA_BLOCK
    s_ref[...] = jnp.zeros_like(s_ref)
    chains = [(hd, rev) for hd in range(GLA_GROUP) for rev in (False, True)]

    def log_decays(r_blk, reverse):
        wg_ref, bg_ref = (wgb_ref, bgb_ref) if reverse else (wgf_ref, bgf_ref)
        zg = _dot(r_blk, wg_ref[...]) + bg_ref[...]
        return (jnp.minimum(zg, 0.0) - jnp.log(1.0 + jnp.exp(-jnp.abs(zg)))) * (1.0 / GLA_GATE_NORM)

    def chain(hd, reverse, rows, la, k_ref, vt_ref, with_output):
        ks = slice(hd * GLA_DK, (hd + 1) * GLA_DK)
        vs = slice(hd * GLA_DV, (hd + 1) * GLA_DV)
        return dict(reverse=reverse, k=k_ref[rows, ks].astype(F32), vt=vt_ref[vs, rows],
                    la=la[:, ks], state=s_ref[int(reverse), hd],
                    q=q_ref[rows, ks].astype(F32) if with_output else None,
                    v=vl_ref[rows, vs] if with_output else None)

    for j in range(GLA_CTX_BLOCKS):
        rows = {rev: slice(blk * n, (blk + 1) * n) for rev, blk in ((False, j), (True, GLA_CTX_BLOCKS - 1 - j))}
        la = {rev: log_decays(rc_ref[rows[rev], :], rev) for rev in (False, True)}
        work = [chain(hd, rev, rows[rev], la[rev], kc_ref, vtc_ref, False) for hd, rev in chains]
        for (hd, reverse), (_, s_new) in zip(chains, _gla_blocks(work)):
            s_ref[int(reverse), hd] = s_new

    def step(j, accumulate):
        rows = {rev: pl.ds(pl.multiple_of(blk * n, n), n) for rev, blk in ((False, j), (True, GLA_LAT_BLOCKS - 1 - j))}
        la = {rev: log_decays(rl_ref[rows[rev], :], rev) for rev in (False, True)}
        work = [chain(hd, rev, rows[rev], la[rev], kl_ref, vtl_ref, True) for hd, rev in chains]
        where = [(rows[rev], slice(hd * GLA_DV, (hd + 1) * GLA_DV)) for hd, rev in chains]
        for (hd, reverse), (rows, vs), (o, s_new) in zip(chains, where, _gla_blocks(work)):
            s_ref[int(reverse), hd] = s_new
            if accumulate:
                o_ref[rows, vs] += o
            else:
                o_ref[rows, vs] = o

    half = GLA_LAT_BLOCKS // 2

    def first(j, _):
        step(j, False)
        return 0

    def second(j, _):
        step(j, True)
        return 0

    lax.fori_loop(0, half, first, 0)
    lax.fori_loop(half, GLA_LAT_BLOCKS, second, 0)


def _gla_scan(q, k, v, vt, r, w_gate_f, w_gate_b, b_gate_f, b_gate_b):
    gk, gv = GLA_GROUP * GLA_DK, GLA_GROUP * GLA_DV
    lat = lambda n: pl.BlockSpec((SEQ, n), lambda b, g: (b, g))
    ctx = lambda n: pl.BlockSpec((CTX, n), lambda b, g: (N_LAT // CTX + b, g))
    grp = lambda rows: pl.BlockSpec((rows, gk), lambda b, g: (0, g))
    return pl.pallas_call(
        _gla_kernel,
        out_shape=jax.ShapeDtypeStruct((N_LAT, GLA_HEADS * GLA_DV), F32),
        grid=(BATCH, GLA_HEADS // GLA_GROUP),
        in_specs=[lat(gk), lat(gk), ctx(gk), lat(gv),
                  pl.BlockSpec((gv, SEQ), lambda b, g: (g, b)),
                  pl.BlockSpec((gv, CTX), lambda b, g: (g, N_LAT // CTX + b)),
                  pl.BlockSpec((SEQ, LANE), lambda b, g: (b, 0)),
                  pl.BlockSpec((CTX, LANE), lambda b, g: (N_LAT // CTX + b, 0)),
                  grp(LANE), grp(LANE), grp(1), grp(1)],
        out_specs=lat(gv),
        scratch_shapes=[pltpu.VMEM((2, GLA_GROUP, GLA_DV, GLA_DK), F32)],
        compiler_params=_params(("parallel", "parallel")),
        name="gla_scan",
    )(q, k, k, v, vt, vt, r, r, w_gate_f, w_gate_b, b_gate_f, b_gate_b)


def _gla_out_kernel(o_ref, sg_ref, on_ref, w_ref, h_ref, m_ref, g_ref, wr_hi_ref, wr_lo_ref, br_ref,
                    h_out_ref, u_ref, route_ref, wts_ref, count_ref):
    parts = []
    for hd in range(GLA_HEADS):
        sl = slice(hd * GLA_DV, (hd + 1) * GLA_DV)
        parts.append(_rms(o_ref[:, sl], on_ref[...]) * sg_ref[:, sl].astype(F32))
    y = _dot(jnp.concatenate(parts, axis=-1).astype(BF16), w_ref[...])
    h_new, u = _mixer_epilogue(y, h_ref[...], m_ref, g_ref)
    h_out_ref[...] = h_new
    u_ref[...] = u
    logits = _dot3(u, wr_hi_ref[...], wr_lo_ref[...]) + br_ref[...]
    lane = lax.broadcasted_iota(jnp.int32, logits.shape, 1)
    l1 = jnp.max(logits, axis=-1, keepdims=True)
    i1 = jnp.min(jnp.where(logits == l1, lane, LANE), axis=-1, keepdims=True)
    rest = jnp.where(lane == i1, -jnp.inf, logits)
    l2 = jnp.max(rest, axis=-1, keepdims=True)
    i2 = jnp.min(jnp.where(rest == l2, lane, LANE), axis=-1, keepdims=True)
    e2 = jnp.exp(l2 - l1)
    wts_ref[...] = jnp.where(lane == 0, 1.0 / (1.0 + e2), jnp.where(lane == 1, e2 / (1.0 + e2), 0.0))

    @pl.when(pl.program_id(0) == 0)
    def _():
        count_ref[...] = jnp.zeros_like(count_ref)

    chosen = jnp.where((lane == i1) | (lane == i2), 1.0, 0.0)
    tm = chosen.shape[0]
    earlier = lax.broadcasted_iota(jnp.int32, (tm, tm), 1) < lax.broadcasted_iota(jnp.int32, (tm, tm), 0)
    rank = count_ref[0:1, :] + _dot(jnp.where(earlier, 1.0, 0.0).astype(BF16), chosen.astype(BF16))
    rank1 = jnp.sum(jnp.where(lane == i1, rank, 0.0), axis=-1, keepdims=True).astype(jnp.int32)
    rank2 = jnp.sum(jnp.where(lane == i2, rank, 0.0), axis=-1, keepdims=True).astype(jnp.int32)
    count_ref[...] = count_ref[...] + jnp.sum(chosen, axis=0, keepdims=True)
    route = jnp.where(lane == 0, i1, jnp.where(lane == 1, i2,
                      jnp.where(lane == 2, rank1, jnp.where(lane == 3, rank2, 0))))
    route_ref[...] = route.T[0:ROUTE_ROWS, :]


def _gla_output(o, sg, o_norm, w_out, h, mods, g, wr_hi, wr_lo, br):
    tm = ROW_TILE
    row = lambda n: pl.BlockSpec((tm, n), lambda i: (i, 0))
    hv = GLA_HEADS * GLA_DV
    return pl.pallas_call(
        _gla_out_kernel,
        out_shape=(jax.ShapeDtypeStruct((N_LAT, D), F32), jax.ShapeDtypeStruct((N_LAT, D), F32),
                   jax.ShapeDtypeStruct((ROUTE_ROWS, N_LAT), jnp.int32), jax.ShapeDtypeStruct((N_LAT, LANE), F32),
                   jax.ShapeDtypeStruct((8, LANE), F32)),
        grid=(N_LAT // tm,),
        in_specs=[row(hv), row(hv), _full((1, GLA_DV)), _full((hv, D)), row(D), _mod_spec(1, tm),
                  _full((4, D)), _full((D, LANE)), _full((D, LANE)), _full((1, LANE))],
        out_specs=(row(D), row(D), pl.BlockSpec((ROUTE_ROWS, tm), lambda i: (0, i)), row(LANE),
                   _full((8, LANE))),
        compiler_params=_params(("arbitrary",)),
        name="gla_output",
    )(o, sg, o_norm, w_out, h, mods, g, wr_hi, wr_lo, br)


def _rotary_lanes(d):
    half, p = d // 2, d // 4
    lanes = np.empty(d, np.int64)
    for axis in range(2):
        lanes[axis * half:axis * half + p] = axis * p + np.arange(p)
        lanes[axis * half + p:(axis + 1) * half] = LANE // 2 + axis * p + np.arange(p)
    return lanes


def _slot_sources(n_plain, n_rotary):
    src = np.full(LANE, -1, np.int64)
    src[_rotary_lanes(n_rotary)] = n_plain + np.arange(n_rotary)
    free = np.flatnonzero(src < 0)[:n_plain]
    src[free] = np.arange(n_plain)
    return src


def _to_slots(w, n_heads, src):
    k = w.shape[0]
    w = jnp.pad(w.reshape(k, n_heads, -1), ((0, 0), (0, 0), (0, 1)))
    return w[:, :, np.where(src < 0, w.shape[2] - 1, src)].reshape(k, n_heads * LANE)


def _attn_weights(w_in, q_norm, w_uq, kv_norm, w_ukv, qk_norm, w_out):
    c = 0
    cols = {}
    for name, n in (("q_lat", MLA_Q_RANK), ("kv_lat", MLA_KV_RANK), ("k_rope", MLA_ROPE),
                    ("q_b", GQA_HEADS * GQA_HEAD_DIM), ("k_b", GQA_KV_HEADS * GQA_HEAD_DIM),
                    ("v_b", GQA_KV_HEADS * GQA_HEAD_DIM)):
        cols[name] = w_in[:, c:c + n]
        c += n
    src_a = _slot_sources(MLA_NOPE, MLA_ROPE)
    src_a_nope = np.where(src_a < MLA_NOPE, src_a, -1)
    src_a_rope = np.where(src_a >= MLA_NOPE, src_a - MLA_NOPE, -1)
    src_b = _slot_sources(0, GQA_HEAD_DIM)
    w_in_p = jnp.concatenate([cols["q_lat"], cols["kv_lat"], _to_slots(cols["k_rope"], 1, src_a_rope),
                              _to_slots(cols["q_b"], GQA_HEADS, src_b),
                              _to_slots(cols["k_b"], GQA_KV_HEADS, src_b)], axis=1)
    ukv = w_ukv.reshape(MLA_KV_RANK, MLA_HEADS, MLA_NOPE + MLA_V)
    w_uk = _to_slots(ukv[:, :, :MLA_NOPE].reshape(MLA_KV_RANK, -1), MLA_HEADS, src_a_nope)
    w_uv = ukv[:, :, MLA_NOPE:].reshape(MLA_KV_RANK, MLA_HEADS * MLA_V)
    return {
        "w_in": w_in_p.astype(BF16),
        "q_norm": q_norm.reshape(1, -1),
        "w_uq": _to_slots(w_uq, MLA_HEADS, src_a).astype(BF16),
        "kv_norm": kv_norm.reshape(1, -1),
        "w_uk": w_uk.astype(BF16),
        "w_uvt": w_uv.T.astype(BF16),
        "w_vbt": cols["v_b"].T.astype(BF16),
        "qk_norm": _to_slots(qk_norm, 1, src_b),
        "w_out": w_out.astype(BF16),
    }


def _rope_tables():
    t = jnp.arange(SEQ, dtype=jnp.int32)
    out = []
    for d in (MLA_ROPE, GQA_HEAD_DIM):
        half, p = d // 2, d // 4
        freqs = ROPE_THETA ** (-jnp.arange(0, half, 2, dtype=F32) / half)
        ang = jnp.concatenate([pos.astype(F32)[:, None] * freqs[None, :] for pos in (t // GRID_W, t % GRID_W)], axis=1)
        rest = LANE // 2 - 2 * p
        cos = jnp.pad(jnp.cos(ang), ((0, 0), (0, rest)), constant_values=1.0)
        sin = jnp.pad(jnp.sin(ang), ((0, 0), (0, rest)))
        for tbl, fill in ((jnp.concatenate([cos, cos], axis=1), 1.0), (jnp.concatenate([-sin, sin], axis=1), 0.0)):
            out.append(jnp.pad(tbl, ((0, ATTN_PROJ_TILE), (0, 0)), constant_values=fill))
    return out


def _gla_weights(w_in, w_gate2, b_gate):
    hk, hv = GLA_HEADS * GLA_DK, GLA_HEADS * GLA_DV
    r = jnp.pad(w_in[:, 2 * hk + 2 * hv:], ((0, 0), (0, LANE - 2 * GLA_GATE_RANK)))
    w_in_p = jnp.concatenate([w_in[:, :2 * hk + 2 * hv], r], axis=1)
    pad_f = ((0, LANE - GLA_GATE_RANK), (0, 0))
    pad_b = ((GLA_GATE_RANK, LANE - 2 * GLA_GATE_RANK), (0, 0))
    return {
        "w_in": w_in_p.astype(BF16),
        "w_vt": w_in[:, 2 * hk:2 * hk + hv].T.astype(BF16),
        "w_gate_f": jnp.pad(w_gate2[0], pad_f).astype(BF16),
        "w_gate_b": jnp.pad(w_gate2[1], pad_b).astype(BF16),
        "b_gate_f": b_gate[0].reshape(1, hk),
        "b_gate_b": b_gate[1].reshape(1, hk),
    }


def kernel(x, c, ctx, c_ctx, mod_w, mod_b, norm_g, attn_w_in, attn_q_norm, attn_w_uq, attn_kv_norm, attn_w_ukv,
           attn_qk_norm, attn_w_out, gla_w_in, gla_w_gate2, gla_b_gate, gla_o_norm, gla_w_out, ffn_w_gate,
           ffn_w_up, ffn_w_down, moe_w_router, moe_b_router, moe_w_gate, moe_w_up, moe_w_down):
    assert x.shape == (BATCH, SEQ, D) and ctx.shape == (BATCH, CTX, D)
    x, ctx = x.reshape(N_LAT, D), ctx.reshape(N_CTX, D)
    cc = jnp.concatenate([c, c_ctx[None, :], jnp.zeros((MOD_ROWS - BATCH - 1, D), F32)], axis=0)
    mods = _mod_vectors(cc, mod_w, mod_b).reshape(mod_w.shape[0], MOD_ROWS, 6, D)

    aw = _attn_weights(attn_w_in[0], attn_q_norm[0], attn_w_uq[0], attn_kv_norm[0], attn_w_ukv[0],
                       attn_qk_norm[0], attn_w_out[0])
    q, k, vt = _attn_project(x, ctx, mods, norm_g[0], aw, _rope_tables())
    o_lat, o_ctx = _attention(q, k, vt)
    h, u = _attn_output(o_lat, o_ctx, aw["w_out"], x, ctx, mods, norm_g[0])
    h = _ffn(u, ffn_w_gate[0].astype(BF16), ffn_w_up[0].astype(BF16), ffn_w_down[0].astype(BF16), h, mods,
             norm_g[0])

    gw = _gla_weights(gla_w_in[0], gla_w_gate2[0], gla_b_gate[0])
    gq, gk, gv, gvt, sg, gr = _gla_project(h, mods, norm_g[1], gw)
    go = _gla_scan(gq, gk, gv, gvt, gr, gw["w_gate_f"], gw["w_gate_b"], gw["b_gate_f"], gw["b_gate_b"])
    wr = jnp.pad(moe_w_router[0], ((0, 0), (0, LANE - N_EXPERTS)))
    wr_hi = wr.astype(BF16)
    wr_lo = (wr - wr_hi.astype(F32)).astype(BF16)
    br = jnp.pad(moe_b_router[0], (0, LANE - N_EXPERTS), constant_values=-jnp.inf).reshape(1, LANE)
    h, u, route, wts, counts = _gla_output(go, sg, gla_o_norm[0].reshape(1, GLA_DV), gla_w_out[0].astype(BF16),
                                           h, mods, norm_g[1], wr_hi, wr_lo, br)
    ends, d1, d2, tile_expert, n_tiles = _routing(route, counts)
    xs = _dispatch(ends, d1, d2, u)
    ys = _experts(tile_expert, n_tiles, xs, moe_w_gate[0].astype(BF16), moe_w_up[0].astype(BF16),
                  moe_w_down[0].astype(BF16))
    h = _combine(d1, d2, wts, h, mods, norm_g[1], ys)
    return h.reshape(BATCH, SEQ, D)
```

```python
import jax
import jax.numpy as jnp
import numpy as np
from jax import lax
from jax.experimental import pallas as pl
from jax.experimental.pallas import tpu as pltpu

F32 = jnp.float32
BF16 = jnp.bfloat16

D = 1024
BATCH = 4
SEQ = 4096
CTX = 256
GRID_W = 64
ROPE_THETA = 10000.0
EPS = 1e-6

N_LAT = BATCH * SEQ
N_CTX = BATCH * CTX
N_ROWS = N_LAT + N_CTX
MOD_ROWS = 8
CTX_MOD_ROW = BATCH

LANE = 128
ROW_TILE = 512
ATTN_PROJ_TILE = 256

MLA_HEADS = 8
MLA_Q_RANK = 384
MLA_KV_RANK = 256
MLA_NOPE = 64
MLA_ROPE = 32
MLA_V = 64
GQA_HEADS = 8
GQA_KV_HEADS = 2
GQA_GROUP = GQA_HEADS // GQA_KV_HEADS
GQA_HEAD_DIM = 64
N_Q_HEADS = MLA_HEADS + GQA_HEADS

GLA_HEADS = 4
GLA_DK = 128
GLA_DV = 256
GLA_GATE_RANK = 16
GLA_GATE_NORM = 16.0
GLA_BLOCK = 128
GLA_SUB = 32

MXU_WIDTH = 256
FFN_DIM = 2816
N_EXPERTS = 8
EXPERT_DIM = 3584
EXPERT_TILE = EXPERT_DIM // 2
MLP_CHUNK = 2 * MXU_WIDTH
MLP_ROW_TILE = 512

VMEM_LIMIT = 56 * 1024 * 1024


def _params(sem):
    return pltpu.CompilerParams(dimension_semantics=sem, vmem_limit_bytes=VMEM_LIMIT)


def _rms(x, g):
    return x * lax.rsqrt(jnp.mean(x * x, axis=-1, keepdims=True) + EPS) * g


def _silu(x):
    return x / (1.0 + jnp.exp(-x))


def _split_bf16(x):
    hi = x.astype(BF16)
    lo = (x - hi.astype(F32)).astype(BF16)
    return hi, lo


def _dot(a, b):
    return jnp.dot(a, b, preferred_element_type=F32)


def _dot_nt(a, b):
    return lax.dot_general(a, b, (((1,), (1,)), ((), ())), preferred_element_type=F32)


def _dot3(a, b_hi, b_lo):
    a_hi, a_lo = _split_bf16(a)
    return _dot(a_hi, b_hi) + (_dot(a_hi, b_lo) + _dot(a_lo, b_hi))


def _modulated(h, g_row, shift, scale):
    return _rms(h, g_row) * (1.0 + scale) + shift


def _rope(x, cos, sin):
    return x * cos + pltpu.roll(x, LANE // 2, 1) * sin


def _mod_row(i, tile):
    r0 = i * tile
    return jnp.where(r0 < N_LAT, r0 // SEQ, CTX_MOD_ROW)


def _mod_spec(layer, tile):
    return pl.BlockSpec((None, None, 6, D), lambda i, *_: (layer, _mod_row(i, tile), 0, 0))


def _rope_block(i):
    r0 = i * ATTN_PROJ_TILE
    return jnp.where(r0 < N_LAT, (r0 % SEQ) // ATTN_PROJ_TILE, SEQ // ATTN_PROJ_TILE)


def _full(shape):
    return pl.BlockSpec(shape, lambda *_: (0,) * len(shape))


def _mod_kernel(c_ref, w_ref, b_ref, o_ref):
    w_hi, w_lo = _split_bf16(w_ref[...])
    o_ref[...] = _dot3(_silu(c_ref[...]), w_hi, w_lo) + b_ref[...]


def _mod_vectors(cc, mod_w, mod_b):
    depth, _, n = mod_w.shape
    tn = 1536
    return pl.pallas_call(
        _mod_kernel,
        out_shape=jax.ShapeDtypeStruct((depth, MOD_ROWS, n), F32),
        grid=(depth, n // tn),
        in_specs=[
            pl.BlockSpec((MOD_ROWS, D), lambda l, j: (0, 0)),
            pl.BlockSpec((None, D, tn), lambda l, j: (l, 0, j)),
            pl.BlockSpec((None, 1, tn), lambda l, j: (l, 0, j)),
        ],
        out_specs=pl.BlockSpec((None, MOD_ROWS, tn), lambda l, j: (l, 0, j)),
        compiler_params=_params(("parallel", "parallel")),
        name="mod_vectors",
    )(cc, mod_w, mod_b.reshape(depth, 1, n))


Q_LAT0, KV_LAT0, K_ROPE0 = 0, 384, 640
Q_B0 = 768
K_B0 = Q_B0 + GQA_HEADS * LANE
ATTN_IN_P = K_B0 + GQA_KV_HEADS * LANE
ATTN_V = 64
ATTN_VA = ATTN_V + 16
LOG2E = 1.4426950408889634


def _input_rows(x_ref, c_ref):
    return jnp.where(pl.program_id(0) < N_LAT // x_ref.shape[0], x_ref[...], c_ref[...])


def _input_specs(tm):
    n_lat = N_LAT // tm
    return [pl.BlockSpec((tm, D), lambda i: (jnp.minimum(i, n_lat - 1), 0)),
            pl.BlockSpec((tm, D), lambda i: (jnp.maximum(i - n_lat, 0), 0))]


def _attn_proj_kernel(x_ref, c_ref, m_ref, g_ref, w_in_ref, qn_ref, w_uq_ref, kvn_ref, w_uk_ref, w_uvt_ref,
                      w_vbt_ref, qkn_ref, ca_ref, sa_ref, cb_ref, sb_ref, q_ref, k_ref, vt_ref):
    u = _modulated(_input_rows(x_ref, c_ref), g_ref[0:1, :], m_ref[0:1, :], m_ref[1:2, :]).astype(BF16)
    z = _dot(u, w_in_ref[...])
    q_lat = _rms(z[:, Q_LAT0:Q_LAT0 + MLA_Q_RANK], qn_ref[...]).astype(BF16)
    kv_lat = _rms(z[:, KV_LAT0:KV_LAT0 + MLA_KV_RANK], kvn_ref[...]).astype(BF16)
    q_a = _dot(q_lat, w_uq_ref[...])
    k_a = _dot(kv_lat, w_uk_ref[...])
    ca, sa, cb, sb = ca_ref[...], sa_ref[...], cb_ref[...], sb_ref[...]
    scale_a = (MLA_NOPE + MLA_ROPE) ** -0.5 * LOG2E
    scale_b = GQA_HEAD_DIM ** -0.5 * LOG2E
    k_rope = _rope(z[:, K_ROPE0:K_ROPE0 + LANE], ca, sa)
    for hd in range(MLA_HEADS):
        sl = slice(hd * LANE, (hd + 1) * LANE)
        q_ref[:, sl] = (_rope(q_a[:, sl], ca, sa) * scale_a).astype(BF16)
        k_ref[:, sl] = (k_a[:, sl] + k_rope).astype(BF16)

    def head_norm(x, gain):
        ms = jnp.sum(x * x, axis=-1, keepdims=True) * (1.0 / GQA_HEAD_DIM)
        return x * lax.rsqrt(ms + EPS) * gain

    for hd in range(GQA_HEADS):
        x = head_norm(z[:, Q_B0 + hd * LANE:Q_B0 + (hd + 1) * LANE], qkn_ref[0:1, :])
        sl = slice((MLA_HEADS + hd) * LANE, (MLA_HEADS + hd + 1) * LANE)
        q_ref[:, sl] = (_rope(x, cb, sb) * scale_b).astype(BF16)
    for kv in range(GQA_KV_HEADS):
        x = head_norm(z[:, K_B0 + kv * LANE:K_B0 + (kv + 1) * LANE], qkn_ref[1:2, :])
        x = _rope(x, cb, sb).astype(BF16)
        for hd in range(kv * GQA_GROUP, (kv + 1) * GQA_GROUP):
            k_ref[:, (MLA_HEADS + hd) * LANE:(MLA_HEADS + hd + 1) * LANE] = x

    vat = _dot_nt(w_uvt_ref[...], kv_lat).astype(BF16)
    vbt = _dot_nt(w_vbt_ref[...], u).astype(BF16)
    ones = jnp.ones((ATTN_VA - ATTN_V, vat.shape[1]), BF16)
    for hd in range(N_Q_HEADS):
        if hd < MLA_HEADS:
            v_t = vat[hd * ATTN_V:(hd + 1) * ATTN_V, :]
        else:
            kv = (hd - MLA_HEADS) // GQA_GROUP
            v_t = vbt[kv * ATTN_V:(kv + 1) * ATTN_V, :]
        vt_ref[hd * ATTN_VA:hd * ATTN_VA + ATTN_V, :] = v_t
        vt_ref[hd * ATTN_VA + ATTN_V:(hd + 1) * ATTN_VA, :] = ones


def _attn_project(x, ctx, mods, g, w, tables):
    tm = ATTN_PROJ_TILE
    row = lambda n: pl.BlockSpec((tm, n), lambda i: (i, 0))
    tab = pl.BlockSpec((tm, LANE), lambda i: (_rope_block(i), 0))
    return pl.pallas_call(
        _attn_proj_kernel,
        out_shape=(jax.ShapeDtypeStruct((N_ROWS, N_Q_HEADS * LANE), BF16),
                   jax.ShapeDtypeStruct((N_ROWS, N_Q_HEADS * LANE), BF16),
                   jax.ShapeDtypeStruct((N_Q_HEADS * ATTN_VA, N_ROWS), BF16)),
        grid=(N_ROWS // tm,),
        in_specs=[*_input_specs(tm), _mod_spec(0, tm), _full((4, D)), _full((D, ATTN_IN_P)),
                  _full((1, MLA_Q_RANK)), _full((MLA_Q_RANK, MLA_HEADS * LANE)),
                  _full((1, MLA_KV_RANK)), _full((MLA_KV_RANK, MLA_HEADS * LANE)),
                  _full((MLA_HEADS * ATTN_V, MLA_KV_RANK)), _full((GQA_KV_HEADS * ATTN_V, D)),
                  _full((2, LANE)), tab, tab, tab, tab],
        out_specs=(row(N_Q_HEADS * LANE), row(N_Q_HEADS * LANE),
                   pl.BlockSpec((N_Q_HEADS * ATTN_VA, tm), lambda i: (0, i))),
        compiler_params=_params(("parallel",)),
        name="attn_project",
    )(x, ctx, mods, g, w["w_in"], w["q_norm"], w["w_uq"], w["kv_norm"], w["w_uk"], w["w_uvt"], w["w_vbt"],
      w["qk_norm"], *tables)


ATT_TQ = 256
ATT_TK = 1024
ATT_LAT_TILES = SEQ // ATT_TQ
ATT_GROUP = 4


def _attn_kernel(ql_ref, qc_ref, kc_ref, kl_ref, vtc_ref, vtl_ref, ol_ref, oc_ref, sa_ref, sb_ref):
    heads = range(ATT_GROUP)
    n_lat = SEQ // ATT_TK
    every = slice(None)
    init = (jnp.full((1, ATT_TQ), -jnp.inf, F32), jnp.zeros((ATTN_VA, ATT_TQ), F32))

    def lat_queries(t, hd):
        return ql_ref[pl.ds(pl.multiple_of(t * ATT_TQ, ATT_TQ), ATT_TQ), hd * LANE:(hd + 1) * LANE]

    def lat_keys(j):
        return slice(j * ATT_TK, (j + 1) * ATT_TK)

    def scores(s_ref, hd, q, k_ref, keys, n_keys):
        s = _dot_nt(k_ref[keys, hd * LANE:(hd + 1) * LANE], q)
        s_ref[hd, 0:n_keys, :] = s
        return jnp.max(s, axis=0, keepdims=True)

    def consume(s_ref, hd, vt_ref, keys, n_keys, cmax, carry):
        m, acc = carry
        m_new = jnp.maximum(m, cmax)
        p = jnp.exp2(s_ref[hd, 0:n_keys, :] - m_new).astype(BF16)
        acc = jnp.exp2(m - m_new) * acc + _dot(vt_ref[hd * ATTN_VA:(hd + 1) * ATTN_VA, keys], p)
        return m_new, acc

    def normalised(carries):
        o_t = jnp.concatenate([acc[0:ATTN_V, :] / acc[ATTN_V:ATTN_V + 1, :] for _, acc in carries], axis=0)
        return o_t.T.astype(BF16)

    def query_tile(t, t_next, first_ref, second_ref, cmax):
        bufs = (first_ref, second_ref)
        carries = (init,) * ATT_GROUP
        for c in range(n_lat + 1):
            cur, nxt = bufs[c % 2], bufs[(c + 1) % 2]
            cmax_next, out = [], []
            for hd in heads:
                if c + 1 < n_lat:
                    cmax_next.append(scores(nxt, hd, lat_queries(t, hd), kl_ref, lat_keys(c + 1), ATT_TK))
                elif c + 1 == n_lat:
                    cmax_next.append(scores(nxt, hd, lat_queries(t, hd), kc_ref, every, CTX))
                else:
                    cmax_next.append(scores(nxt, hd, lat_queries(t_next, hd), kl_ref, lat_keys(0), ATT_TK))
                if c < n_lat:
                    out.append(consume(cur, hd, vtl_ref, lat_keys(c), ATT_TK, cmax[hd], carries[hd]))
                else:
                    out.append(consume(cur, hd, vtc_ref, every, CTX, cmax[hd], carries[hd]))
            cmax, carries = cmax_next, out
        ol_ref[pl.ds(pl.multiple_of(t * ATT_TQ, ATT_TQ), ATT_TQ), :] = normalised(carries)
        return tuple(cmax)

    def tile_pair(i, cmax):
        cmax = query_tile(2 * i, 2 * i + 1, sa_ref, sb_ref, cmax)
        return query_tile(2 * i + 1, jnp.minimum(2 * i + 2, ATT_LAT_TILES - 1), sb_ref, sa_ref, cmax)

    cmax0 = tuple(scores(sa_ref, hd, lat_queries(0, hd), kl_ref, lat_keys(0), ATT_TK) for hd in heads)
    lax.fori_loop(0, ATT_LAT_TILES // 2, tile_pair, cmax0)

    cmax_c = [scores(sa_ref, hd, qc_ref[:, hd * LANE:(hd + 1) * LANE], kc_ref, every, CTX) for hd in heads]
    oc_ref[...] = normalised([consume(sa_ref, hd, vtc_ref, every, CTX, cmax_c[hd], init) for hd in heads])


def _attention(q, k, vt):
    assert (SEQ // ATT_TK) % 2 == 0 and ATT_LAT_TILES % 2 == 0 and CTX == ATT_TQ
    gw = ATT_GROUP * LANE
    gv = ATT_GROUP * ATTN_V
    gva = ATT_GROUP * ATTN_VA
    ctx_blk = N_LAT // CTX
    return pl.pallas_call(
        _attn_kernel,
        out_shape=(jax.ShapeDtypeStruct((N_LAT, N_Q_HEADS * ATTN_V), BF16),
                   jax.ShapeDtypeStruct((N_CTX, N_Q_HEADS * ATTN_V), BF16)),
        grid=(BATCH, N_Q_HEADS // ATT_GROUP),
        in_specs=[pl.BlockSpec((SEQ, gw), lambda b, g: (b, g)),
                  pl.BlockSpec((CTX, gw), lambda b, g: (ctx_blk + b, g)),
                  pl.BlockSpec((CTX, gw), lambda b, g: (ctx_blk + b, g)),
                  pl.BlockSpec((SEQ, gw), lambda b, g: (b, g)),
                  pl.BlockSpec((gva, CTX), lambda b, g: (g, ctx_blk + b)),
                  pl.BlockSpec((gva, SEQ), lambda b, g: (g, b))],
        out_specs=(pl.BlockSpec((SEQ, gv), lambda b, g: (b, g)), pl.BlockSpec((CTX, gv), lambda b, g: (b, g))),
        scratch_shapes=[pltpu.VMEM((ATT_GROUP, ATT_TK, ATT_TQ), F32)] * 2,
        compiler_params=_params(("parallel", "parallel")),
        name="attention",
    )(q, q, k, k, vt, vt)


def _mixer_epilogue(y, h, m_ref, g_ref):
    h_new = h + m_ref[2:3, :] * _rms(y, g_ref[1:2, :])
    u = _modulated(h_new, g_ref[2:3, :], m_ref[3:4, :], m_ref[4:5, :])
    return h_new, u


def _attn_out_kernel(ol_ref, oc_ref, w_ref, x_ref, c_ref, m_ref, g_ref, h_out_ref, u_ref):
    y = _dot(_input_rows(ol_ref, oc_ref), w_ref[...])
    h_new, u = _mixer_epilogue(y, _input_rows(x_ref, c_ref), m_ref, g_ref)
    h_out_ref[...] = h_new
    u_ref[...] = u.astype(BF16)


def _attn_output(o_lat, o_ctx, w_out, x, ctx, mods, g):
    assert N_Q_HEADS * ATTN_V == D
    tm = ROW_TILE
    row = lambda n: pl.BlockSpec((tm, n), lambda i: (i, 0))
    return pl.pallas_call(
        _attn_out_kernel,
        out_shape=(jax.ShapeDtypeStruct((N_ROWS, D), F32), jax.ShapeDtypeStruct((N_ROWS, D), BF16)),
        grid=(N_ROWS // tm,),
        in_specs=[*_input_specs(tm), _full((D, D)), *_input_specs(tm), _mod_spec(0, tm), _full((4, D))],
        out_specs=(row(D), row(D)),
        compiler_params=_params(("parallel",)),
        name="attn_output",
    )(o_lat, o_ctx, w_out, x, ctx, mods, g)


def _swiglu(x, wg_ref, wu_ref, wd_ref, f_dim, chunk):
    y = None
    for c0 in range(0, f_dim, chunk):
        cols = slice(c0, min(c0 + chunk, f_dim))
        a = _silu(_dot(x, wg_ref[:, cols])) * _dot(x, wu_ref[:, cols])
        part = _dot(a.astype(BF16), wd_ref[cols, :])
        y = part if y is None else y + part
    return y


def _ffn_kernel(x_ref, wg_ref, wu_ref, wd_ref, h_ref, m_ref, g_ref, o_ref):
    y = _swiglu(x_ref[...], wg_ref, wu_ref, wd_ref, FFN_DIM, MLP_CHUNK)
    o_ref[...] = h_ref[...] + m_ref[5:6, :] * _rms(y, g_ref[3:4, :])


def _resident(shape):
    return pl.BlockSpec(shape, lambda *_: (0,) * len(shape), pipeline_mode=pl.Buffered(1))


def _ffn(x, wg, wu, wd, h, mods, g):
    tm = MLP_ROW_TILE
    row = lambda n: pl.BlockSpec((tm, n), lambda i: (i, 0))
    return pl.pallas_call(
        _ffn_kernel,
        out_shape=jax.ShapeDtypeStruct((N_ROWS, D), F32),
        grid=(N_ROWS // tm,),
        in_specs=[row(D), _resident((D, FFN_DIM)), _resident((D, FFN_DIM)), _resident((FFN_DIM, D)),
                  row(D), _mod_spec(0, tm), _full((4, D))],
        out_specs=row(D),
        compiler_params=_params(("parallel",)),
        name="ffn_mlp",
    )(x, wg, wu, wd, h, mods, g)


N_ASSIGN = 2 * N_LAT
N_SORTED = N_ASSIGN + N_EXPERTS * MLP_ROW_TILE
N_SORTED_TILES = N_SORTED // MLP_ROW_TILE
ROUTE_TILE = 1024
DMA_UNROLL = 8
ROUTE_ROWS = 8


def _row_copy(src_ref, src_row, dst_ref, dst_row, sem):
    return pltpu.make_async_copy(src_ref.at[pl.ds(src_row, 1), :], dst_ref.at[pl.ds(dst_row, 1), :], sem)


def _dispatch_kernel(ends_ref, d1_ref, d2_ref, u_ref, xs_ref, zero_ref, sem, zero_sem):
    tm = MLP_ROW_TILE

    def zero_tile(e):
        rows = pl.ds(pl.multiple_of(ends_ref[e] - tm, tm), tm)
        return pltpu.make_async_copy(zero_ref, xs_ref.at[rows, :], zero_sem)

    def group_size(e):
        return ends_ref[e] - (ends_ref[e - 1] if e > 0 else 0)

    def tail_tile(k):
        rows = pl.ds(pl.multiple_of(ends_ref[N_EXPERTS - 1] + k * tm, tm), tm)
        return pltpu.make_async_copy(zero_ref, xs_ref.at[rows, :], zero_sem)

    def has_tail(k):
        return ends_ref[N_EXPERTS - 1] + k * tm < N_SORTED

    @pl.when(pl.program_id(0) == 0)
    def _():
        zero_ref[...] = jnp.zeros_like(zero_ref)
        for e in range(N_EXPERTS):
            pl.when(group_size(e) > 0)(lambda e=e: zero_tile(e).start())
            pl.when(has_tail(e))(lambda e=e: tail_tile(e).start())
        for e in range(N_EXPERTS):
            pl.when(group_size(e) > 0)(lambda e=e: zero_tile(e).wait())
            pl.when(has_tail(e))(lambda e=e: tail_tile(e).wait())

    def start(t, _):
        _row_copy(u_ref, t, xs_ref, d1_ref[0, t], sem).start()
        _row_copy(u_ref, t, xs_ref, d2_ref[0, t], sem).start()
        return 0

    def wait(t, _):
        _row_copy(u_ref, t, xs_ref, d1_ref[0, t], sem).wait()
        _row_copy(u_ref, t, xs_ref, d2_ref[0, t], sem).wait()
        return 0

    lax.fori_loop(0, ROUTE_TILE, start, 0, unroll=DMA_UNROLL)
    lax.fori_loop(0, ROUTE_TILE, wait, 0, unroll=DMA_UNROLL)


def _route_spec():
    return pl.BlockSpec((None, 1, ROUTE_TILE), lambda i, *_: (i, 0, 0), memory_space=pltpu.SMEM)


def _dispatch(ends, d1, d2, u):
    tm = ROUTE_TILE
    grid_spec = pltpu.PrefetchScalarGridSpec(
        num_scalar_prefetch=1,
        grid=(N_LAT // tm,),
        in_specs=[_route_spec(), _route_spec(), pl.BlockSpec((tm, D), lambda i, ends: (i, 0))],
        out_specs=pl.BlockSpec(memory_space=pl.ANY),
        scratch_shapes=[pltpu.VMEM((MLP_ROW_TILE, D), F32), pltpu.SemaphoreType.DMA(()),
                        pltpu.SemaphoreType.DMA(())],
    )
    return pl.pallas_call(
        _dispatch_kernel,
        out_shape=jax.ShapeDtypeStruct((N_SORTED, D), F32),
        grid_spec=grid_spec,
        compiler_params=_params(("arbitrary",)),
        name="moe_dispatch",
    )(ends, d1, d2, u)


def _experts_kernel(te_ref, nt_ref, x_ref, wg_ref, wu_ref, wd_ref, o_ref, xb_ref):
    del te_ref
    i, f = pl.program_id(0), pl.program_id(1)
    live = i < nt_ref[0]

    @pl.when(live & (f == 0))
    def _():
        xb_ref[...] = x_ref[...].astype(BF16)

    @pl.when(live)
    def _():
        y = _swiglu(xb_ref[...], wg_ref, wu_ref, wd_ref, EXPERT_TILE, MLP_CHUNK)

        @pl.when(f == 0)
        def _():
            o_ref[...] = y

        @pl.when(f > 0)
        def _():
            o_ref[...] += y

    @pl.when(jnp.logical_not(live) & (f == 0))
    def _():
        o_ref[...] = jnp.zeros_like(o_ref)


def _experts(tile_expert, n_tiles, xs, wg, wu, wd):
    tm = MLP_ROW_TILE
    nf = EXPERT_DIM // EXPERT_TILE

    def f_blk(i, f, nt):
        return jnp.where(i < nt[0], f, nf - 1)

    grid_spec = pltpu.PrefetchScalarGridSpec(
        num_scalar_prefetch=2,
        grid=(N_SORTED_TILES, nf),
        in_specs=[pl.BlockSpec((tm, D), lambda i, f, te, nt: (jnp.minimum(i, nt[0] - 1), 0)),
                  pl.BlockSpec((None, D, EXPERT_TILE), lambda i, f, te, nt: (te[i], 0, f_blk(i, f, nt))),
                  pl.BlockSpec((None, D, EXPERT_TILE), lambda i, f, te, nt: (te[i], 0, f_blk(i, f, nt))),
                  pl.BlockSpec((None, EXPERT_TILE, D), lambda i, f, te, nt: (te[i], f_blk(i, f, nt), 0))],
        out_specs=pl.BlockSpec((tm, D), lambda i, f, te, nt: (i, 0)),
        scratch_shapes=[pltpu.VMEM((tm, D), BF16)],
    )
    return pl.pallas_call(
        _experts_kernel,
        out_shape=jax.ShapeDtypeStruct((N_SORTED, D), F32),
        grid_spec=grid_spec,
        compiler_params=_params(("arbitrary", "arbitrary")),
        name="moe_experts",
    )(tile_expert, n_tiles, xs, wg, wu, wd)


def _combine_kernel(d1_ref, d2_ref, n1_ref, n2_ref, w_ref, h_ref, m_ref, g_ref, ys_ref, o_ref, buf_ref, sems):
    i, n = pl.program_id(0), pl.num_programs(0)
    slot = i % 2

    def gather(a_ref, b_ref, slot, start):
        def body(t, _):
            for choice, d_ref in enumerate((a_ref, b_ref)):
                copy = _row_copy(ys_ref, d_ref[0, t], buf_ref.at[slot, choice], t, sems.at[slot])
                copy.start() if start else copy.wait()
            return 0

        lax.fori_loop(0, ROUTE_TILE, body, 0, unroll=DMA_UNROLL)

    @pl.when(i == 0)
    def _():
        gather(d1_ref, d2_ref, 0, True)

    @pl.when(i + 1 < n)
    def _():
        gather(n1_ref, n2_ref, 1 - slot, True)

    gather(d1_ref, d2_ref, slot, False)
    w = w_ref[...]
    lane = lax.broadcasted_iota(jnp.int32, w.shape, 1)
    w1 = jnp.sum(jnp.where(lane == 0, w, 0.0), axis=-1, keepdims=True)
    w2 = jnp.sum(jnp.where(lane == 1, w, 0.0), axis=-1, keepdims=True)
    y = w1 * buf_ref[slot, 0] + w2 * buf_ref[slot, 1]
    o_ref[...] = h_ref[...] + m_ref[5:6, :] * _rms(y, g_ref[3:4, :])


def _combine(d1, d2, wts, h, mods, g, ys):
    tm = ROUTE_TILE
    n = N_LAT // tm
    row = lambda w: pl.BlockSpec((tm, w), lambda i: (i, 0))
    nxt = pl.BlockSpec((None, 1, tm), lambda i: (jnp.minimum(i + 1, n - 1), 0, 0), memory_space=pltpu.SMEM)
    return pl.pallas_call(
        _combine_kernel,
        out_shape=jax.ShapeDtypeStruct((N_LAT, D), F32),
        grid=(n,),
        in_specs=[_route_spec(), _route_spec(), nxt, nxt, row(LANE), row(D), _mod_spec(1, tm), _full((4, D)),
                  pl.BlockSpec(memory_space=pl.ANY)],
        out_specs=row(D),
        scratch_shapes=[pltpu.VMEM((2, 2, tm, D), F32), pltpu.SemaphoreType.DMA((2,))],
        compiler_params=_params(("arbitrary",)),
        name="moe_combine",
    )(d1, d2, d1, d2, wts, h, mods, g, ys)


def _routing(route, counts):
    tm = MLP_ROW_TILE
    experts = jnp.arange(N_EXPERTS, dtype=jnp.int32)
    padded = (counts[0, :N_EXPERTS].astype(jnp.int32) + tm - 1) // tm * tm
    ends = jnp.cumsum(padded).astype(jnp.int32)
    starts = ends - padded

    def dest(choice, rank):
        return jnp.sum(jnp.where(choice[:, None] == experts, starts[None, :], 0), axis=1) + rank

    d1, d2 = dest(route[0], route[2]), dest(route[1], route[3])
    n_tiles = ends[-1] // tm
    tile_start = jnp.minimum(jnp.arange(N_SORTED_TILES, dtype=jnp.int32), n_tiles - 1) * tm
    tile_expert = jnp.sum((tile_start[:, None] >= ends[None, :]).astype(jnp.int32), axis=1)
    shape = (N_LAT // ROUTE_TILE, 1, ROUTE_TILE)
    return (ends, d1.astype(jnp.int32).reshape(shape), d2.astype(jnp.int32).reshape(shape),
            tile_expert.astype(jnp.int32), n_tiles.astype(jnp.int32).reshape(1))


GLA_Q0, GLA_K0, GLA_V0, GLA_G0, GLA_R0 = 0, 512, 1024, 2048, 3072
GLA_IN_P = GLA_R0 + LANE


def _gla_proj_kernel(h_ref, m_ref, g_ref, w_in_ref, w_vt_ref, q_ref, k_ref, v_ref, vt_ref, sg_ref, r_ref):
    u = _modulated(h_ref[...], g_ref[0:1, :], m_ref[0:1, :], m_ref[1:2, :]).astype(BF16)
    z = _dot(u, w_in_ref[...])
    q_ref[...] = (z[:, GLA_Q0:GLA_K0] * (GLA_DK ** -0.5)).astype(BF16)
    k_ref[...] = z[:, GLA_K0:GLA_V0].astype(BF16)
    v_ref[...] = z[:, GLA_V0:GLA_G0].astype(BF16)
    vt_ref[...] = _dot_nt(w_vt_ref[...], u).astype(BF16)
    sg_ref[...] = _silu(z[:, GLA_G0:GLA_R0]).astype(BF16)
    r_ref[...] = z[:, GLA_R0:].astype(BF16)


def _gla_project(h, mods, g, w):
    tm = ROW_TILE
    row = lambda n, dt=None: pl.BlockSpec((tm, n), lambda i: (i, 0))
    hk, hv = GLA_HEADS * GLA_DK, GLA_HEADS * GLA_DV
    sds = jax.ShapeDtypeStruct
    return pl.pallas_call(
        _gla_proj_kernel,
        out_shape=(sds((N_ROWS, hk), BF16), sds((N_ROWS, hk), BF16), sds((N_ROWS, hv), BF16),
                   sds((hv, N_ROWS), BF16), sds((N_ROWS, hv), BF16), sds((N_ROWS, LANE), BF16)),
        grid=(N_ROWS // tm,),
        in_specs=[row(D), _mod_spec(1, tm), _full((4, D)), _full((D, GLA_IN_P)), _full((hv, D))],
        out_specs=(row(hk), row(hk), row(hv), pl.BlockSpec((hv, tm), lambda i: (0, i)), row(hv), row(LANE)),
        compiler_params=_params(("parallel",)),
        name="gla_project",
    )(h, mods, g, w["w_in"], w["w_vt"])


GLA_LAT_BLOCKS = SEQ // GLA_BLOCK
GLA_CTX_BLOCKS = CTX // GLA_BLOCK
GLA_STRIPS = GLA_BLOCK // GLA_SUB


def _gla_blocks(chains):
    n = GLA_BLOCK
    r = lax.broadcasted_iota(jnp.int32, (n, n), 0)
    c = lax.broadcasted_iota(jnp.int32, (n, n), 1)
    row = lax.broadcasted_iota(jnp.int32, (n, GLA_DK), 0)
    keep = {rev: (c >= r) if rev else (c <= r) for rev in (False, True)}
    tri = {rev: jnp.where(keep[rev], 1.0, 0.0).astype(BF16) for rev in (False, True)}

    cums = []
    for ch in chains:
        parts = _dot(tri[ch["reverse"]], jnp.concatenate(_split_bf16(ch["la"]), axis=1))
        cums.append(parts[:, :GLA_DK] + parts[:, GLA_DK:])

    states, outs = [], []
    for ch, cum in zip(chains, cums):
        total = cum[0:1, :] if ch["reverse"] else cum[n - 1:n, :]
        k_state = (ch["k"] * jnp.exp(total - cum)).astype(BF16)
        states.append(ch["state"] * jnp.exp(total) + _dot(ch["vt"], k_state))
        if ch["q"] is None:
            outs.append(None)
        else:
            outs.append(_dot_nt((ch["q"] * jnp.exp(cum)).astype(BF16), ch["state"].astype(BF16)))

    strips = [[] for _ in chains]
    for i in range(GLA_STRIPS):
        lo, hi = i * GLA_SUB, (i + 1) * GLA_SUB
        for ci, (ch, cum) in enumerate(zip(chains, cums)):
            if ch["q"] is None:
                continue
            if ch["reverse"]:
                ref = cum[hi:hi + 1, :] if i < GLA_STRIPS - 1 else jnp.zeros((1, GLA_DK), F32)
                live = row >= lo
            else:
                ref = cum[lo - 1:lo, :] if i > 0 else jnp.zeros((1, GLA_DK), F32)
                live = row < hi
            q_loc = (ch["q"][lo:hi, :] * jnp.exp(cum[lo:hi, :] - ref)).astype(BF16)
            k_loc = jnp.where(live, ch["k"] * jnp.exp(ref - cum), 0.0).astype(BF16)
            strips[ci].append(_dot_nt(q_loc, k_loc))

    for ci, ch in enumerate(chains):
        if ch["q"] is not None:
            scores = jnp.where(keep[ch["reverse"]], jnp.concatenate(strips[ci], axis=0), 0.0).astype(BF16)
            outs[ci] = outs[ci] + _dot(scores, ch["v"])
    return list(zip(outs, states))


GLA_GROUP = 2


def _gla_kernel(q_ref, kl_ref, kc_ref, vl_ref, vtl_ref, vtc_ref, rl_ref, rc_ref,
                wgf_ref, wgb_ref, bgf_ref, bgb_ref, o_ref, s_ref):
    n = GLA_BLOCK
    s_ref[...] = jnp.zeros_like(s_ref)
    chains = [(hd, rev) for hd in range(GLA_GROUP) for rev in (False, True)]

    def log_decays(r_blk, reverse):
        wg_ref, bg_ref = (wgb_ref, bgb_ref) if reverse else (wgf_ref, bgf_ref)
        zg = _dot(r_blk, wg_ref[...]) + bg_ref[...]
        return (jnp.minimum(zg, 0.0) - jnp.log(1.0 + jnp.exp(-jnp.abs(zg)))) * (1.0 / GLA_GATE_NORM)

    def chain(hd, reverse, rows, la, k_ref, vt_ref, with_output):
        ks = slice(hd * GLA_DK, (hd + 1) * GLA_DK)
        vs = slice(hd * GLA_DV, (hd + 1) * GLA_DV)
        return dict(reverse=reverse, k=k_ref[rows, ks].astype(F32), vt=vt_ref[vs, rows],
                    la=la[:, ks], state=s_ref[int(reverse), hd],
                    q=q_ref[rows, ks].astype(F32) if with_output else None,
                    v=vl_ref[rows, vs] if with_output else None)

    for j in range(GLA_CTX_BLOCKS):
        rows = {rev: slice(blk * n, (blk + 1) * n) for rev, blk in ((False, j), (True, GLA_CTX_BLOCKS - 1 - j))}
        la = {rev: log_decays(rc_ref[rows[rev], :], rev) for rev in (False, True)}
        work = [chain(hd, rev, rows[rev], la[rev], kc_ref, vtc_ref, False) for hd, rev in chains]
        for (hd, reverse), (_, s_new) in zip(chains, _gla_blocks(work)):
            s_ref[int(reverse), hd] = s_new

    def step(j, accumulate):
        rows = {rev: pl.ds(pl.multiple_of(blk * n, n), n) for rev, blk in ((False, j), (True, GLA_LAT_BLOCKS - 1 - j))}
        la = {rev: log_decays(rl_ref[rows[rev], :], rev) for rev in (False, True)}
        work = [chain(hd, rev, rows[rev], la[rev], kl_ref, vtl_ref, True) for hd, rev in chains]
        where = [(rows[rev], slice(hd * GLA_DV, (hd + 1) * GLA_DV)) for hd, rev in chains]
        for (hd, reverse), (rows, vs), (o, s_new) in zip(chains, where, _gla_blocks(work)):
            s_ref[int(reverse), hd] = s_new
            if accumulate:
                o_ref[rows, vs] += o
            else:
                o_ref[rows, vs] = o

    half = GLA_LAT_BLOCKS // 2

    def first(j, _):
        step(j, False)
        return 0

    def second(j, _):
        step(j, True)
        return 0

    lax.fori_loop(0, half, first, 0)
    lax.fori_loop(half, GLA_LAT_BLOCKS, second, 0)


def _gla_scan(q, k, v, vt, r, w_gate_f, w_gate_b, b_gate_f, b_gate_b):
    gk, gv = GLA_GROUP * GLA_DK, GLA_GROUP * GLA_DV
    lat = lambda n: pl.BlockSpec((SEQ, n), lambda b, g: (b, g))
    ctx = lambda n: pl.BlockSpec((CTX, n), lambda b, g: (N_LAT // CTX + b, g))
    grp = lambda rows: pl.BlockSpec((rows, gk), lambda b, g: (0, g))
    return pl.pallas_call(
        _gla_kernel,
        out_shape=jax.ShapeDtypeStruct((N_LAT, GLA_HEADS * GLA_DV), F32),
        grid=(BATCH, GLA_HEADS // GLA_GROUP),
        in_specs=[lat(gk), lat(gk), ctx(gk), lat(gv),
                  pl.BlockSpec((gv, SEQ), lambda b, g: (g, b)),
                  pl.BlockSpec((gv, CTX), lambda b, g: (g, N_LAT // CTX + b)),
                  pl.BlockSpec((SEQ, LANE), lambda b, g: (b, 0)),
                  pl.BlockSpec((CTX, LANE), lambda b, g: (N_LAT // CTX + b, 0)),
                  grp(LANE), grp(LANE), grp(1), grp(1)],
        out_specs=lat(gv),
        scratch_shapes=[pltpu.VMEM((2, GLA_GROUP, GLA_DV, GLA_DK), F32)],
        compiler_params=_params(("parallel", "parallel")),
        name="gla_scan",
    )(q, k, k, v, vt, vt, r, r, w_gate_f, w_gate_b, b_gate_f, b_gate_b)


def _gla_out_kernel(o_ref, sg_ref, on_ref, w_ref, h_ref, m_ref, g_ref, wr_hi_ref, wr_lo_ref, br_ref,
                    h_out_ref, u_ref, route_ref, wts_ref, count_ref):
    parts = []
    for hd in range(GLA_HEADS):
        sl = slice(hd * GLA_DV, (hd + 1) * GLA_DV)
        parts.append(_rms(o_ref[:, sl], on_ref[...]) * sg_ref[:, sl].astype(F32))
    y = _dot(jnp.concatenate(parts, axis=-1).astype(BF16), w_ref[...])
    h_new, u = _mixer_epilogue(y, h_ref[...], m_ref, g_ref)
    h_out_ref[...] = h_new
    u_ref[...] = u
    logits = _dot3(u, wr_hi_ref[...], wr_lo_ref[...]) + br_ref[...]
    lane = lax.broadcasted_iota(jnp.int32, logits.shape, 1)
    l1 = jnp.max(logits, axis=-1, keepdims=True)
    i1 = jnp.min(jnp.where(logits == l1, lane, LANE), axis=-1, keepdims=True)
    rest = jnp.where(lane == i1, -jnp.inf, logits)
    l2 = jnp.max(rest, axis=-1, keepdims=True)
    i2 = jnp.min(jnp.where(rest == l2, lane, LANE), axis=-1, keepdims=True)
    e2 = jnp.exp(l2 - l1)
    wts_ref[...] = jnp.where(lane == 0, 1.0 / (1.0 + e2), jnp.where(lane == 1, e2 / (1.0 + e2), 0.0))

    @pl.when(pl.program_id(0) == 0)
    def _():
        count_ref[...] = jnp.zeros_like(count_ref)

    chosen = jnp.where((lane == i1) | (lane == i2), 1.0, 0.0)
    tm = chosen.shape[0]
    earlier = lax.broadcasted_iota(jnp.int32, (tm, tm), 1) < lax.broadcasted_iota(jnp.int32, (tm, tm), 0)
    rank = count_ref[0:1, :] + _dot(jnp.where(earlier, 1.0, 0.0).astype(BF16), chosen.astype(BF16))
    rank1 = jnp.sum(jnp.where(lane == i1, rank, 0.0), axis=-1, keepdims=True).astype(jnp.int32)
    rank2 = jnp.sum(jnp.where(lane == i2, rank, 0.0), axis=-1, keepdims=True).astype(jnp.int32)
    count_ref[...] = count_ref[...] + jnp.sum(chosen, axis=0, keepdims=True)
    route = jnp.where(lane == 0, i1, jnp.where(lane == 1, i2,
                      jnp.where(lane == 2, rank1, jnp.where(lane == 3, rank2, 0))))
    route_ref[...] = route.T[0:ROUTE_ROWS, :]


def _gla_output(o, sg, o_norm, w_out, h, mods, g, wr_hi, wr_lo, br):
    tm = ROW_TILE
    row = lambda n: pl.BlockSpec((tm, n), lambda i: (i, 0))
    hv = GLA_HEADS * GLA_DV
    return pl.pallas_call(
        _gla_out_kernel,
        out_shape=(jax.ShapeDtypeStruct((N_LAT, D), F32), jax.ShapeDtypeStruct((N_LAT, D), F32),
                   jax.ShapeDtypeStruct((ROUTE_ROWS, N_LAT), jnp.int32), jax.ShapeDtypeStruct((N_LAT, LANE), F32),
                   jax.ShapeDtypeStruct((8, LANE), F32)),
        grid=(N_LAT // tm,),
        in_specs=[row(hv), row(hv), _full((1, GLA_DV)), _full((hv, D)), row(D), _mod_spec(1, tm),
                  _full((4, D)), _full((D, LANE)), _full((D, LANE)), _full((1, LANE))],
        out_specs=(row(D), row(D), pl.BlockSpec((ROUTE_ROWS, tm), lambda i: (0, i)), row(LANE),
                   _full((8, LANE))),
        compiler_params=_params(("arbitrary",)),
        name="gla_output",
    )(o, sg, o_norm, w_out, h, mods, g, wr_hi, wr_lo, br)


def _rotary_lanes(d):
    half, p = d // 2, d // 4
    lanes = np.empty(d, np.int64)
    for axis in range(2):
        lanes[axis * half:axis * half + p] = axis * p + np.arange(p)
        lanes[axis * half + p:(axis + 1) * half] = LANE // 2 + axis * p + np.arange(p)
    return lanes


def _slot_sources(n_plain, n_rotary):
    src = np.full(LANE, -1, np.int64)
    src[_rotary_lanes(n_rotary)] = n_plain + np.arange(n_rotary)
    free = np.flatnonzero(src < 0)[:n_plain]
    src[free] = np.arange(n_plain)
    return src


def _to_slots(w, n_heads, src):
    k = w.shape[0]
    w = jnp.pad(w.reshape(k, n_heads, -1), ((0, 0), (0, 0), (0, 1)))
    return w[:, :, np.where(src < 0, w.shape[2] - 1, src)].reshape(k, n_heads * LANE)


def _attn_weights(w_in, q_norm, w_uq, kv_norm, w_ukv, qk_norm, w_out):
    c = 0
    cols = {}
    for name, n in (("q_lat", MLA_Q_RANK), ("kv_lat", MLA_KV_RANK), ("k_rope", MLA_ROPE),
                    ("q_b", GQA_HEADS * GQA_HEAD_DIM), ("k_b", GQA_KV_HEADS * GQA_HEAD_DIM),
                    ("v_b", GQA_KV_HEADS * GQA_HEAD_DIM)):
        cols[name] = w_in[:, c:c + n]
        c += n
    src_a = _slot_sources(MLA_NOPE, MLA_ROPE)
    src_a_nope = np.where(src_a < MLA_NOPE, src_a, -1)
    src_a_rope = np.where(src_a >= MLA_NOPE, src_a - MLA_NOPE, -1)
    src_b = _slot_sources(0, GQA_HEAD_DIM)
    w_in_p = jnp.concatenate([cols["q_lat"], cols["kv_lat"], _to_slots(cols["k_rope"], 1, src_a_rope),
                              _to_slots(cols["q_b"], GQA_HEADS, src_b),
                              _to_slots(cols["k_b"], GQA_KV_HEADS, src_b)], axis=1)
    ukv = w_ukv.reshape(MLA_KV_RANK, MLA_HEADS, MLA_NOPE + MLA_V)
    w_uk = _to_slots(ukv[:, :, :MLA_NOPE].reshape(MLA_KV_RANK, -1), MLA_HEADS, src_a_nope)
    w_uv = ukv[:, :, MLA_NOPE:].reshape(MLA_KV_RANK, MLA_HEADS * MLA_V)
    return {
        "w_in": w_in_p.astype(BF16),
        "q_norm": q_norm.reshape(1, -1),
        "w_uq": _to_slots(w_uq, MLA_HEADS, src_a).astype(BF16),
        "kv_norm": kv_norm.reshape(1, -1),
        "w_uk": w_uk.astype(BF16),
        "w_uvt": w_uv.T.astype(BF16),
        "w_vbt": cols["v_b"].T.astype(BF16),
        "qk_norm": _to_slots(qk_norm, 1, src_b),
        "w_out": w_out.astype(BF16),
    }


def _rope_tables():
    t = jnp.arange(SEQ, dtype=jnp.int32)
    out = []
    for d in (MLA_ROPE, GQA_HEAD_DIM):
        half, p = d // 2, d // 4
        freqs = ROPE_THETA ** (-jnp.arange(0, half, 2, dtype=F32) / half)
        ang = jnp.concatenate([pos.astype(F32)[:, None] * freqs[None, :] for pos in (t // GRID_W, t % GRID_W)], axis=1)
        rest = LANE // 2 - 2 * p
        cos = jnp.pad(jnp.cos(ang), ((0, 0), (0, rest)), constant_values=1.0)
        sin = jnp.pad(jnp.sin(ang), ((0, 0), (0, rest)))
        for tbl, fill in ((jnp.concatenate([cos, cos], axis=1), 1.0), (jnp.concatenate([-sin, sin], axis=1), 0.0)):
            out.append(jnp.pad(tbl, ((0, ATTN_PROJ_TILE), (0, 0)), constant_values=fill))
    return out


def _gla_weights(w_in, w_gate2, b_gate):
    hk, hv = GLA_HEADS * GLA_DK, GLA_HEADS * GLA_DV
    r = jnp.pad(w_in[:, 2 * hk + 2 * hv:], ((0, 0), (0, LANE - 2 * GLA_GATE_RANK)))
    w_in_p = jnp.concatenate([w_in[:, :2 * hk + 2 * hv], r], axis=1)
    pad_f = ((0, LANE - GLA_GATE_RANK), (0, 0))
    pad_b = ((GLA_GATE_RANK, LANE - 2 * GLA_GATE_RANK), (0, 0))
    return {
        "w_in": w_in_p.astype(BF16),
        "w_vt": w_in[:, 2 * hk:2 * hk + hv].T.astype(BF16),
        "w_gate_f": jnp.pad(w_gate2[0], pad_f).astype(BF16),
        "w_gate_b": jnp.pad(w_gate2[1], pad_b).astype(BF16),
        "b_gate_f": b_gate[0].reshape(1, hk),
        "b_gate_b": b_gate[1].reshape(1, hk),
    }


def kernel(x, c, ctx, c_ctx, mod_w, mod_b, norm_g, attn_w_in, attn_q_norm, attn_w_uq, attn_kv_norm, attn_w_ukv,
           attn_qk_norm, attn_w_out, gla_w_in, gla_w_gate2, gla_b_gate, gla_o_norm, gla_w_out, ffn_w_gate,
           ffn_w_up, ffn_w_down, moe_w_router, moe_b_router, moe_w_gate, moe_w_up, moe_w_down):
    assert x.shape == (BATCH, SEQ, D) and ctx.shape == (BATCH, CTX, D)
    x, ctx = x.reshape(N_LAT, D), ctx.reshape(N_CTX, D)
    cc = jnp.concatenate([c, c_ctx[None, :], jnp.zeros((MOD_ROWS - BATCH - 1, D), F32)], axis=0)
    mods = _mod_vectors(cc, mod_w, mod_b).reshape(mod_w.shape[0], MOD_ROWS, 6, D)

    aw = _attn_weights(attn_w_in[0], attn_q_norm[0], attn_w_uq[0], attn_kv_norm[0], attn_w_ukv[0],
                       attn_qk_norm[0], attn_w_out[0])
    q, k, vt = _attn_project(x, ctx, mods, norm_g[0], aw, _rope_tables())
    o_lat, o_ctx = _attention(q, k, vt)
    h, u = _attn_output(o_lat, o_ctx, aw["w_out"], x, ctx, mods, norm_g[0])
    h = _ffn(u, ffn_w_gate[0].astype(BF16), ffn_w_up[0].astype(BF16), ffn_w_down[0].astype(BF16), h, mods,
             norm_g[0])

    gw = _gla_weights(gla_w_in[0], gla_w_gate2[0], gla_b_gate[0])
    gq, gk, gv, gvt, sg, gr = _gla_project(h, mods, norm_g[1], gw)
    go = _gla_scan(gq, gk, gv, gvt, gr, gw["w_gate_f"], gw["w_gate_b"], gw["b_gate_f"], gw["b_gate_b"])
    wr = jnp.pad(moe_w_router[0], ((0, 0), (0, LANE - N_EXPERTS)))
    wr_hi = wr.astype(BF16)
    wr_lo = (wr - wr_hi.astype(F32)).astype(BF16)
    br = jnp.pad(moe_b_router[0], (0, LANE - N_EXPERTS), constant_values=-jnp.inf).reshape(1, LANE)
    h, u, route, wts, counts = _gla_output(go, sg, gla_o_norm[0].reshape(1, GLA_DV), gla_w_out[0].astype(BF16),
                                           h, mods, norm_g[1], wr_hi, wr_lo, br)
    ends, d1, d2, tile_expert, n_tiles = _routing(route, counts)
    xs = _dispatch(ends, d1, d2, u)
    ys = _experts(tile_expert, n_tiles, xs, moe_w_gate[0].astype(BF16), moe_w_up[0].astype(BF16),
                  moe_w_down[0].astype(BF16))
    h = _combine(d1, d2, wts, h, mods, norm_g[1], ys)
    return h.reshape(BATCH, SEQ, D)
```
